```python
import jax
import jax.numpy as jnp
from jax import lax
import numpy as np


D_MODEL = 2048
BATCH = 16
SEQ = 2048
DEPTH = 2

HEAD_DIM = 128
N_BRANCH = 4
BRANCH_HEADS = 4
BRANCH_WIDTH = BRANCH_HEADS * HEAD_DIM

MOBA_BLOCK = 256
MOBA_TOPK = 3
MOBA_QCHUNK = 16
ROPE_THETA = 10000.0

MLSTM_CHUNK = 128
MLSTM_CONV = 4

GLA_HEADS = 4
GLA_DK = 64
GLA_DV = 128
GLA_RANK = 16
GLA_TAU = 16.0
GLA_CHUNK = 64

FOX_QBLOCK = 128

N_EXPERTS = 32
TOP_K = 4
D_EXPERT = D_MODEL // 2
SWIGLU_LIMIT = 7.0
SWIGLU_ALPHA = 1.702
MOE_ROW_BLOCK = 512

RMS_EPS = 1e-6

IN_WIDTHS = (
    BRANCH_WIDTH, BRANCH_WIDTH, BRANCH_WIDTH,
    2 * BRANCH_WIDTH, BRANCH_WIDTH, BRANCH_HEADS, BRANCH_HEADS,
    BRANCH_WIDTH,
    GLA_HEADS * GLA_DK, GLA_HEADS * GLA_DK, GLA_HEADS * GLA_DV,
    GLA_RANK, GLA_HEADS * GLA_DV,
    BRANCH_WIDTH, BRANCH_WIDTH, BRANCH_WIDTH, BRANCH_HEADS,
    N_BRANCH * D_MODEL,
)
D_IN = sum(IN_WIDTHS)

kernel_name = 'hybrid_moba_mlstm_gla_fox_moe_block'


def rmsnorm(x, w):
    xf = x.astype(jnp.float32)
    y = xf * lax.rsqrt(jnp.mean(xf * xf, axis=-1, keepdims=True) + RMS_EPS)
    return y.astype(x.dtype) * w


def head_rmsnorm(x, w):
    xf = x.astype(jnp.float32)
    y = xf * lax.rsqrt(jnp.mean(xf * xf, axis=-1, keepdims=True) + RMS_EPS)
    return y.astype(x.dtype) * w[None, :, None, :]


def to_heads(a, n):
    b, s, _ = a.shape
    return a.reshape(b, s, n, -1).transpose(0, 2, 1, 3)


def from_heads(a):
    b, h, s, d = a.shape
    return a.transpose(0, 2, 1, 3).reshape(b, s, h * d)


def rope(x, pos):
    half = x.shape[-1] // 2
    inv = ROPE_THETA ** (-jnp.arange(half, dtype=jnp.float32) / half)
    ang = pos.astype(jnp.float32)[:, None] * inv[None, :]
    cos = jnp.cos(ang).astype(x.dtype)
    sin = jnp.sin(ang).astype(x.dtype)
    x1, x2 = x[..., :half], x[..., half:]
    return jnp.concatenate([x1 * cos - x2 * sin, x2 * cos + x1 * sin], axis=-1)


def causal_conv(x, w, b):
    k, ch = w.shape
    y = lax.conv_general_dilated(x, w[:, None, :], window_strides=(1,), padding=[(k - 1, 0)],
                                 dimension_numbers=('NWC', 'WIO', 'NWC'), feature_group_count=ch)
    return y + b


def moba_attention(q, k, v):
    B, H, S, Dh = q.shape
    nblk = -(-S // MOBA_BLOCK)
    s_pad = nblk * MOBA_BLOCK
    pad = ((0, 0), (0, 0), (0, s_pad - S), (0, 0))
    q, k, v = jnp.pad(q, pad), jnp.pad(k, pad), jnp.pad(v, pad)
    kb = k.reshape(B, H, nblk, MOBA_BLOCK, Dh)
    vb = v.reshape(B, H, nblk, MOBA_BLOCK, Dh)
    k_mean = jnp.mean(kb.astype(jnp.float32), axis=3)
    gate = jnp.einsum('bhtd,bhnd->bhtn', q.astype(jnp.float32), k_mean)
    q_blk = jnp.arange(s_pad) // MOBA_BLOCK
    fully_past = jnp.arange(nblk)[None, :] < q_blk[:, None]
    gate = jnp.where(fully_past, gate, -jnp.inf)
    n_sel = max(1, min(MOBA_TOPK, nblk - 1))
    _, sel = lax.top_k(gate, n_sel)
    nq = s_pad // MOBA_QCHUNK
    q_chunks = jnp.moveaxis(q.reshape(B, H, nq, MOBA_QCHUNK, Dh), 2, 0)
    sel_chunks = jnp.moveaxis(sel.reshape(B, H, nq, MOBA_QCHUNK, n_sel), 2, 0)
    starts = jnp.arange(nq, dtype=jnp.int32) * MOBA_QCHUNK
    b_ix = jnp.arange(B)[:, None, None, None]
    h_ix = jnp.arange(H)[None, :, None, None]
    scale = Dh ** -0.5
    n_past = n_sel * MOBA_BLOCK

    def chunk(args):
        qc, sc, t0 = args
        tq = t0 + jnp.arange(MOBA_QCHUNK)
        k_sel = kb[b_ix, h_ix, sc]
        v_sel = vb[b_ix, h_ix, sc]
        s_past = jnp.einsum('bhqd,bhqjkd->bhqjk', qc, k_sel).astype(jnp.float32) * scale
        valid = jnp.arange(n_sel)[None, :] < (tq // MOBA_BLOCK)[:, None]
        s_past = jnp.where(valid[None, None, :, :, None], s_past, -jnp.inf)
        own = t0 // MOBA_BLOCK
        k_own = lax.dynamic_index_in_dim(kb, own, axis=2, keepdims=False)
        v_own = lax.dynamic_index_in_dim(vb, own, axis=2, keepdims=False)
        s_own = jnp.einsum('bhqd,bhkd->bhqk', qc, k_own).astype(jnp.float32) * scale
        k_pos = own * MOBA_BLOCK + jnp.arange(MOBA_BLOCK)
        s_own = jnp.where(k_pos[None, :] <= tq[:, None], s_own, -jnp.inf)
        s_all = jnp.concatenate([s_past.reshape(B, H, MOBA_QCHUNK, n_past), s_own], axis=-1)
        p = jax.nn.softmax(s_all, axis=-1).astype(v.dtype)
        p_past = p[..., :n_past].reshape(B, H, MOBA_QCHUNK, n_sel, MOBA_BLOCK)
        return (jnp.einsum('bhqjk,bhqjkd->bhqd', p_past, v_sel)
                + jnp.einsum('bhqk,bhkd->bhqd', p[..., n_past:], v_own))

    out = lax.map(chunk, (q_chunks, sel_chunks, starts))
    out = jnp.moveaxis(out, 0, 2).reshape(B, H, s_pad, Dh)
    return out[:, :, :S]


def mlstm_chunkwise(q, k, v, i_pre, f_pre):
    B, H, S, Dh = q.shape
    L = MLSTM_CHUNK
    nc = S // L
    q = q.astype(jnp.float32)
    k = k.astype(jnp.float32) * (Dh ** -0.5)
    v = v.astype(jnp.float32)
    ig = i_pre.astype(jnp.float32)
    lf = jax.nn.log_sigmoid(f_pre.astype(jnp.float32))

    def to_chunks(a):
        return jnp.moveaxis(a.reshape(B, H, nc, L, *a.shape[3:]), 2, 0)

    causal = jnp.tril(jnp.ones((L, L), dtype=bool))

    def step(carry, xs):
        C, n, m = carry
        qc, kc, vc, ic, fc = xs
        g = jnp.cumsum(fc, axis=-1)
        dmat = jnp.where(causal, g[..., :, None] - g[..., None, :] + ic[..., None, :], -jnp.inf)
        a = g + m[..., None]
        m_out = jnp.maximum(a, jnp.max(dmat, axis=-1))
        w = jnp.exp(dmat - m_out[..., None]) * jnp.einsum('bhtd,bhsd->bhts', qc, kc)
        inter = jnp.exp(a - m_out)
        num = inter[..., None] * jnp.einsum('bhtd,bhde->bhte', qc, C) + jnp.einsum('bhts,bhse->bhte', w, vc)
        den = inter * jnp.einsum('bhtd,bhd->bht', qc, n) + jnp.sum(w, axis=-1)
        h = num / jnp.maximum(jnp.abs(den), jnp.exp(-m_out))[..., None]
        g_last = g[..., -1]
        lw = g_last[..., None] - g + ic
        m_new = jnp.maximum(g_last + m, jnp.max(lw, axis=-1))
        wk = jnp.exp(lw - m_new[..., None])
        decay = jnp.exp(g_last + m - m_new)
        C = decay[..., None, None] * C + jnp.einsum('bhsd,bhse->bhde', kc * wk[..., None], vc)
        n = decay[..., None] * n + jnp.einsum('bhs,bhsd->bhd', wk, kc)
        return (C, n, m_new), h

    init = (jnp.zeros((B, H, Dh, Dh), jnp.float32), jnp.zeros((B, H, Dh), jnp.float32),
            jnp.zeros((B, H), jnp.float32))
    _, hs = lax.scan(step, init, (to_chunks(q), to_chunks(k), to_chunks(v), to_chunks(ig), to_chunks(lf)))
    return jnp.moveaxis(hs, 0, 2).reshape(B, H, S, Dh)


def gla_chunkwise(q, k, v, log_a):
    B, H, S, dk = q.shape
    dv = v.shape[-1]
    L = GLA_CHUNK
    nc = S // L
    q = q.astype(jnp.float32) * (dk ** -0.5)
    k = k.astype(jnp.float32)
    v = v.astype(jnp.float32)

    def to_chunks(a):
        return jnp.moveaxis(a.reshape(B, H, nc, L, a.shape[-1]), 2, 0)

    causal = jnp.tril(jnp.ones((L, L), dtype=bool))[:, :, None]

    def step(state, xs):
        qc, kc, vc, lac = xs
        b = jnp.cumsum(lac, axis=2)
        o_inter = jnp.einsum('bhtd,bhde->bhte', qc * jnp.exp(b), state)
        rel = jnp.where(causal, b[:, :, :, None, :] - b[:, :, None, :, :], -jnp.inf)
        attn = jnp.einsum('bhtd,bhsd,bhtsd->bhts', qc, kc, jnp.exp(rel))
        o = o_inter + jnp.einsum('bhts,bhse->bhte', attn, vc)
        b_last = b[:, :, -1, :]
        state = (jnp.exp(b_last)[..., None] * state
                 + jnp.einsum('bhsd,bhse->bhde', kc * jnp.exp(b_last[:, :, None, :] - b), vc))
        return state, o

    init = jnp.zeros((B, H, dk, dv), jnp.float32)
    _, os_ = lax.scan(step, init, (to_chunks(q), to_chunks(k), to_chunks(v), to_chunks(log_a)))
    return jnp.moveaxis(os_, 0, 2).reshape(B, H, S, dv)


def forgetting_attention(q, k, v, logf):
    B, H, S, Dh = q.shape
    nb = S // FOX_QBLOCK
    F = jnp.cumsum(logf.astype(jnp.float32), axis=-1)
    q_blocks = jnp.moveaxis(q.reshape(B, H, nb, FOX_QBLOCK, Dh), 2, 0)
    F_blocks = jnp.moveaxis(F.reshape(B, H, nb, FOX_QBLOCK), 2, 0)
    starts = jnp.arange(nb, dtype=jnp.int32) * FOX_QBLOCK
    k_pos = jnp.arange(S)
    scale = Dh ** -0.5

    def block(args):
        qb, Fb, t0 = args
        s = jnp.einsum('bhqd,bhkd->bhqk', qb, k).astype(jnp.float32) * scale + Fb[..., None] - F[:, :, None, :]
        mask = k_pos[None, :] <= (t0 + jnp.arange(FOX_QBLOCK))[:, None]
        p = jax.nn.softmax(jnp.where(mask, s, -jnp.inf), axis=-1).astype(v.dtype)
        return jnp.einsum('bhqk,bhkd->bhqd', p, v)

    out = lax.map(block, (q_blocks, F_blocks, starts))
    return jnp.moveaxis(out, 0, 2).reshape(B, H, S, Dh)


def hybrid_mixer(h, w_in, conv_w, conv_b, mlstm_i_b, mlstm_f_b, mlstm_norm_w,
                 gla_w_a2, gla_b_a2, gla_norm_w, fox_f_b, w_branch, w_out):
    B, S, _ = h.shape
    dt = h.dtype
    proj = h @ w_in
    split_points = [int(p) for p in np.cumsum(IN_WIDTHS)[:-1]]
    (a_q, a_k, a_v, m_qk, m_v, m_i, m_f, m_o,
     gl_q, gl_k, gl_v, gl_a, gl_r,
     f_q, f_k, f_v, f_f, gates) = jnp.split(proj, split_points, axis=-1)

    pos = jnp.arange(S)
    y_a = from_heads(moba_attention(rope(to_heads(a_q, BRANCH_HEADS), pos),
                                    rope(to_heads(a_k, BRANCH_HEADS), pos),
                                    to_heads(a_v, BRANCH_HEADS))).astype(dt)

    m_q, m_k = jnp.split(jax.nn.silu(causal_conv(m_qk, conv_w, conv_b)), 2, axis=-1)
    i_pre = (m_i + mlstm_i_b).transpose(0, 2, 1)
    f_pre = (m_f + mlstm_f_b).transpose(0, 2, 1)
    h_b = mlstm_chunkwise(to_heads(m_q, BRANCH_HEADS), to_heads(m_k, BRANCH_HEADS),
                          to_heads(m_v, BRANCH_HEADS), i_pre, f_pre)
    y_b = from_heads(head_rmsnorm(h_b.astype(dt), mlstm_norm_w)) * jax.nn.sigmoid(m_o)

    log_a = jax.nn.log_sigmoid((gl_a @ gla_w_a2 + gla_b_a2).astype(jnp.float32)) / GLA_TAU
    o_c = gla_chunkwise(to_heads(gl_q, GLA_HEADS), to_heads(gl_k, GLA_HEADS),
                        to_heads(gl_v, GLA_HEADS), to_heads(log_a, GLA_HEADS))
    y_c = from_heads(head_rmsnorm(o_c.astype(dt), gla_norm_w)) * jax.nn.silu(gl_r)

    logf = jax.nn.log_sigmoid((f_f + fox_f_b).astype(jnp.float32)).transpose(0, 2, 1)
    y_d = from_heads(forgetting_attention(to_heads(f_q, BRANCH_HEADS), to_heads(f_k, BRANCH_HEADS),
                                          to_heads(f_v, BRANCH_HEADS), logf)).astype(dt)

    gate_parts = jnp.split(gates, N_BRANCH, axis=-1)
    branches = (y_a, y_b, y_c, y_d)
    merged = jax.nn.sigmoid(gate_parts[0]) * (branches[0] @ w_branch[0])
    for b in range(1, N_BRANCH):
        merged = merged + jax.nn.sigmoid(gate_parts[b]) * (branches[b] @ w_branch[b])
    return merged @ w_out


def clamped_swiglu(hu):
    g, lin = hu[..., ::2], hu[..., 1::2]
    g = jnp.minimum(g, SWIGLU_LIMIT)
    lin = jnp.clip(lin, -SWIGLU_LIMIT, SWIGLU_LIMIT)
    return g * jax.nn.sigmoid(SWIGLU_ALPHA * g) * (lin + 1.0)


def moe_ffn(h, router_w, router_b, w_up, b_up, w_down, b_down):
    B, S, D = h.shape
    T = B * S
    TK = T * TOP_K
    xt = h.reshape(T, D)
    logits = (xt @ router_w).astype(jnp.float32) + router_b
    top_val, top_idx = lax.top_k(logits, TOP_K)
    gate = jax.nn.softmax(top_val, axis=-1)
    flat_e = top_idx.reshape(-1)
    flat_tok = jnp.arange(TK, dtype=jnp.int32) // TOP_K
    flat_w = gate.reshape(-1)
    order = jnp.argsort(flat_e)
    e_sorted = flat_e[order]
    counts = jnp.bincount(flat_e, length=N_EXPERTS)
    padded = (counts + MOE_ROW_BLOCK - 1) // MOE_ROW_BLOCK * MOE_ROW_BLOCK
    pad_end = jnp.cumsum(padded)
    pad_start = pad_end - padded
    start = jnp.cumsum(counts) - counts
    slot = pad_start[e_sorted] + jnp.arange(TK) - start[e_sorted]
    n_blocks = -(-TK // MOE_ROW_BLOCK) + N_EXPERTS
    P = n_blocks * MOE_ROW_BLOCK
    slot_tok = jnp.full((P,), T, dtype=jnp.int32).at[slot].set(flat_tok[order])
    slot_w = jnp.zeros((P,), jnp.float32).at[slot].set(flat_w[order])
    block_e = jnp.minimum(jnp.searchsorted(pad_end, jnp.arange(n_blocks) * MOE_ROW_BLOCK, side='right'),
                          N_EXPERTS - 1)
    x_ext = jnp.concatenate([xt, jnp.zeros((1, D), xt.dtype)], axis=0)

    def expert_block(args):
        tok, e = args
        xb = x_ext[tok]
        act = clamped_swiglu(xb @ w_up[e] + b_up[e])
        return act @ w_down[e] + b_down[e]

    yb = lax.map(expert_block, (slot_tok.reshape(n_blocks, MOE_ROW_BLOCK), block_e))
    y = jnp.zeros((T + 1, D), jnp.float32).at[slot_tok].add(yb.reshape(P, D).astype(jnp.float32) * slot_w[:, None])
    return y[:T].reshape(B, S, D).astype(h.dtype)


def setup_inputs(seed: int = 0) -> dict:
    key = jax.random.key(seed)
    ks = jax.random.split(key, 24)
    f32 = jnp.float32
    D, L, BW = D_MODEL, DEPTH, BRANCH_WIDTH

    def nrm(k, shape, scale):
        return jax.random.normal(k, shape, f32) * scale

    return {
        'x': nrm(ks[0], (BATCH, SEQ, D), 1.0),
        'c': nrm(ks[1], (BATCH, D), 1.0),
        'ada_w': nrm(ks[2], (L, D, 6 * D), 0.5 * D ** -0.5),
        'ada_b': nrm(ks[3], (L, 6 * D), 0.02),
        'norm_w': 1.0 + nrm(ks[4], (L, 4, D), 0.05),
        'w_in': nrm(ks[5], (L, D, D_IN), D ** -0.5),
        'conv_w': nrm(ks[6], (L, MLSTM_CONV, 2 * BW), MLSTM_CONV ** -0.5),
        'conv_b': nrm(ks[7], (L, 2 * BW), 0.02),
        'mlstm_i_b': nrm(ks[8], (L, BRANCH_HEADS), 0.1),
        'mlstm_f_b': jax.random.uniform(ks[9], (L, BRANCH_HEADS), f32, 3.0, 6.0),
        'mlstm_norm_w': 1.0 + nrm(ks[10], (L, BRANCH_HEADS, HEAD_DIM), 0.05),
        'gla_w_a2': nrm(ks[11], (L, GLA_RANK, GLA_HEADS * GLA_DK), GLA_RANK ** -0.5),
        'gla_b_a2': nrm(ks[12], (L, GLA_HEADS * GLA_DK), 0.1),
        'gla_norm_w': 1.0 + nrm(ks[13], (L, GLA_HEADS, GLA_DV), 0.05),
        'fox_f_b': jax.random.uniform(ks[14], (L, BRANCH_HEADS), f32, 1.0, 4.0),
        'w_branch': nrm(ks[15], (L, N_BRANCH, BW, D), BW ** -0.5),
        'w_out': nrm(ks[16], (L, D, D), D ** -0.5),
        'router_w': nrm(ks[17], (L, D, N_EXPERTS), D ** -0.5),
        'router_b': nrm(ks[18], (L, N_EXPERTS), 0.01),
        'w_up': nrm(ks[19], (L, N_EXPERTS, D, 2 * D_EXPERT), D ** -0.5),
        'b_up': nrm(ks[20], (L, N_EXPERTS, 2 * D_EXPERT), 0.01),
        'w_down': nrm(ks[21], (L, N_EXPERTS, D_EXPERT, D), D_EXPERT ** -0.5),
        'b_down': nrm(ks[22], (L, N_EXPERTS, D), 0.01),
    }


def reference(x, c, ada_w, ada_b, norm_w, w_in, conv_w, conv_b, mlstm_i_b, mlstm_f_b, mlstm_norm_w,
              gla_w_a2, gla_b_a2, gla_norm_w, fox_f_b, w_branch, w_out, router_w, router_b,
              w_up, b_up, w_down, b_down):
    cond = jax.nn.silu(c)
    for l in range(DEPTH):
        mod = (cond @ ada_w[l] + ada_b[l])[:, None, :]
        sh_m, sc_m, g_m, sh_f, sc_f, g_f = jnp.split(mod, 6, axis=-1)
        h = rmsnorm(x, norm_w[l, 0]) * (1.0 + sc_m) + sh_m
        y = hybrid_mixer(h, w_in[l], conv_w[l], conv_b[l], mlstm_i_b[l], mlstm_f_b[l], mlstm_norm_w[l],
                         gla_w_a2[l], gla_b_a2[l], gla_norm_w[l], fox_f_b[l], w_branch[l], w_out[l])
        x = x + g_m * rmsnorm(y, norm_w[l, 1])
        h = rmsnorm(x, norm_w[l, 2]) * (1.0 + sc_f) + sh_f
        y = moe_ffn(h, router_w[l], router_b[l], w_up[l], b_up[l], w_down[l], b_down[l])
        x = x + g_f * rmsnorm(y, norm_w[l, 3])
    return x
```

```python
import functools

import jax
import jax.numpy as jnp
from jax import lax
from jax.experimental import pallas as pl
from jax.experimental.pallas import tpu as pltpu

F32 = jnp.float32
BF16 = jnp.bfloat16
NEG_INF = float("-inf")

D_MODEL = 2048
HEAD_DIM = 128
N_BRANCH = 4
BRANCH_HEADS = 4
BRANCH_WIDTH = BRANCH_HEADS * HEAD_DIM

MOBA_BLOCK = 256
MOBA_TOPK = 3
ROPE_THETA = 10000.0

MLSTM_CHUNK = 128
MLSTM_CONV = 4

GLA_HEADS = 4
GLA_DK = 64
GLA_DV = 128
GLA_RANK = 16
GLA_TAU = 16.0
GLA_CHUNK = 64
GLA_SUB = 16

N_EXPERTS = 32
TOP_K = 4
D_EXPERT = D_MODEL // 2
SWIGLU_LIMIT = 7.0
SWIGLU_ALPHA = 1.702
MOE_ROW_BLOCK = 512

RMS_EPS = 1e-6

LANES = 128
CHUNK = 128

C_AQ, C_AK, C_AV = 0, 512, 1024
C_MQ, C_MK, C_MV, C_MO = 1536, 2048, 2560, 3072
C_GQ, C_GK, C_GV, C_GR = 3584, 3840, 4096, 4608
C_FQ, C_FK, C_FV = 5120, 5632, 6144
C_GATE = 6656
C_SMALL = 14848
SM_I, SM_F, SM_GA, SM_FF = 0, 4, 8, 24
N_PROJ = 15360

NN = (((1,), (0,)), ((), ()))
NT = (((1,), (1,)), ((), ()))
TN = (((0,), (0,)), ((), ()))

VMEM_LIMIT = 56 * 1024 * 1024


def _cp(sem, vmem=VMEM_LIMIT):
    return pltpu.CompilerParams(dimension_semantics=sem, vmem_limit_bytes=vmem)


def _dot(a, b, dn=NN):
    return lax.dot_general(a, b, dn, preferred_element_type=F32)


def _split3(x):
    hi = x.astype(BF16)
    r1 = x - hi.astype(F32)
    mid = r1.astype(BF16)
    lo = (r1 - mid.astype(F32)).astype(BF16)
    return hi, mid, lo


def _dot01(sel, x, dn=NN):
    hi, mid, lo = _split3(x)
    return _dot(sel, hi, dn) + _dot(sel, mid, dn) + _dot(sel, lo, dn)


def _dot_x3(a, b, dn=NN):
    ah = a.astype(BF16)
    al = (a - ah.astype(F32)).astype(BF16)
    bh = b.astype(BF16)
    bl = (b - bh.astype(F32)).astype(BF16)
    return _dot(ah, bh, dn) + _dot(al, bh, dn) + _dot(ah, bl, dn)


def _log_sigmoid(x):
    return jnp.minimum(x, 0.0) - jnp.log1p(jnp.exp(-jnp.abs(x)))


def _sigmoid(x):
    return 1.0 / (1.0 + jnp.exp(-x))


def _tril_bf16(n):
    r = lax.broadcasted_iota(jnp.int32, (n, n), 0)
    c = lax.broadcasted_iota(jnp.int32, (n, n), 1)
    return jnp.where(c <= r, 1.0, 0.0).astype(BF16)


def _row_selector(lane_idx):
    r = lax.broadcasted_iota(jnp.int32, (8, LANES), 0)
    c = lax.broadcasted_iota(jnp.int32, (8, LANES), 1)
    return jnp.where((r == 0) & (c == lane_idx), 1.0, 0.0).astype(BF16)


def _pick_lane(x, lane_idx):
    lane = lax.broadcasted_iota(jnp.int32, x.shape, 1)
    return jnp.sum(jnp.where(lane == lane_idx, x, 0.0), axis=-1, keepdims=True)


def _ada_kernel(c_ref, w_ref, b_ref, o_ref):
    c = c_ref[...]
    cond = c * _sigmoid(c)
    o_ref[0] = _dot(cond.astype(BF16), w_ref[0].astype(BF16)) + b_ref[0]


def _ada_mod(c, ada_w, ada_b):
    depth, d, n = ada_w.shape
    b = c.shape[0]
    tn = 1024
    return pl.pallas_call(
        _ada_kernel,
        grid=(depth, n // tn),
        in_specs=[
            pl.BlockSpec((b, d), lambda l, j: (0, 0)),
            pl.BlockSpec((1, d, tn), lambda l, j: (l, 0, j)),
            pl.BlockSpec((1, 1, tn), lambda l, j: (l, 0, j)),
        ],
        out_specs=pl.BlockSpec((1, b, tn), lambda l, j: (l, 0, j)),
        out_shape=jax.ShapeDtypeStruct((depth, b, n), F32),
        compiler_params=_cp(("parallel", "parallel")),
        name="ada_mod",
    )(c, ada_w, ada_b.reshape(depth, 1, n))


def _norm_mod_rows(x, nw, sc, sh):
    var = jnp.mean(x * x, axis=-1, keepdims=True)
    y = x * lax.rsqrt(var + RMS_EPS) * nw
    return y * (1.0 + sc) + sh


def _rms_rows(y, nw):
    var = jnp.mean(y * y, axis=-1, keepdims=True)
    return y * lax.rsqrt(var + RMS_EPS) * nw


def _inproj_kernel(x_ref, nw_ref, sc_ref, sh_ref, w_ref, o_ref, h_scr, *, rows):
    @pl.when(pl.program_id(1) == 0)
    def _():
        def body(r, carry):
            sl = pl.ds(pl.multiple_of(r * rows, rows), rows)
            h = _norm_mod_rows(x_ref[sl, :], nw_ref[0], sc_ref[0], sh_ref[0])
            h_scr[sl, :] = h.astype(BF16)
            return carry
        lax.fori_loop(0, x_ref.shape[0] // rows, body, 0)

    o_ref[...] = _dot(h_scr[...], w_ref[...])


def _in_proj(x2, mod3, nw3, w_re, seq, *, nw_idx, sc_idx, sh_idx):
    t, d = x2.shape
    n = w_re.shape[1]
    tm = min(1024, seq)
    tn = 1024
    per_b = seq // tm
    return pl.pallas_call(
        functools.partial(_inproj_kernel, rows=min(256, tm)),
        grid=(t // tm, n // tn),
        in_specs=[
            pl.BlockSpec((tm, d), lambda i, j: (i, 0)),
            pl.BlockSpec((1, 1, d), lambda i, j: (nw_idx, 0, 0)),
            pl.BlockSpec((1, 1, d), lambda i, j: (i // per_b, 0, sc_idx)),
            pl.BlockSpec((1, 1, d), lambda i, j: (i // per_b, 0, sh_idx)),
            pl.BlockSpec((d, tn), lambda i, j: (0, j)),
        ],
        out_specs=pl.BlockSpec((tm, tn), lambda i, j: (i, j)),
        out_shape=jax.ShapeDtypeStruct((t, n), F32),
        scratch_shapes=[pltpu.VMEM((tm, d), BF16)],
        compiler_params=_cp(("parallel", "arbitrary")),
        name="in_proj",
    )(x2, nw3, mod3, mod3, w_re)


def _rope(x, cos, sin):
    return x * cos + pltpu.roll(x, HEAD_DIM // 2, 1) * sin


def _moba_kernel(q_ref, k_ref, v_ref, cos_ref, sin_ref, o_ref, kr_scr, km_scr, *, nblk, n_sel):
    blk = MOBA_BLOCK
    i = pl.program_id(2)

    @pl.when(i == 0)
    def _():
        km_scr[...] = jnp.zeros_like(km_scr)
        for j in range(nblk):
            rows = pl.ds(j * blk, blk)
            kk = _rope(k_ref[rows, :], cos_ref[rows, :], sin_ref[rows, :])
            kr_scr[rows, :] = kk.astype(BF16)
            km_scr[j:j + 1, :] = jnp.mean(kk, axis=0, keepdims=True)

    rows_q = pl.ds(pl.multiple_of(i * blk, blk), blk)
    q = _rope(q_ref[...], cos_ref[rows_q, :], sin_ref[rows_q, :])
    lane = lax.broadcasted_iota(jnp.int32, (blk, LANES), 1)
    gate = jnp.where(lane < i, _dot_x3(q, km_scr[...], NT), NEG_INF)
    qb = q.astype(BF16)
    scale = HEAD_DIM ** -0.5

    s = _dot(qb, kr_scr[rows_q, :], NT) * scale
    r_io = lax.broadcasted_iota(jnp.int32, (blk, blk), 0)
    c_io = lax.broadcasted_iota(jnp.int32, (blk, blk), 1)
    s = jnp.where(c_io <= r_io, s, NEG_INF)
    m0 = jnp.max(s, axis=-1, keepdims=True)
    p = jnp.exp(s - m0)
    l0 = jnp.sum(p, axis=-1, keepdims=True)
    acc0 = _dot(p.astype(BF16), v_ref[rows_q, :].astype(BF16))

    def body(j, carry):
        m, l, acc = carry
        rows = pl.ds(pl.multiple_of(j * blk, blk), blk)
        gj = jnp.sum(jnp.where(lane == j, gate, 0.0), axis=-1, keepdims=True)
        beats = (gate > gj) | ((gate == gj) & (lane < j))
        rank = jnp.sum(jnp.where(beats, 1.0, 0.0), axis=-1, keepdims=True)
        s = _dot(qb, kr_scr[rows, :], NT) * scale
        s = jnp.where(rank < n_sel, s, NEG_INF)
        m_new = jnp.maximum(m, jnp.max(s, axis=-1, keepdims=True))
        alpha = jnp.exp(m - m_new)
        p = jnp.exp(s - m_new)
        l = alpha * l + jnp.sum(p, axis=-1, keepdims=True)
        acc = alpha * acc + _dot(p.astype(BF16), v_ref[rows, :].astype(BF16))
        return m_new, l, acc

    _, l, acc = lax.fori_loop(0, i, body, (m0, l0, acc0))
    o_ref[...] = (acc / l).astype(o_ref.dtype)


def _moba(proj, cos, sin, batch, seq):
    nblk = seq // MOBA_BLOCK
    n_sel = max(1, min(MOBA_TOPK, nblk - 1))
    hb = HEAD_DIM
    return pl.pallas_call(
        functools.partial(_moba_kernel, nblk=nblk, n_sel=n_sel),
        grid=(batch, BRANCH_HEADS, nblk),
        in_specs=[
            pl.BlockSpec((MOBA_BLOCK, hb), lambda b, h, i: (b * nblk + i, C_AQ // hb + h)),
            pl.BlockSpec((seq, hb), lambda b, h, i: (b, C_AK // hb + h)),
            pl.BlockSpec((seq, hb), lambda b, h, i: (b, C_AV // hb + h)),
            pl.BlockSpec((seq, hb), lambda b, h, i: (0, 0)),
            pl.BlockSpec((seq, hb), lambda b, h, i: (0, 0)),
        ],
        out_specs=pl.BlockSpec((MOBA_BLOCK, hb), lambda b, h, i: (b * nblk + i, h)),
        out_shape=jax.ShapeDtypeStruct((batch * seq, BRANCH_WIDTH), BF16),
        scratch_shapes=[pltpu.VMEM((seq, hb), BF16), pltpu.VMEM((LANES, hb), F32)],
        compiler_params=_cp(("parallel", "parallel", "arbitrary")),
        name="moba",
    )(proj, proj, proj, cos, sin)


def _fox_kernel(q_ref, k_ref, v_ref, sm_ref, bias_ref, o_ref, fcol_scr, frow_scr, *, seq, tq):
    h = pl.program_id(1)
    i = pl.program_id(2)
    f_lane = SM_FF + h

    @pl.when(i == 0)
    def _():
        tril = _tril_bf16(CHUNK)
        carry = jnp.zeros((1, LANES), F32)
        for c in range(seq // CHUNK):
            rows = pl.ds(c * CHUNK, CHUNK)
            lf = _log_sigmoid(sm_ref[rows, :] + bias_ref[...])
            cs = _dot01(tril, lf) + carry
            fcol_scr[rows, :] = cs
            carry = cs[CHUNK - 1:CHUNK, :]
        frow_scr[...] = _dot01(_row_selector(f_lane), fcol_scr[...], NT)

    rows_q = pl.ds(pl.multiple_of(i * tq, tq), tq)
    f_t = _pick_lane(fcol_scr[rows_q, :], f_lane)
    qb = q_ref[...].astype(BF16)
    scale = HEAD_DIM ** -0.5

    def scores(rows):
        s = _dot(qb, k_ref[rows, :].astype(BF16), NT) * scale
        return s + f_t - frow_scr[0:1, rows]

    s = scores(rows_q)
    r_io = lax.broadcasted_iota(jnp.int32, (tq, tq), 0)
    c_io = lax.broadcasted_iota(jnp.int32, (tq, tq), 1)
    s = jnp.where(c_io <= r_io, s, NEG_INF)
    m0 = jnp.max(s, axis=-1, keepdims=True)
    p = jnp.exp(s - m0)
    l0 = jnp.sum(p, axis=-1, keepdims=True)
    acc0 = _dot(p.astype(BF16), v_ref[rows_q, :].astype(BF16))

    def body(j, carry):
        m, l, acc = carry
        rows = pl.ds(pl.multiple_of(j * tq, tq), tq)
        s = scores(rows)
        m_new = jnp.maximum(m, jnp.max(s, axis=-1, keepdims=True))
        alpha = jnp.exp(m - m_new)
        p = jnp.exp(s - m_new)
        l = alpha * l + jnp.sum(p, axis=-1, keepdims=True)
        acc = alpha * acc + _dot(p.astype(BF16), v_ref[rows, :].astype(BF16))
        return m_new, l, acc

    _, l, acc = lax.fori_loop(0, i, body, (m0, l0, acc0))
    o_ref[...] = (acc / l).astype(o_ref.dtype)


def _fox(proj, small_bias, batch, seq):
    tq = 256
    nq = seq // tq
    hb = HEAD_DIM
    return pl.pallas_call(
        functools.partial(_fox_kernel, seq=seq, tq=tq),
        grid=(batch, BRANCH_HEADS, nq),
        in_specs=[
            pl.BlockSpec((tq, hb), lambda b, h, i: (b * nq + i, C_FQ // hb + h)),
            pl.BlockSpec((seq, hb), lambda b, h, i: (b, C_FK // hb + h)),
            pl.BlockSpec((seq, hb), lambda b, h, i: (b, C_FV // hb + h)),
            pl.BlockSpec((seq, LANES), lambda b, h, i: (b, C_SMALL // LANES)),
            pl.BlockSpec((1, LANES), lambda b, h, i: (0, 0)),
        ],
        out_specs=pl.BlockSpec((tq, hb), lambda b, h, i: (b * nq + i, h)),
        out_shape=jax.ShapeDtypeStruct((batch * seq, BRANCH_WIDTH), BF16),
        scratch_shapes=[pltpu.VMEM((seq, LANES), F32), pltpu.VMEM((8, seq), F32)],
        compiler_params=_cp(("parallel", "parallel", "arbitrary")),
        name="fox",
    )(proj, proj, proj, proj, small_bias)


def _causal_conv_silu(x, w_ref, b_ref):
    k = w_ref.shape[0]
    row = lax.broadcasted_iota(jnp.int32, x.shape, 0)
    y = x * w_ref[k - 1:k, :] + b_ref[...]
    for sft in range(1, k):
        xs = jnp.where(row >= sft, pltpu.roll(x, sft, 0), 0.0)
        y = y + xs * w_ref[k - 1 - sft:k - sft, :]
    return y * _sigmoid(y)


def _mlstm_kernel(q_ref, k_ref, v_ref, o_ref, sm_ref, bias_ref, cwq_ref, cwk_ref, cbq_ref, cbk_ref,
                  nw_ref, y_ref, q_scr, k_scr, x_scr, g_scr, irow_scr, grow_scr, *, seq):
    h = pl.program_id(1)
    L = MLSTM_CHUNK
    nc = seq // L
    i_lane = SM_I + h
    f_lane = SM_F + h

    q_scr[...] = _causal_conv_silu(q_ref[...], cwq_ref, cbq_ref)
    k_scr[...] = _causal_conv_silu(k_ref[...], cwk_ref, cbk_ref) * (HEAD_DIM ** -0.5)

    smb = sm_ref[...] + bias_ref[...]
    lane_s = lax.broadcasted_iota(jnp.int32, smb.shape, 1)
    x_scr[...] = jnp.where(lane_s < SM_F, smb, _log_sigmoid(smb))
    tril = _tril_bf16(L)
    for c in range(nc):
        rows = pl.ds(c * L, L)
        g_scr[rows, :] = _dot01(tril, x_scr[rows, :])
    irow_scr[...] = _dot01(_row_selector(i_lane), x_scr[...], NT)
    grow_scr[...] = _dot01(_row_selector(f_lane), g_scr[...], NT)

    r_io = lax.broadcasted_iota(jnp.int32, (L, L), 0)
    c_io = lax.broadcasted_iota(jnp.int32, (L, L), 1)
    causal = c_io <= r_io

    def step(c, carry):
        C, n, m = carry
        rows = pl.ds(pl.multiple_of(c * L, L), L)
        qc = q_scr[rows, :]
        kc = k_scr[rows, :]
        vb = v_ref[rows, :].astype(BF16)
        qb = qc.astype(BF16)
        g_col = _pick_lane(g_scr[rows, :], f_lane)
        i_col = _pick_lane(x_scr[rows, :], i_lane)
        g_row = grow_scr[0:1, rows]
        i_row = irow_scr[0:1, rows]
        dmat = jnp.where(causal, g_col - g_row + i_row, NEG_INF)
        a = g_col + m
        m_out = jnp.maximum(a, jnp.max(dmat, axis=-1, keepdims=True))
        w = jnp.exp(dmat - m_out) * _dot(qb, kc.astype(BF16), NT)
        inter = jnp.exp(a - m_out)
        num = inter * _dot(qb, C.astype(BF16)) + _dot(w.astype(BF16), vb)
        den = inter * jnp.sum(qc * n, axis=-1, keepdims=True) + jnp.sum(w, axis=-1, keepdims=True)
        hh = num / jnp.maximum(jnp.abs(den), jnp.exp(-m_out))
        g_last = g_row[:, L - 1:L]
        lw = g_last - g_col + i_col
        m_new = jnp.maximum(g_last + m, jnp.max(lw, axis=0, keepdims=True))
        wk = jnp.exp(lw - m_new)
        decay = jnp.exp(g_last + m - m_new)
        kw = kc * wk
        C = decay * C + _dot(kw.astype(BF16), vb, TN)
        n = decay * n + jnp.sum(kw, axis=0, keepdims=True)
        yn = _rms_rows(hh, nw_ref[0])
        y_ref[rows, :] = (yn * _sigmoid(o_ref[rows, :])).astype(y_ref.dtype)
        return C, n, m_new

    init = (jnp.zeros((HEAD_DIM, HEAD_DIM), F32), jnp.zeros((1, HEAD_DIM), F32), jnp.zeros((1, 1), F32))
    lax.fori_loop(0, nc, step, init)


def _mlstm(proj, small_bias, conv_w, conv_b, norm_w, batch, seq):
    hb = HEAD_DIM
    nh = BRANCH_HEADS
    blk = lambda c0: pl.BlockSpec((seq, hb), lambda b, h: (b, c0 // hb + h))
    return pl.pallas_call(
        functools.partial(_mlstm_kernel, seq=seq),
        grid=(batch, nh),
        in_specs=[
            blk(C_MQ), blk(C_MK), blk(C_MV), blk(C_MO),
            pl.BlockSpec((seq, LANES), lambda b, h: (b, C_SMALL // LANES)),
            pl.BlockSpec((1, LANES), lambda b, h: (0, 0)),
            pl.BlockSpec((MLSTM_CONV, hb), lambda b, h: (0, h)),
            pl.BlockSpec((MLSTM_CONV, hb), lambda b, h: (0, nh + h)),
            pl.BlockSpec((1, hb), lambda b, h: (0, h)),
            pl.BlockSpec((1, hb), lambda b, h: (0, nh + h)),
            pl.BlockSpec((1, 1, hb), lambda b, h: (h, 0, 0)),
        ],
        out_specs=pl.BlockSpec((seq, hb), lambda b, h: (b, h)),
        out_shape=jax.ShapeDtypeStruct((batch * seq, BRANCH_WIDTH), BF16),
        scratch_shapes=[
            pltpu.VMEM((seq, hb), F32), pltpu.VMEM((seq, hb), F32),
            pltpu.VMEM((seq, LANES), F32), pltpu.VMEM((seq, LANES), F32),
            pltpu.VMEM((8, seq), F32), pltpu.VMEM((8, seq), F32),
        ],
        compiler_params=_cp(("parallel", "parallel")),
        name="mlstm",
    )(proj, proj, proj, proj, proj, small_bias, conv_w, conv_w, conv_b.reshape(1, -1), conv_b.reshape(1, -1),
      norm_w.reshape(nh, 1, hb))


def _gla_kernel(q_ref, k_ref, v_ref, r_ref, sm_ref, wa_ref, ba_ref, nw_ref, y_ref, la_scr, *, seq):
    L = GLA_CHUNK
    sub = GLA_SUB
    nc = seq // L
    dk, dv = GLA_DK, GLA_DV

    z = _dot_x3(sm_ref[...], wa_ref[...]) + ba_ref[...]
    la_scr[...] = _log_sigmoid(z) * (1.0 / GLA_TAU)
    tril = _tril_bf16(L)
    r_io = lax.broadcasted_iota(jnp.int32, (sub, L), 0)
    c_io = lax.broadcasted_iota(jnp.int32, (sub, L), 1)

    def step(c, states):
        rows = pl.ds(pl.multiple_of(c * L, L), L)
        bc2 = _dot01(tril, la_scr[rows, :])
        q2 = q_ref[rows, :] * (dk ** -0.5)
        k2 = k_ref[rows, :]
        new_states = []
        for hh in range(2):
            st = states[hh]
            lanes = slice(hh * dk, (hh + 1) * dk)
            b = bc2[:, lanes]
            qh = q2[:, lanes]
            kh = k2[:, lanes]
            vb = v_ref[rows, hh * dv:(hh + 1) * dv].astype(BF16)
            o = _dot((qh * jnp.exp(b)).astype(BF16), st.astype(BF16), NT)
            blocks = []
            for ib in range(L // sub):
                ref_row = b[ib * sub:ib * sub + 1, :]
                qi = qh[ib * sub:(ib + 1) * sub, :] * jnp.exp(b[ib * sub:(ib + 1) * sub, :] - ref_row)
                upto = (ib + 1) * sub
                row_l = lax.broadcasted_iota(jnp.int32, (L, dk), 0)
                ki = kh * jnp.exp(jnp.where(row_l < upto, ref_row - b, 0.0))
                a = _dot(qi.astype(BF16), ki.astype(BF16), NT)
                blocks.append(jnp.where(c_io <= r_io + ib * sub, a, 0.0))
            attn = jnp.concatenate(blocks, axis=0)
            o = o + _dot(attn.astype(BF16), vb)
            b_last = b[L - 1:L, :]
            ke = kh * jnp.exp(b_last - b)
            st = jnp.exp(b_last) * st + _dot(vb, ke.astype(BF16), TN)
            new_states.append(st)
            yn = _rms_rows(o, nw_ref[0, hh:hh + 1, :])
            r = r_ref[rows, hh * dv:(hh + 1) * dv]
            y_ref[rows, hh * dv:(hh + 1) * dv] = (yn * (r * _sigmoid(r))).astype(y_ref.dtype)
        return tuple(new_states)

    init = (jnp.zeros((dv, dk), F32), jnp.zeros((dv, dk), F32))
    lax.fori_loop(0, nc, step, init)


def _gla(proj, w_a2_pad, b_a2, norm_w, batch, seq):
    return pl.pallas_call(
        functools.partial(_gla_kernel, seq=seq),
        grid=(batch, GLA_HEADS // 2),
        in_specs=[
            pl.BlockSpec((seq, LANES), lambda b, p: (b, C_GQ // LANES + p)),
            pl.BlockSpec((seq, LANES), lambda b, p: (b, C_GK // LANES + p)),
            pl.BlockSpec((seq, 2 * GLA_DV), lambda b, p: (b, C_GV // (2 * GLA_DV) + p)),
            pl.BlockSpec((seq, 2 * GLA_DV), lambda b, p: (b, C_GR // (2 * GLA_DV) + p)),
            pl.BlockSpec((seq, LANES), lambda b, p: (b, C_SMALL // LANES)),
            pl.BlockSpec((LANES, LANES), lambda b, p: (0, p)),
            pl.BlockSpec((1, LANES), lambda b, p: (0, p)),
            pl.BlockSpec((1, 2, GLA_DV), lambda b, p: (p, 0, 0)),
        ],
        out_specs=pl.BlockSpec((seq, 2 * GLA_DV), lambda b, p: (b, p)),
        out_shape=jax.ShapeDtypeStruct((batch * seq, GLA_HEADS * GLA_DV), BF16),
        scratch_shapes=[pltpu.VMEM((seq, LANES), F32)],
        compiler_params=_cp(("parallel", "parallel")),
        name="gla",
    )(proj, proj, proj, proj, proj, w_a2_pad, b_a2.reshape(1, -1), norm_w.reshape(GLA_HEADS // 2, 2, GLA_DV))


def _merge_kernel(ya_ref, yb_ref, yc_ref, yd_ref, g0_ref, g1_ref, g2_ref, g3_ref, w_ref, o_ref):
    ys = (ya_ref, yb_ref, yc_ref, yd_ref)
    gs = (g0_ref, g1_ref, g2_ref, g3_ref)
    acc = None
    for b in range(N_BRANCH):
        term = _sigmoid(gs[b][...]) * _dot(ys[b][...], w_ref[b])
        acc = term if acc is None else acc + term
    o_ref[...] = acc.astype(o_ref.dtype)


def _merge(ys, proj, w_branch_bf16):
    t = proj.shape[0]
    d = D_MODEL
    tm, tn = 1024, 512
    tm = min(tm, t)
    yspec = pl.BlockSpec((tm, BRANCH_WIDTH), lambda i, j: (i, 0))
    gspec = lambda b: pl.BlockSpec((tm, tn), lambda i, j: (i, (C_GATE + b * d) // tn + j))
    return pl.pallas_call(
        _merge_kernel,
        grid=(t // tm, d // tn),
        in_specs=[yspec] * 4 + [gspec(b) for b in range(N_BRANCH)]
        + [pl.BlockSpec((N_BRANCH, BRANCH_WIDTH, tn), lambda i, j: (0, 0, j))],
        out_specs=pl.BlockSpec((tm, tn), lambda i, j: (i, j)),
        out_shape=jax.ShapeDtypeStruct((t, d), BF16),
        compiler_params=_cp(("parallel", "parallel")),
        name="merge",
    )(*ys, proj, proj, proj, proj, w_branch_bf16)


def _outproj_kernel(m_ref, w_ref, x_ref, nw_ref, g_ref, o_ref):
    y = _dot(m_ref[...], w_ref[...])
    o_ref[...] = x_ref[...] + g_ref[0] * _rms_rows(y, nw_ref[0])


def _out_proj(merged, w_out_bf16, x2, nw3, mod3, seq, *, nw_idx, g_idx):
    t, d = x2.shape
    tm = min(512, seq)
    per_b = seq // tm
    return pl.pallas_call(
        _outproj_kernel,
        grid=(t // tm,),
        in_specs=[
            pl.BlockSpec((tm, d), lambda i: (i, 0)),
            pl.BlockSpec((d, d), lambda i: (0, 0)),
            pl.BlockSpec((tm, d), lambda i: (i, 0)),
            pl.BlockSpec((1, 1, d), lambda i: (nw_idx, 0, 0)),
            pl.BlockSpec((1, 1, d), lambda i: (i // per_b, 0, g_idx)),
        ],
        out_specs=pl.BlockSpec((tm, d), lambda i: (i, 0)),
        out_shape=jax.ShapeDtypeStruct((t, d), F32),
        compiler_params=_cp(("parallel",)),
        name="out_proj",
    )(merged, w_out_bf16, x2, nw3, mod3)


def _router_kernel(x_ref, nw_ref, sc_ref, sh_ref, rw_ref, rb_ref, h_ref, r_ref):
    h = _norm_mod_rows(x_ref[...], nw_ref[0], sc_ref[0], sh_ref[0])
    h_ref[...] = h
    logits = _dot_x3(h, rw_ref[...]) + rb_ref[...]
    lane = lax.broadcasted_iota(jnp.int32, logits.shape, 1)
    lane_f = lane.astype(F32)
    l = jnp.where(lane < N_EXPERTS, logits, NEG_INF)
    vals, idxs = [], []
    for _ in range(TOP_K):
        m = jnp.max(l, axis=-1, keepdims=True)
        idx = jnp.min(jnp.where(l == m, lane_f, float(LANES)), axis=-1, keepdims=True)
        vals.append(m)
        idxs.append(idx)
        l = jnp.where(lane_f == idx, NEG_INF, l)
    es = [jnp.exp(v - vals[0]) for v in vals]
    tot = es[0]
    for e in es[1:]:
        tot = tot + e
    out = jnp.zeros(logits.shape, F32)
    for r in range(TOP_K):
        out = jnp.where(lane == r, es[r] / tot, out)
        out = jnp.where(lane == TOP_K + r, idxs[r], out)
    r_ref[...] = out


def _router(x2, nw3, mod3, router_w_pad, router_b_pad, seq, *, nw_idx, sc_idx, sh_idx):
    t, d = x2.shape
    tm = min(256, seq)
    per_b = seq // tm
    return pl.pallas_call(
        _router_kernel,
        grid=(t // tm,),
        in_specs=[
            pl.BlockSpec((tm, d), lambda i: (i, 0)),
            pl.BlockSpec((1, 1, d), lambda i: (nw_idx, 0, 0)),
            pl.BlockSpec((1, 1, d), lambda i: (i // per_b, 0, sc_idx)),
            pl.BlockSpec((1, 1, d), lambda i: (i // per_b, 0, sh_idx)),
            pl.BlockSpec((d, LANES), lambda i: (0, 0)),
            pl.BlockSpec((1, LANES), lambda i: (0, 0)),
        ],
        out_specs=[pl.BlockSpec((tm, d), lambda i: (i, 0)), pl.BlockSpec((tm, LANES), lambda i: (i, 0))],
        out_shape=[jax.ShapeDtypeStruct((t, d), F32), jax.ShapeDtypeStruct((t, LANES), F32)],
        compiler_params=_cp(("parallel",)),
        name="router",
    )(x2, nw3, mod3, mod3, router_w_pad, router_b_pad)


def _gather_kernel(nused_ref, idx_hbm, table_hbm, o_ref, idx_smem, isem, sem, *, rows):
    b = pl.program_id(0)

    def row_copy(tok, r):
        return pltpu.make_async_copy(table_hbm.at[pl.ds(tok, 1), :], o_ref.at[pl.ds(r, 1), :], sem)

    @pl.when(b < nused_ref[0])
    def _():
        icp = pltpu.make_async_copy(idx_hbm.at[b], idx_smem, isem)
        icp.start()
        icp.wait()

        def issue(r, carry):
            row_copy(idx_smem[r], r).start()
            return carry
        lax.fori_loop(0, rows, issue, 0, unroll=8)

        def drain(r, carry):
            row_copy(0, r).wait()
            return carry
        lax.fori_loop(0, rows, drain, 0, unroll=8)

    @pl.when(b >= nused_ref[0])
    def _():
        o_ref[...] = jnp.zeros_like(o_ref)


def _gather_rows(table, idx, n_used):
    rows = MOE_ROW_BLOCK
    n, d = idx.shape[0], table.shape[1]
    nb = n // rows
    grid_spec = pltpu.PrefetchScalarGridSpec(
        num_scalar_prefetch=1,
        grid=(nb,),
        in_specs=[pl.BlockSpec(memory_space=pl.ANY), pl.BlockSpec(memory_space=pl.ANY)],
        out_specs=pl.BlockSpec((rows, d), lambda b, nu: (b, 0)),
        scratch_shapes=[pltpu.SMEM((rows,), jnp.int32), pltpu.SemaphoreType.DMA, pltpu.SemaphoreType.DMA],
    )
    return pl.pallas_call(
        functools.partial(_gather_kernel, rows=rows),
        grid_spec=grid_spec,
        out_shape=jax.ShapeDtypeStruct((n, d), table.dtype),
        compiler_params=_cp(("arbitrary",)),
        name="gather_rows",
    )(n_used, idx.reshape(nb, rows), table)


def _expert_kernel(be_ref, nused_ref, x_ref, wg_ref, wl_ref, bg_ref, bl_ref, wd_ref, bd_ref, o_ref):
    b = pl.program_id(0)

    @pl.when(b < nused_ref[0])
    def _():
        x = x_ref[...].astype(BF16)
        g = _dot(x, wg_ref[0]) + bg_ref[0]
        lin = _dot(x, wl_ref[0]) + bl_ref[0]
        g = jnp.minimum(g, SWIGLU_LIMIT)
        lin = jnp.clip(lin, -SWIGLU_LIMIT, SWIGLU_LIMIT)
        act = g * _sigmoid(SWIGLU_ALPHA * g) * (lin + 1.0)
        o_ref[...] = _dot(act.astype(BF16), wd_ref[0]) + bd_ref[0]

    @pl.when(b >= nused_ref[0])
    def _():
        o_ref[...] = jnp.zeros_like(o_ref)


def _experts(x_sorted, block_e, n_used, wg, wl, bg, bl, wd, bd):
    p, d = x_sorted.shape
    rows = MOE_ROW_BLOCK
    nb = p // rows
    de = D_EXPERT

    def xmap(b, be, nu):
        return (jnp.minimum(b, nu[0] - 1), 0)

    grid_spec = pltpu.PrefetchScalarGridSpec(
        num_scalar_prefetch=2,
        grid=(nb,),
        in_specs=[
            pl.BlockSpec((rows, d), xmap),
            pl.BlockSpec((1, d, de), lambda b, be, nu: (be[b], 0, 0)),
            pl.BlockSpec((1, d, de), lambda b, be, nu: (be[b], 0, 0)),
            pl.BlockSpec((1, 1, de), lambda b, be, nu: (be[b], 0, 0)),
            pl.BlockSpec((1, 1, de), lambda b, be, nu: (be[b], 0, 0)),
            pl.BlockSpec((1, de, d), lambda b, be, nu: (be[b], 0, 0)),
            pl.BlockSpec((1, 1, d), lambda b, be, nu: (be[b], 0, 0)),
        ],
        out_specs=pl.BlockSpec((rows, d), lambda b, be, nu: (b, 0)),
    )
    return pl.pallas_call(
        _expert_kernel,
        grid_spec=grid_spec,
        out_shape=jax.ShapeDtypeStruct((p, d), F32),
        compiler_params=_cp(("arbitrary",)),
        name="experts",
    )(block_e, n_used, x_sorted, wg, wl, bg, bl, wd, bd)


def _combine_kernel(y4_ref, r_ref, x_ref, nw_ref, g_ref, o_ref):
    d = x_ref.shape[1]
    gates = r_ref[...]
    y = None
    for k in range(TOP_K):
        term = gates[:, k:k + 1] * y4_ref[:, k * d:(k + 1) * d]
        y = term if y is None else y + term
    o_ref[...] = x_ref[...] + g_ref[0] * _rms_rows(y, nw_ref[0])


def _combine(y4, route, x2, nw3, mod3, seq, *, nw_idx, g_idx):
    t, d = x2.shape
    tm = min(256, seq)
    per_b = seq // tm
    return pl.pallas_call(
        _combine_kernel,
        grid=(t // tm,),
        in_specs=[
            pl.BlockSpec((tm, TOP_K * d), lambda i: (i, 0)),
            pl.BlockSpec((tm, LANES), lambda i: (i, 0)),
            pl.BlockSpec((tm, d), lambda i: (i, 0)),
            pl.BlockSpec((1, 1, d), lambda i: (nw_idx, 0, 0)),
            pl.BlockSpec((1, 1, d), lambda i: (i // per_b, 0, g_idx)),
        ],
        out_specs=pl.BlockSpec((tm, d), lambda i: (i, 0)),
        out_shape=jax.ShapeDtypeStruct((t, d), F32),
        compiler_params=_cp(("parallel",)),
        name="combine",
    )(y4, route, x2, nw3, mod3)


def _reorder_w_in(w):
    d = w.shape[0]
    parts = [
        w[:, 0:3072],
        w[:, 3080:3592],
        w[:, 3592:4616],
        w[:, 4632:5144],
        w[:, 5144:6680],
        w[:, 6684:14876],
        w[:, 3072:3080],
        w[:, 4616:4632],
        w[:, 6680:6684],
    ]
    used = sum(p.shape[1] for p in parts)
    parts.append(jnp.zeros((d, N_PROJ - used), w.dtype))
    return jnp.concatenate(parts, axis=1).astype(BF16)


def _routing_tables(route, t):
    tk = t * TOP_K
    rb = MOE_ROW_BLOCK
    flat_e = route[:, TOP_K:2 * TOP_K].astype(jnp.int32).reshape(-1)
    flat_tok = jnp.arange(tk, dtype=jnp.int32) // TOP_K
    order = jnp.argsort(flat_e)
    e_sorted = flat_e[order]
    counts = jnp.bincount(flat_e, length=N_EXPERTS)
    padded = (counts + rb - 1) // rb * rb
    pad_end = jnp.cumsum(padded)
    pad_start = pad_end - padded
    start = jnp.cumsum(counts) - counts
    slot = (pad_start[e_sorted] + jnp.arange(tk) - start[e_sorted]).astype(jnp.int32)
    n_blocks = -(-tk // rb) + N_EXPERTS
    p = n_blocks * rb
    slot_tok = jnp.zeros((p,), jnp.int32).at[slot].set(flat_tok[order])
    inv_slot = jnp.zeros((tk,), jnp.int32).at[order].set(slot)
    block_e = jnp.minimum(jnp.searchsorted(pad_end, jnp.arange(n_blocks) * rb, side='right'),
                          N_EXPERTS - 1).astype(jnp.int32)
    n_used = (pad_end[-1] // rb).astype(jnp.int32).reshape(1)
    return slot_tok, inv_slot, block_e, n_used


def _rope_tables(seq):
    half = HEAD_DIM // 2
    inv = ROPE_THETA ** (-jnp.arange(half, dtype=F32) / half)
    ang = jnp.arange(seq).astype(F32)[:, None] * inv[None, :]
    cos, sin = jnp.cos(ang), jnp.sin(ang)
    return jnp.concatenate([cos, cos], axis=-1), jnp.concatenate([-sin, sin], axis=-1)


def kernel(x, c, ada_w, ada_b, norm_w, w_in, conv_w, conv_b, mlstm_i_b, mlstm_f_b, mlstm_norm_w, gla_w_a2, gla_b_a2, gla_norm_w, fox_f_b, w_branch, w_out, router_w, router_b, w_up, b_up, w_down, b_down):
    batch, seq, d = x.shape
    depth = ada_w.shape[0]
    t = batch * seq
    x2 = x.reshape(t, d)
    mod = _ada_mod(c, ada_w, ada_b)
    cos, sin = _rope_tables(seq)

    for l in range(depth):
        mod3 = mod[l].reshape(batch, 1, 6 * d)
        nw3 = norm_w[l].reshape(4, 1, d)

        proj = _in_proj(x2, mod3, nw3, _reorder_w_in(w_in[l]), seq, nw_idx=0, sc_idx=1, sh_idx=0)
        small_bias = jnp.zeros((1, LANES), F32)
        small_bias = small_bias.at[0, SM_I:SM_I + BRANCH_HEADS].set(mlstm_i_b[l])
        small_bias = small_bias.at[0, SM_F:SM_F + BRANCH_HEADS].set(mlstm_f_b[l])
        small_bias = small_bias.at[0, SM_FF:SM_FF + BRANCH_HEADS].set(fox_f_b[l])
        w_a2_pad = jnp.zeros((LANES, GLA_HEADS * GLA_DK), F32).at[SM_GA:SM_GA + GLA_RANK].set(gla_w_a2[l])

        y_a = _moba(proj, cos, sin, batch, seq)
        y_b = _mlstm(proj, small_bias, conv_w[l], conv_b[l], mlstm_norm_w[l], batch, seq)
        y_c = _gla(proj, w_a2_pad, gla_b_a2[l], gla_norm_w[l], batch, seq)
        y_d = _fox(proj, small_bias, batch, seq)
        merged = _merge((y_a, y_b, y_c, y_d), proj, w_branch[l].astype(BF16))
        x2 = _out_proj(merged, w_out[l].astype(BF16), x2, nw3, mod3, seq, nw_idx=1, g_idx=2)

        rw_pad = jnp.zeros((d, LANES), F32).at[:, :N_EXPERTS].set(router_w[l])
        rb_pad = jnp.zeros((1, LANES), F32).at[0, :N_EXPERTS].set(router_b[l])
        h2, route = _router(x2, nw3, mod3, rw_pad, rb_pad, seq, nw_idx=2, sc_idx=4, sh_idx=3)
        slot_tok, inv_slot, block_e, n_used = _routing_tables(route, t)
        x_sorted = _gather_rows(h2, slot_tok, n_used)
        yb = _experts(
            x_sorted, block_e, n_used,
            w_up[l][:, :, 0::2].astype(BF16), w_up[l][:, :, 1::2].astype(BF16),
            b_up[l][:, None, 0::2], b_up[l][:, None, 1::2],
            w_down[l].astype(BF16), b_down[l][:, None, :])
        n_all = jnp.full((1,), (t * TOP_K) // MOE_ROW_BLOCK, jnp.int32)
        y4 = _gather_rows(yb, inv_slot, n_all).reshape(t, TOP_K * d)
        x2 = _combine(y4, route, x2, nw3, mod3, seq, nw_idx=3, g_idx=5)

    return x2.reshape(batch, seq, d)
```

```python
import functools

import jax
import jax.numpy as jnp
from jax import lax
from jax.experimental import pallas as pl
from jax.experimental.pallas import tpu as pltpu

F32 = jnp.float32
BF16 = jnp.bfloat16
NEG_INF = float("-inf")
MASK_PENALTY = -1e9

D_MODEL = 2048
HEAD_DIM = 128
N_BRANCH = 4
BRANCH_HEADS = 4
BRANCH_WIDTH = BRANCH_HEADS * HEAD_DIM

MOBA_BLOCK = 256
MOBA_TOPK = 3
ROPE_THETA = 10000.0

MLSTM_CHUNK = 128
MLSTM_CONV = 4

GLA_HEADS = 4
GLA_DK = 64
GLA_DV = 128
GLA_RANK = 16
GLA_TAU = 16.0
GLA_CHUNK = 64
GLA_SUB = 16

N_EXPERTS = 32
TOP_K = 4
D_EXPERT = D_MODEL // 2
SWIGLU_LIMIT = 7.0
SWIGLU_ALPHA = 1.702
MOE_ROW_BLOCK = 512

RMS_EPS = 1e-6

LANES = 128
MXU_DIM = 256
CHUNK = 128
ATT_TILE = 512

C_AQ, C_AK, C_AV = 0, 512, 1024
C_MQ, C_MK, C_MV, C_MO = 1536, 2048, 2560, 3072
C_GQ, C_GK, C_GV, C_GR = 3584, 3840, 4096, 4608
C_FQ, C_FK, C_FV = 5120, 5632, 6144
C_GATE = 6656
C_SMALL = 14848
SM_I, SM_F, SM_GA, SM_FF = 0, 4, 8, 24
N_PROJ = 15360

NN = (((1,), (0,)), ((), ()))
NT = (((1,), (1,)), ((), ()))
TN = (((0,), (0,)), ((), ()))

VMEM_LIMIT = 56 * 1024 * 1024


def _cp(sem, vmem=VMEM_LIMIT):
    return pltpu.CompilerParams(dimension_semantics=sem, vmem_limit_bytes=vmem)


def _dot(a, b, dn=NN):
    return lax.dot_general(a, b, dn, preferred_element_type=F32)


def _split3(x):
    hi = x.astype(BF16)
    r1 = x - hi.astype(F32)
    mid = r1.astype(BF16)
    lo = (r1 - mid.astype(F32)).astype(BF16)
    return hi, mid, lo


def _dot01(sel, x, dn=NN):
    hi, mid, lo = _split3(x)
    return _dot(sel, hi, dn) + _dot(sel, mid, dn) + _dot(sel, lo, dn)


def _dot_x3(a, b, dn=NN):
    ah = a.astype(BF16)
    al = (a - ah.astype(F32)).astype(BF16)
    bh = b.astype(BF16)
    bl = (b - bh.astype(F32)).astype(BF16)
    return _dot(ah, bh, dn) + _dot(al, bh, dn) + _dot(ah, bl, dn)


def _log_sigmoid(x):
    return jnp.minimum(x, 0.0) - jnp.log1p(jnp.exp(-jnp.abs(x)))


def _sigmoid(x):
    return 1.0 / (1.0 + jnp.exp(-x))


def _tril_bf16(n):
    r = lax.broadcasted_iota(jnp.int32, (n, n), 0)
    c = lax.broadcasted_iota(jnp.int32, (n, n), 1)
    return jnp.where(c <= r, 1.0, 0.0).astype(BF16)


def _row_selector(lane_idx):
    r = lax.broadcasted_iota(jnp.int32, (8, LANES), 0)
    c = lax.broadcasted_iota(jnp.int32, (8, LANES), 1)
    return jnp.where((r == 0) & (c == lane_idx), 1.0, 0.0).astype(BF16)


def _pick_lane(x, lane_idx):
    lane = lax.broadcasted_iota(jnp.int32, x.shape, 1)
    return jnp.sum(jnp.where(lane == lane_idx, x, 0.0), axis=-1, keepdims=True)


def _ada_kernel(c_ref, w_ref, b_ref, o_ref):
    c = c_ref[...]
    cond = c * _sigmoid(c)
    o_ref[0] = _dot(cond.astype(BF16), w_ref[0].astype(BF16)) + b_ref[0]


def _ada_mod(c, ada_w, ada_b):
    depth, d, n = ada_w.shape
    b = c.shape[0]
    tn = 1024
    return pl.pallas_call(
        _ada_kernel,
        grid=(depth, n // tn),
        in_specs=[
            pl.BlockSpec((b, d), lambda l, j: (0, 0)),
            pl.BlockSpec((1, d, tn), lambda l, j: (l, 0, j)),
            pl.BlockSpec((1, 1, tn), lambda l, j: (l, 0, j)),
        ],
        out_specs=pl.BlockSpec((1, b, tn), lambda l, j: (l, 0, j)),
        out_shape=jax.ShapeDtypeStruct((depth, b, n), F32),
        compiler_params=_cp(("parallel", "parallel")),
        name="ada_mod",
    )(c, ada_w, ada_b.reshape(depth, 1, n))


def _norm_mod_rows(x, nw, sc, sh):
    var = jnp.mean(x * x, axis=-1, keepdims=True)
    y = x * lax.rsqrt(var + RMS_EPS) * nw
    return y * (1.0 + sc) + sh


def _rms_rows(y, nw):
    var = jnp.mean(y * y, axis=-1, keepdims=True)
    return y * lax.rsqrt(var + RMS_EPS) * nw


def _inproj_kernel(x_ref, nw_ref, sc_ref, sh_ref, w_ref, o_ref, h_scr, *, rows):
    @pl.when(pl.program_id(1) == 0)
    def _():
        def body(r, carry):
            sl = pl.ds(pl.multiple_of(r * rows, rows), rows)
            h = _norm_mod_rows(x_ref[sl, :], nw_ref[0], sc_ref[0], sh_ref[0])
            h_scr[sl, :] = h.astype(BF16)
            return carry
        lax.fori_loop(0, x_ref.shape[0] // rows, body, 0)

    o_ref[...] = _dot(h_scr[...], w_ref[...])


def _in_proj(x2, mod3, nw3, w_re, seq, *, nw_idx, sc_idx, sh_idx):
    t, d = x2.shape
    n = w_re.shape[1]
    tm = min(1024, seq)
    tn = 1024
    per_b = seq // tm
    return pl.pallas_call(
        functools.partial(_inproj_kernel, rows=min(256, tm)),
        grid=(t // tm, n // tn),
        in_specs=[
            pl.BlockSpec((tm, d), lambda i, j: (i, 0)),
            pl.BlockSpec((1, 1, d), lambda i, j: (nw_idx, 0, 0)),
            pl.BlockSpec((1, 1, d), lambda i, j: (i // per_b, 0, sc_idx)),
            pl.BlockSpec((1, 1, d), lambda i, j: (i // per_b, 0, sh_idx)),
            pl.BlockSpec((d, tn), lambda i, j: (0, j)),
        ],
        out_specs=pl.BlockSpec((tm, tn), lambda i, j: (i, j)),
        out_shape=jax.ShapeDtypeStruct((t, n), F32),
        scratch_shapes=[pltpu.VMEM((tm, d), BF16)],
        compiler_params=_cp(("parallel", "arbitrary")),
        name="in_proj",
    )(x2, nw3, mod3, mod3, w_re)


def _attention_sweep(i, tile, q_aug, kaug_scr, vb_scr, o_ref):
    rows_q = pl.ds(pl.multiple_of(i * tile, tile), tile)
    s = _dot(q_aug, kaug_scr[rows_q, :], NT)
    r_io = lax.broadcasted_iota(jnp.int32, (tile, tile), 0)
    c_io = lax.broadcasted_iota(jnp.int32, (tile, tile), 1)
    s = jnp.where(c_io <= r_io, s, NEG_INF)
    m0 = jnp.max(s, axis=-1, keepdims=True)
    p = jnp.exp(s - m0)
    l0 = jnp.sum(p, axis=-1, keepdims=True)
    acc0 = _dot(p.astype(BF16), vb_scr[rows_q, :])

    def body(j, carry):
        m, l, acc = carry
        rows = pl.ds(pl.multiple_of(j * tile, tile), tile)
        s = _dot(q_aug, kaug_scr[rows, :], NT)
        m_new = jnp.maximum(m, jnp.max(s, axis=-1, keepdims=True))
        alpha = jnp.exp(m - m_new)
        p = jnp.exp(s - m_new)
        l = alpha * l + jnp.sum(p, axis=-1, keepdims=True)
        acc = alpha * acc + _dot(p.astype(BF16), vb_scr[rows, :])
        return m_new, l, acc

    _, l, acc = lax.fori_loop(0, i, body, (m0, l0, acc0))
    o_ref[...] = (acc / l).astype(o_ref.dtype)


def _rope(x, cos, sin):
    return x * cos + pltpu.roll(x, HEAD_DIM // 2, 1) * sin


def _moba_kernel(q_ref, k_ref, v_ref, cos_ref, sin_ref, o_ref, kaug_scr, vb_scr, km_scr, *, seq, tile, n_sel):
    blk = MOBA_BLOCK
    i = pl.program_id(2)

    @pl.when(i == 0)
    def _():
        km_scr[...] = jnp.zeros_like(km_scr)
        lane_b = lax.broadcasted_iota(jnp.int32, (blk, LANES), 1)
        for j in range(seq // blk):
            rows = pl.ds(j * blk, blk)
            kk = _rope(k_ref[rows, :], cos_ref[rows, :], sin_ref[rows, :])
            kaug_scr[rows, 0:HEAD_DIM] = kk.astype(BF16)
            kaug_scr[rows, HEAD_DIM:2 * HEAD_DIM] = jnp.where(lane_b == j, 1.0, 0.0).astype(BF16)
            km_scr[j:j + 1, :] = jnp.mean(kk, axis=0, keepdims=True)
            vb_scr[rows, :] = v_ref[rows, :].astype(BF16)

    rows_q = pl.ds(pl.multiple_of(i * tile, tile), tile)
    q = _rope(q_ref[...], cos_ref[rows_q, :], sin_ref[rows_q, :])
    lane = lax.broadcasted_iota(jnp.int32, (tile, LANES), 1)
    lane_f = lane.astype(F32)
    row = lax.broadcasted_iota(jnp.int32, (tile, LANES), 0)
    q_blk = i * (tile // blk) + row // blk
    g = jnp.where(lane < q_blk, _dot_x3(q, km_scr[...], NT), NEG_INF)
    keep = lane == q_blk
    for _ in range(n_sel):
        m = jnp.max(g, axis=-1, keepdims=True)
        idx = jnp.min(jnp.where(g == m, lane_f, float(LANES)), axis=-1, keepdims=True)
        hit = lane_f == idx
        keep = keep | (hit & (m > NEG_INF))
        g = jnp.where(hit, NEG_INF, g)
    penalty = jnp.where(keep, 0.0, MASK_PENALTY)
    q_aug = jnp.concatenate([(q * (HEAD_DIM ** -0.5)).astype(BF16), penalty.astype(BF16)], axis=1)
    _attention_sweep(i, tile, q_aug, kaug_scr, vb_scr, o_ref)


def _moba(proj, cos, sin, batch, seq):
    tile = min(ATT_TILE, seq)
    nq = seq // tile
    nblk = seq // MOBA_BLOCK
    n_sel = max(1, min(MOBA_TOPK, nblk - 1))
    hb = HEAD_DIM
    return pl.pallas_call(
        functools.partial(_moba_kernel, seq=seq, tile=tile, n_sel=n_sel),
        grid=(batch, BRANCH_HEADS, nq),
        in_specs=[
            pl.BlockSpec((tile, hb), lambda b, h, i: (b * nq + i, C_AQ // hb + h)),
            pl.BlockSpec((seq, hb), lambda b, h, i: (b, C_AK // hb + h)),
            pl.BlockSpec((seq, hb), lambda b, h, i: (b, C_AV // hb + h)),
            pl.BlockSpec((seq, hb), lambda b, h, i: (0, 0)),
            pl.BlockSpec((seq, hb), lambda b, h, i: (0, 0)),
        ],
        out_specs=pl.BlockSpec((tile, hb), lambda b, h, i: (b * nq + i, h)),
        out_shape=jax.ShapeDtypeStruct((batch * seq, BRANCH_WIDTH), BF16),
        scratch_shapes=[pltpu.VMEM((seq, 2 * hb), BF16), pltpu.VMEM((seq, hb), BF16), pltpu.VMEM((LANES, hb), F32)],
        compiler_params=_cp(("parallel", "parallel", "arbitrary")),
        name="moba",
    )(proj, proj, proj, cos, sin)


def _fox_kernel(q_ref, k_ref, v_ref, sm_ref, bias_ref, o_ref, kaug_scr, vb_scr, fcol_scr, *, seq, tile):
    h = pl.program_id(1)
    i = pl.program_id(2)
    f_lane = SM_FF + h

    def bias_lanes(f, first):
        hi, mid, lo = _split3(f)
        lane = lax.broadcasted_iota(jnp.int32, (f.shape[0], LANES), 1)
        ones_first = 3 - first
        out = jnp.where((lane >= ones_first) & (lane < ones_first + 3), 1.0, 0.0)
        out = jnp.where(lane == first, hi.astype(F32), out)
        out = jnp.where(lane == first + 1, mid.astype(F32), out)
        out = jnp.where(lane == first + 2, lo.astype(F32), out)
        return out.astype(BF16)

    @pl.when(i == 0)
    def _():
        tril = _tril_bf16(CHUNK)
        carry = jnp.zeros((1, LANES), F32)
        for c in range(seq // CHUNK):
            rows = pl.ds(c * CHUNK, CHUNK)
            lf = _log_sigmoid(sm_ref[rows, :] + bias_ref[...])
            cs = _dot01(tril, lf) + carry
            fcol_scr[rows, :] = cs
            carry = cs[CHUNK - 1:CHUNK, :]
        for c in range(seq // tile):
            rows = pl.ds(c * tile, tile)
            kaug_scr[rows, 0:HEAD_DIM] = k_ref[rows, :].astype(BF16)
            kaug_scr[rows, HEAD_DIM:2 * HEAD_DIM] = bias_lanes(-_pick_lane(fcol_scr[rows, :], f_lane), 3)
            vb_scr[rows, :] = v_ref[rows, :].astype(BF16)

    rows_q = pl.ds(pl.multiple_of(i * tile, tile), tile)
    f_t = _pick_lane(fcol_scr[rows_q, :], f_lane)
    q_aug = jnp.concatenate([(q_ref[...] * (HEAD_DIM ** -0.5)).astype(BF16), bias_lanes(f_t, 0)], axis=1)
    _attention_sweep(i, tile, q_aug, kaug_scr, vb_scr, o_ref)


def _fox(proj, small_bias, batch, seq):
    tile = min(ATT_TILE, seq)
    nq = seq // tile
    hb = HEAD_DIM
    return pl.pallas_call(
        functools.partial(_fox_kernel, seq=seq, tile=tile),
        grid=(batch, BRANCH_HEADS, nq),
        in_specs=[
            pl.BlockSpec((tile, hb), lambda b, h, i: (b * nq + i, C_FQ // hb + h)),
            pl.BlockSpec((seq, hb), lambda b, h, i: (b, C_FK // hb + h)),
            pl.BlockSpec((seq, hb), lambda b, h, i: (b, C_FV // hb + h)),
            pl.BlockSpec((seq, LANES), lambda b, h, i: (b, C_SMALL // LANES)),
            pl.BlockSpec((1, LANES), lambda b, h, i: (0, 0)),
        ],
        out_specs=pl.BlockSpec((tile, hb), lambda b, h, i: (b * nq + i, h)),
        out_shape=jax.ShapeDtypeStruct((batch * seq, BRANCH_WIDTH), BF16),
        scratch_shapes=[pltpu.VMEM((seq, 2 * hb), BF16), pltpu.VMEM((seq, hb), BF16), pltpu.VMEM((seq, LANES), F32)],
        compiler_params=_cp(("parallel", "parallel", "arbitrary")),
        name="fox",
    )(proj, proj, proj, proj, small_bias)


def _causal_conv_silu(x, w_ref, b_ref):
    k = w_ref.shape[0]
    row = lax.broadcasted_iota(jnp.int32, x.shape, 0)
    y = x * w_ref[k - 1:k, :] + b_ref[...]
    for sft in range(1, k):
        xs = jnp.where(row >= sft, pltpu.roll(x, sft, 0), 0.0)
        y = y + xs * w_ref[k - 1 - sft:k - sft, :]
    return y * _sigmoid(y)


def _mlstm_kernel(q_ref, k_ref, v_ref, o_ref, sm_ref, bias_ref, cwq_ref, cwk_ref, cbq_ref, cbk_ref,
                  nw_ref, y_ref, q_scr, k_scr, x_scr, g_scr, irow_scr, grow_scr, *, seq):
    h = pl.program_id(1)
    L = MLSTM_CHUNK
    nc = seq // L
    i_lane = SM_I + h
    f_lane = SM_F + h

    q_scr[...] = _causal_conv_silu(q_ref[...], cwq_ref, cbq_ref)
    k_scr[...] = _causal_conv_silu(k_ref[...], cwk_ref, cbk_ref) * (HEAD_DIM ** -0.5)

    smb = sm_ref[...] + bias_ref[...]
    lane_s = lax.broadcasted_iota(jnp.int32, smb.shape, 1)
    x_scr[...] = jnp.where(lane_s < SM_F, smb, _log_sigmoid(smb))
    tril = _tril_bf16(L)
    for c in range(nc):
        rows = pl.ds(c * L, L)
        g_scr[rows, :] = _dot01(tril, x_scr[rows, :])
    irow_scr[...] = _dot01(_row_selector(i_lane), x_scr[...], NT)
    grow_scr[...] = _dot01(_row_selector(f_lane), g_scr[...], NT)

    r_io = lax.broadcasted_iota(jnp.int32, (L, L), 0)
    c_io = lax.broadcasted_iota(jnp.int32, (L, L), 1)
    causal = c_io <= r_io

    def step(c, carry):
        C, n, m = carry
        rows = pl.ds(pl.multiple_of(c * L, L), L)
        qc = q_scr[rows, :]
        kc = k_scr[rows, :]
        vb = v_ref[rows, :].astype(BF16)
        qb = qc.astype(BF16)
        g_col = _pick_lane(g_scr[rows, :], f_lane)
        i_col = _pick_lane(x_scr[rows, :], i_lane)
        g_row = grow_scr[0:1, rows]
        i_row = irow_scr[0:1, rows]
        dmat = jnp.where(causal, g_col - g_row + i_row, NEG_INF)
        a = g_col + m
        m_out = jnp.maximum(a, jnp.max(dmat, axis=-1, keepdims=True))
        w = jnp.exp(dmat - m_out) * _dot(qb, kc.astype(BF16), NT)
        inter = jnp.exp(a - m_out)
        num = inter * _dot(qb, C.astype(BF16)) + _dot(w.astype(BF16), vb)
        den = inter * jnp.sum(qc * n, axis=-1, keepdims=True) + jnp.sum(w, axis=-1, keepdims=True)
        hh = num / jnp.maximum(jnp.abs(den), jnp.exp(-m_out))
        g_last = g_row[:, L - 1:L]
        lw = g_last - g_col + i_col
        m_new = jnp.maximum(g_last + m, jnp.max(lw, axis=0, keepdims=True))
        wk = jnp.exp(lw - m_new)
        decay = jnp.exp(g_last + m - m_new)
        kw = kc * wk
        C = decay * C + _dot(kw.astype(BF16), vb, TN)
        n = decay * n + jnp.sum(kw, axis=0, keepdims=True)
        yn = _rms_rows(hh, nw_ref[0])
        y_ref[rows, :] = (yn * _sigmoid(o_ref[rows, :])).astype(y_ref.dtype)
        return C, n, m_new

    init = (jnp.zeros((HEAD_DIM, HEAD_DIM), F32), jnp.zeros((1, HEAD_DIM), F32), jnp.zeros((1, 1), F32))
    lax.fori_loop(0, nc, step, init)


def _mlstm(proj, small_bias, conv_w, conv_b, norm_w, batch, seq):
    hb = HEAD_DIM
    nh = BRANCH_HEADS
    blk = lambda c0: pl.BlockSpec((seq, hb), lambda b, h: (b, c0 // hb + h))
    return pl.pallas_call(
        functools.partial(_mlstm_kernel, seq=seq),
        grid=(batch, nh),
        in_specs=[
            blk(C_MQ), blk(C_MK), blk(C_MV), blk(C_MO),
            pl.BlockSpec((seq, LANES), lambda b, h: (b, C_SMALL // LANES)),
            pl.BlockSpec((1, LANES), lambda b, h: (0, 0)),
            pl.BlockSpec((MLSTM_CONV, hb), lambda b, h: (0, h)),
            pl.BlockSpec((MLSTM_CONV, hb), lambda b, h: (0, nh + h)),
            pl.BlockSpec((1, hb), lambda b, h: (0, h)),
            pl.BlockSpec((1, hb), lambda b, h: (0, nh + h)),
            pl.BlockSpec((1, 1, hb), lambda b, h: (h, 0, 0)),
        ],
        out_specs=pl.BlockSpec((seq, hb), lambda b, h: (b, h)),
        out_shape=jax.ShapeDtypeStruct((batch * seq, BRANCH_WIDTH), BF16),
        scratch_shapes=[
            pltpu.VMEM((seq, hb), F32), pltpu.VMEM((seq, hb), F32),
            pltpu.VMEM((seq, LANES), F32), pltpu.VMEM((seq, LANES), F32),
            pltpu.VMEM((8, seq), F32), pltpu.VMEM((8, seq), F32),
        ],
        compiler_params=_cp(("parallel", "parallel")),
        name="mlstm",
    )(proj, proj, proj, proj, proj, small_bias, conv_w, conv_w, conv_b.reshape(1, -1), conv_b.reshape(1, -1),
      norm_w.reshape(nh, 1, hb))


def _gla_kernel(q_ref, k_ref, v_ref, r_ref, sm_ref, wa_ref, ba_ref, nw_ref, y_ref, la_scr, *, seq):
    L = GLA_CHUNK
    sub = GLA_SUB
    nc = seq // L
    dk, dv = GLA_DK, GLA_DV

    z = _dot_x3(sm_ref[...], wa_ref[...]) + ba_ref[...]
    la_scr[...] = _log_sigmoid(z) * (1.0 / GLA_TAU)
    tril = _tril_bf16(L)
    r_io = lax.broadcasted_iota(jnp.int32, (sub, L), 0)
    c_io = lax.broadcasted_iota(jnp.int32, (sub, L), 1)

    def step(c, states):
        rows = pl.ds(pl.multiple_of(c * L, L), L)
        bc2 = _dot01(tril, la_scr[rows, :])
        q2 = q_ref[rows, :] * (dk ** -0.5)
        k2 = k_ref[rows, :]
        new_states = []
        for hh in range(2):
            st = states[hh]
            lanes = slice(hh * dk, (hh + 1) * dk)
            b = bc2[:, lanes]
            qh = q2[:, lanes]
            kh = k2[:, lanes]
            vb = v_ref[rows, hh * dv:(hh + 1) * dv].astype(BF16)
            o = _dot((qh * jnp.exp(b)).astype(BF16), st.astype(BF16), NT)
            blocks = []
            for ib in range(L // sub):
                ref_row = b[ib * sub:ib * sub + 1, :]
                qi = qh[ib * sub:(ib + 1) * sub, :] * jnp.exp(b[ib * sub:(ib + 1) * sub, :] - ref_row)
                upto = (ib + 1) * sub
                row_l = lax.broadcasted_iota(jnp.int32, (L, dk), 0)
                ki = kh * jnp.exp(jnp.where(row_l < upto, ref_row - b, 0.0))
                a = _dot(qi.astype(BF16), ki.astype(BF16), NT)
                blocks.append(jnp.where(c_io <= r_io + ib * sub, a, 0.0))
            attn = jnp.concatenate(blocks, axis=0)
            o = o + _dot(attn.astype(BF16), vb)
            b_last = b[L - 1:L, :]
            ke = kh * jnp.exp(b_last - b)
            st = jnp.exp(b_last) * st + _dot(vb, ke.astype(BF16), TN)
            new_states.append(st)
            yn = _rms_rows(o, nw_ref[0, hh:hh + 1, :])
            r = r_ref[rows, hh * dv:(hh + 1) * dv]
            y_ref[rows, hh * dv:(hh + 1) * dv] = (yn * (r * _sigmoid(r))).astype(y_ref.dtype)
        return tuple(new_states)

    init = (jnp.zeros((dv, dk), F32), jnp.zeros((dv, dk), F32))
    lax.fori_loop(0, nc, step, init)


def _gla(proj, w_a2_pad, b_a2, norm_w, batch, seq):
    return pl.pallas_call(
        functools.partial(_gla_kernel, seq=seq),
        grid=(batch, GLA_HEADS // 2),
        in_specs=[
            pl.BlockSpec((seq, LANES), lambda b, p: (b, C_GQ // LANES + p)),
            pl.BlockSpec((seq, LANES), lambda b, p: (b, C_GK // LANES + p)),
            pl.BlockSpec((seq, 2 * GLA_DV), lambda b, p: (b, C_GV // (2 * GLA_DV) + p)),
            pl.BlockSpec((seq, 2 * GLA_DV), lambda b, p: (b, C_GR // (2 * GLA_DV) + p)),
            pl.BlockSpec((seq, LANES), lambda b, p: (b, C_SMALL // LANES)),
            pl.BlockSpec((LANES, LANES), lambda b, p: (0, p)),
            pl.BlockSpec((1, LANES), lambda b, p: (0, p)),
            pl.BlockSpec((1, 2, GLA_DV), lambda b, p: (p, 0, 0)),
        ],
        out_specs=pl.BlockSpec((seq, 2 * GLA_DV), lambda b, p: (b, p)),
        out_shape=jax.ShapeDtypeStruct((batch * seq, GLA_HEADS * GLA_DV), BF16),
        scratch_shapes=[pltpu.VMEM((seq, LANES), F32)],
        compiler_params=_cp(("parallel", "parallel")),
        name="gla",
    )(proj, proj, proj, proj, proj, w_a2_pad, b_a2.reshape(1, -1), norm_w.reshape(GLA_HEADS // 2, 2, GLA_DV))


def _merge_kernel(ya_ref, yb_ref, yc_ref, yd_ref, g0_ref, g1_ref, g2_ref, g3_ref, w_ref, o_ref):
    ys = (ya_ref, yb_ref, yc_ref, yd_ref)
    gs = (g0_ref, g1_ref, g2_ref, g3_ref)
    acc = None
    for b in range(N_BRANCH):
        term = _sigmoid(gs[b][...]) * _dot(ys[b][...], w_ref[b])
        acc = term if acc is None else acc + term
    o_ref[...] = acc.astype(o_ref.dtype)


def _merge(ys, proj, w_branch_bf16):
    t = proj.shape[0]
    d = D_MODEL
    tm, tn = 1024, 512
    tm = min(tm, t)
    yspec = pl.BlockSpec((tm, BRANCH_WIDTH), lambda i, j: (i, 0))
    gspec = lambda b: pl.BlockSpec((tm, tn), lambda i, j: (i, (C_GATE + b * d) // tn + j))
    return pl.pallas_call(
        _merge_kernel,
        grid=(t // tm, d // tn),
        in_specs=[yspec] * 4 + [gspec(b) for b in range(N_BRANCH)]
        + [pl.BlockSpec((N_BRANCH, BRANCH_WIDTH, tn), lambda i, j: (0, 0, j))],
        out_specs=pl.BlockSpec((tm, tn), lambda i, j: (i, j)),
        out_shape=jax.ShapeDtypeStruct((t, d), BF16),
        compiler_params=_cp(("parallel", "parallel")),
        name="merge",
    )(*ys, proj, proj, proj, proj, w_branch_bf16)


def _outproj_kernel(m_ref, w_ref, x_ref, nw_ref, g_ref, o_ref):
    y = _dot(m_ref[...], w_ref[...])
    o_ref[...] = x_ref[...] + g_ref[0] * _rms_rows(y, nw_ref[0])


def _out_proj(merged, w_out_bf16, x2, nw3, mod3, seq, *, nw_idx, g_idx):
    t, d = x2.shape
    tm = min(512, seq)
    per_b = seq // tm
    return pl.pallas_call(
        _outproj_kernel,
        grid=(t // tm,),
        in_specs=[
            pl.BlockSpec((tm, d), lambda i: (i, 0)),
            pl.BlockSpec((d, d), lambda i: (0, 0)),
            pl.BlockSpec((tm, d), lambda i: (i, 0)),
            pl.BlockSpec((1, 1, d), lambda i: (nw_idx, 0, 0)),
            pl.BlockSpec((1, 1, d), lambda i: (i // per_b, 0, g_idx)),
        ],
        out_specs=pl.BlockSpec((tm, d), lambda i: (i, 0)),
        out_shape=jax.ShapeDtypeStruct((t, d), F32),
        compiler_params=_cp(("parallel",)),
        name="out_proj",
    )(merged, w_out_bf16, x2, nw3, mod3)


def _router_kernel(x_ref, nw_ref, sc_ref, sh_ref, rw_ref, rb_ref, h_ref, r_ref):
    h = _norm_mod_rows(x_ref[...], nw_ref[0], sc_ref[0], sh_ref[0])
    h_ref[...] = h
    logits = _dot_x3(h, rw_ref[...]) + rb_ref[...]
    lane = lax.broadcasted_iota(jnp.int32, logits.shape, 1)
    lane_f = lane.astype(F32)
    l = jnp.where(lane < N_EXPERTS, logits, NEG_INF)
    vals, idxs = [], []
    for _ in range(TOP_K):
        m = jnp.max(l, axis=-1, keepdims=True)
        idx = jnp.min(jnp.where(l == m, lane_f, float(LANES)), axis=-1, keepdims=True)
        vals.append(m)
        idxs.append(idx)
        l = jnp.where(lane_f == idx, NEG_INF, l)
    es = [jnp.exp(v - vals[0]) for v in vals]
    tot = es[0]
    for e in es[1:]:
        tot = tot + e
    out = jnp.zeros(logits.shape, F32)
    for r in range(TOP_K):
        out = jnp.where(lane == r, es[r] / tot, out)
        out = jnp.where(lane == TOP_K + r, idxs[r], out)
    r_ref[...] = out


def _router(x2, nw3, mod3, router_w_pad, router_b_pad, seq, *, nw_idx, sc_idx, sh_idx):
    t, d = x2.shape
    tm = min(256, seq)
    per_b = seq // tm
    return pl.pallas_call(
        _router_kernel,
        grid=(t // tm,),
        in_specs=[
            pl.BlockSpec((tm, d), lambda i: (i, 0)),
            pl.BlockSpec((1, 1, d), lambda i: (nw_idx, 0, 0)),
            pl.BlockSpec((1, 1, d), lambda i: (i // per_b, 0, sc_idx)),
            pl.BlockSpec((1, 1, d), lambda i: (i // per_b, 0, sh_idx)),
            pl.BlockSpec((d, LANES), lambda i: (0, 0)),
            pl.BlockSpec((1, LANES), lambda i: (0, 0)),
        ],
        out_specs=[pl.BlockSpec((tm, d), lambda i: (i, 0)), pl.BlockSpec((tm, LANES), lambda i: (i, 0))],
        out_shape=[jax.ShapeDtypeStruct((t, d), F32), jax.ShapeDtypeStruct((t, LANES), F32)],
        compiler_params=_cp(("parallel",)),
        name="router",
    )(x2, nw3, mod3, mod3, router_w_pad, router_b_pad)


def _gather_kernel(nused_ref, idx_hbm, table_hbm, o_ref, idx_smem, isem, sem, *, rows):
    b = pl.program_id(0)

    def row_copy(tok, r):
        return pltpu.make_async_copy(table_hbm.at[pl.ds(tok, 1), :], o_ref.at[pl.ds(r, 1), :], sem)

    @pl.when(b < nused_ref[0])
    def _():
        icp = pltpu.make_async_copy(idx_hbm.at[b], idx_smem, isem)
        icp.start()
        icp.wait()

        def issue(r, carry):
            row_copy(idx_smem[r], r).start()
            return carry
        lax.fori_loop(0, rows, issue, 0, unroll=8)

        def drain(r, carry):
            row_copy(0, r).wait()
            return carry
        lax.fori_loop(0, rows, drain, 0, unroll=8)

    @pl.when(b >= nused_ref[0])
    def _():
        o_ref[...] = jnp.zeros_like(o_ref)


def _gather_rows(table, idx, n_used):
    rows = MOE_ROW_BLOCK
    n, d = idx.shape[0], table.shape[1]
    nb = n // rows
    grid_spec = pltpu.PrefetchScalarGridSpec(
        num_scalar_prefetch=1,
        grid=(nb,),
        in_specs=[pl.BlockSpec(memory_space=pl.ANY), pl.BlockSpec(memory_space=pl.ANY)],
        out_specs=pl.BlockSpec((rows, d), lambda b, nu: (b, 0)),
        scratch_shapes=[pltpu.SMEM((rows,), jnp.int32), pltpu.SemaphoreType.DMA, pltpu.SemaphoreType.DMA],
    )
    return pl.pallas_call(
        functools.partial(_gather_kernel, rows=rows),
        grid_spec=grid_spec,
        out_shape=jax.ShapeDtypeStruct((n, d), table.dtype),
        compiler_params=_cp(("arbitrary",)),
        name="gather_rows",
    )(n_used, idx.reshape(nb, rows), table)


def _deinterleave_kernel(w_ref, o_ref):
    n = MXU_DIM
    r = lax.broadcasted_iota(jnp.int32, (n, n), 0)
    c = lax.broadcasted_iota(jnp.int32, (n, n), 1)
    src = jnp.where(c < n // 2, 2 * c, 2 * (c - n // 2) + 1)
    perm = jnp.where(r == src, 1.0, 0.0).astype(BF16)
    for m in range(w_ref.shape[1] // n):
        cols = slice(m * n, (m + 1) * n)
        o_ref[:, cols] = _dot(w_ref[:, cols].astype(BF16), perm).astype(BF16)


def _deinterleave_cast(w2):
    rows, n = w2.shape
    tr = 1024
    return pl.pallas_call(
        _deinterleave_kernel,
        grid=(rows // tr,),
        in_specs=[pl.BlockSpec((tr, n), lambda i: (i, 0))],
        out_specs=pl.BlockSpec((tr, n), lambda i: (i, 0)),
        out_shape=jax.ShapeDtypeStruct((rows, n), BF16),
        compiler_params=_cp(("parallel",)),
        name="deinterleave_cast",
    )(w2)


def _expert_kernel(be_ref, nused_ref, x_ref, wu_ref, bu_ref, wd_ref, bd_ref, o_ref):
    b = pl.program_id(0)

    @pl.when(b < nused_ref[0])
    def _():
        x = x_ref[...].astype(BF16)
        hu = _dot(x, wu_ref[0]) + bu_ref[0]
        half = MXU_DIM // 2
        acts = []
        for m in range(hu.shape[1] // MXU_DIM):
            g = jnp.minimum(hu[:, m * MXU_DIM:m * MXU_DIM + half], SWIGLU_LIMIT)
            lin = jnp.clip(hu[:, m * MXU_DIM + half:(m + 1) * MXU_DIM], -SWIGLU_LIMIT, SWIGLU_LIMIT)
            acts.append((g * _sigmoid(SWIGLU_ALPHA * g) * (lin + 1.0)).astype(BF16))
        act = jnp.concatenate(acts, axis=1)
        o_ref[...] = _dot(act, wd_ref[0]) + bd_ref[0]

    @pl.when(b >= nused_ref[0])
    def _():
        o_ref[...] = jnp.zeros_like(o_ref)


def _experts(x_sorted, block_e, n_used, wu, bu, wd, bd):
    p, d = x_sorted.shape
    rows = MOE_ROW_BLOCK
    nb = p // rows
    de = D_EXPERT

    def xmap(b, be, nu):
        return (jnp.minimum(b, nu[0] - 1), 0)

    grid_spec = pltpu.PrefetchScalarGridSpec(
        num_scalar_prefetch=2,
        grid=(nb,),
        in_specs=[
            pl.BlockSpec((rows, d), xmap),
            pl.BlockSpec((1, d, 2 * de), lambda b, be, nu: (be[b], 0, 0)),
            pl.BlockSpec((1, 1, 2 * de), lambda b, be, nu: (be[b], 0, 0)),
            pl.BlockSpec((1, de, d), lambda b, be, nu: (be[b], 0, 0)),
            pl.BlockSpec((1, 1, d), lambda b, be, nu: (be[b], 0, 0)),
        ],
        out_specs=pl.BlockSpec((rows, d), lambda b, be, nu: (b, 0)),
    )
    return pl.pallas_call(
        _expert_kernel,
        grid_spec=grid_spec,
        out_shape=jax.ShapeDtypeStruct((p, d), F32),
        compiler_params=_cp(("arbitrary",)),
        name="experts",
    )(block_e, n_used, x_sorted, wu, bu, wd, bd)


def _combine_kernel(y0_ref, y1_ref, y2_ref, y3_ref, r_ref, x_ref, nw_ref, g_ref, o_ref):
    gates = r_ref[...]
    y = None
    for k, y_ref in enumerate((y0_ref, y1_ref, y2_ref, y3_ref)):
        term = gates[:, k:k + 1] * y_ref[0]
        y = term if y is None else y + term
    o_ref[...] = x_ref[...] + g_ref[0] * _rms_rows(y, nw_ref[0])


def _combine(y4, route, x2, nw3, mod3, seq, *, nw_idx, g_idx):
    t, d = x2.shape
    tm = min(256, seq)
    per_b = seq // tm
    yspec = lambda k: pl.BlockSpec((1, tm, d), lambda i: (k, i, 0))
    return pl.pallas_call(
        _combine_kernel,
        grid=(t // tm,),
        in_specs=[yspec(k) for k in range(TOP_K)] + [
            pl.BlockSpec((tm, LANES), lambda i: (i, 0)),
            pl.BlockSpec((tm, d), lambda i: (i, 0)),
            pl.BlockSpec((1, 1, d), lambda i: (nw_idx, 0, 0)),
            pl.BlockSpec((1, 1, d), lambda i: (i // per_b, 0, g_idx)),
        ],
        out_specs=pl.BlockSpec((tm, d), lambda i: (i, 0)),
        out_shape=jax.ShapeDtypeStruct((t, d), F32),
        compiler_params=_cp(("parallel",)),
        name="combine",
    )(y4, y4, y4, y4, route, x2, nw3, mod3)


def _reorder_w_in(w):
    d = w.shape[0]
    parts = [
        w[:, 0:3072],
        w[:, 3080:3592],
        w[:, 3592:4616],
        w[:, 4632:5144],
        w[:, 5144:6680],
        w[:, 6684:14876],
        w[:, 3072:3080],
        w[:, 4616:4632],
        w[:, 6680:6684],
    ]
    used = sum(p.shape[1] for p in parts)
    parts.append(jnp.zeros((d, N_PROJ - used), w.dtype))
    return jnp.concatenate(parts, axis=1).astype(BF16)


def _routing_tables(route, t):
    tk = t * TOP_K
    rb = MOE_ROW_BLOCK
    flat_e = route[:, TOP_K:2 * TOP_K].astype(jnp.int32).reshape(-1)
    flat_ids = jnp.arange(tk, dtype=jnp.int32)
    e_sorted, order = lax.sort((flat_e, flat_ids), num_keys=1, is_stable=True)
    experts = jnp.arange(N_EXPERTS, dtype=jnp.int32)
    counts = jnp.sum((flat_e[:, None] == experts[None, :]).astype(jnp.int32), axis=0)
    padded = (counts + rb - 1) // rb * rb
    pad_end = jnp.cumsum(padded)
    pad_start = pad_end - padded
    start = jnp.cumsum(counts) - counts
    slot_sorted = pad_start[e_sorted] + flat_ids - start[e_sorted]
    _, inv_slot = lax.sort((order, slot_sorted), num_keys=1)
    n_blocks = -(-tk // rb) + N_EXPERTS
    blk_first = jnp.arange(n_blocks, dtype=jnp.int32) * rb
    block_e = jnp.minimum(jnp.sum((pad_end[None, :] <= blk_first[:, None]).astype(jnp.int32), axis=1),
                          N_EXPERTS - 1).astype(jnp.int32)
    pos = jnp.arange(n_blocks * rb, dtype=jnp.int32)
    e_pos = jnp.repeat(block_e, rb)
    r_pos = pos - pad_start[e_pos]
    src = jnp.clip(start[e_pos] + r_pos, 0, tk - 1)
    slot_tok = jnp.where(r_pos < counts[e_pos], order[src] // TOP_K, 0).astype(jnp.int32)
    n_used = (pad_end[-1] // rb).astype(jnp.int32).reshape(1)
    return slot_tok, inv_slot.astype(jnp.int32), block_e, n_used


def _rope_tables(seq):
    half = HEAD_DIM // 2
    inv = ROPE_THETA ** (-jnp.arange(half, dtype=F32) / half)
    ang = jnp.arange(seq).astype(F32)[:, None] * inv[None, :]
    cos, sin = jnp.cos(ang), jnp.sin(ang)
    return jnp.concatenate([cos, cos], axis=-1), jnp.concatenate([-sin, sin], axis=-1)


def _deinterleave_bias(b_up):
    depth, e, n = b_up.shape
    half = MXU_DIM // 2
    return b_up.reshape(depth, e, n // MXU_DIM, half, 2).transpose(0, 1, 2, 4, 3).reshape(depth, e, 1, n)


def kernel(x, c, ada_w, ada_b, norm_w, w_in, conv_w, conv_b, mlstm_i_b, mlstm_f_b, mlstm_norm_w, gla_w_a2, gla_b_a2, gla_norm_w, fox_f_b, w_branch, w_out, router_w, router_b, w_up, b_up, w_down, b_down):
    batch, seq, d = x.shape
    depth = ada_w.shape[0]
    t = batch * seq
    x2 = x.reshape(t, d)
    mod = _ada_mod(c, ada_w, ada_b)
    cos, sin = _rope_tables(seq)
    w_up_d = _deinterleave_cast(w_up.reshape(depth * N_EXPERTS * d, 2 * D_EXPERT))
    w_up_d = w_up_d.reshape(depth, N_EXPERTS, d, 2 * D_EXPERT)
    b_up_d = _deinterleave_bias(b_up)

    for l in range(depth):
        mod3 = mod[l].reshape(batch, 1, 6 * d)
        nw3 = norm_w[l].reshape(4, 1, d)

        proj = _in_proj(x2, mod3, nw3, _reorder_w_in(w_in[l]), seq, nw_idx=0, sc_idx=1, sh_idx=0)
        small_bias = jnp.zeros((1, LANES), F32)
        small_bias = small_bias.at[0, SM_I:SM_I + BRANCH_HEADS].set(mlstm_i_b[l])
        small_bias = small_bias.at[0, SM_F:SM_F + BRANCH_HEADS].set(mlstm_f_b[l])
        small_bias = small_bias.at[0, SM_FF:SM_FF + BRANCH_HEADS].set(fox_f_b[l])
        w_a2_pad = jnp.zeros((LANES, GLA_HEADS * GLA_DK), F32).at[SM_GA:SM_GA + GLA_RANK].set(gla_w_a2[l])

        y_a = _moba(proj, cos, sin, batch, seq)
        y_b = _mlstm(proj, small_bias, conv_w[l], conv_b[l], mlstm_norm_w[l], batch, seq)
        y_c = _gla(proj, w_a2_pad, gla_b_a2[l], gla_norm_w[l], batch, seq)
        y_d = _fox(proj, small_bias, batch, seq)
        merged = _merge((y_a, y_b, y_c, y_d), proj, w_branch[l].astype(BF16))
        x2 = _out_proj(merged, w_out[l].astype(BF16), x2, nw3, mod3, seq, nw_idx=1, g_idx=2)

        rw_pad = jnp.zeros((d, LANES), F32).at[:, :N_EXPERTS].set(router_w[l])
        rb_pad = jnp.zeros((1, LANES), F32).at[0, :N_EXPERTS].set(router_b[l])
        h2, route = _router(x2, nw3, mod3, rw_pad, rb_pad, seq, nw_idx=2, sc_idx=4, sh_idx=3)
        slot_tok, inv_slot, block_e, n_used = _routing_tables(route, t)
        x_sorted = _gather_rows(h2, slot_tok, n_used)
        yb = _experts(x_sorted, block_e, n_used, w_up_d[l], b_up_d[l], w_down[l].astype(BF16),
                      b_down[l][:, None, :])
        n_all = jnp.full((1,), (t * TOP_K) // MOE_ROW_BLOCK, jnp.int32)
        inv_kmajor = inv_slot.reshape(t, TOP_K).T.reshape(-1)
        y4 = _gather_rows(yb, inv_kmajor, n_all).reshape(TOP_K, t, d)
        x2 = _combine(y4, route, x2, nw3, mod3, seq, nw_idx=3, g_idx=5)

    return x2.reshape(batch, seq, d)
```

```python
import functools

import jax
import jax.numpy as jnp
from jax import lax
from jax.experimental import pallas as pl
from jax.experimental.pallas import tpu as pltpu

F32 = jnp.float32
BF16 = jnp.bfloat16
NEG_INF = float("-inf")
MASK_PENALTY = -1e9

D_MODEL = 2048
HEAD_DIM = 128
N_BRANCH = 4
BRANCH_HEADS = 4
BRANCH_WIDTH = BRANCH_HEADS * HEAD_DIM

MOBA_BLOCK = 256
MOBA_TOPK = 3
ROPE_THETA = 10000.0

MLSTM_CHUNK = 128
MLSTM_CONV = 4

GLA_HEADS = 4
GLA_DK = 64
GLA_DV = 128
GLA_RANK = 16
GLA_TAU = 16.0
GLA_CHUNK = 64
GLA_SUB = 16

N_EXPERTS = 32
TOP_K = 4
D_EXPERT = D_MODEL // 2
SWIGLU_LIMIT = 7.0
SWIGLU_ALPHA = 1.702
MOE_ROW_BLOCK = 512

RMS_EPS = 1e-6

LANES = 128
MXU_DIM = 256
CHUNK = 128
ATT_TILE = 512

C_AQ, C_AK, C_AV = 0, 512, 1024
C_MQ, C_MK, C_MV, C_MO = 1536, 2048, 2560, 3072
C_GQ, C_GK, C_GV, C_GR = 3584, 3840, 4096, 4608
C_FQ, C_FK, C_FV = 5120, 5632, 6144
C_GATE = 6656
C_SMALL = 14848
SM_I, SM_F, SM_GA, SM_FF = 0, 4, 8, 24
N_PROJ = 15360

NN = (((1,), (0,)), ((), ()))
NT = (((1,), (1,)), ((), ()))
TN = (((0,), (0,)), ((), ()))

VMEM_LIMIT = 56 * 1024 * 1024


def _cp(sem, vmem=VMEM_LIMIT):
    return pltpu.CompilerParams(dimension_semantics=sem, vmem_limit_bytes=vmem)


def _dot(a, b, dn=NN):
    return lax.dot_general(a, b, dn, preferred_element_type=F32)


def _split3(x):
    hi = x.astype(BF16)
    r1 = x - hi.astype(F32)
    mid = r1.astype(BF16)
    lo = (r1 - mid.astype(F32)).astype(BF16)
    return hi, mid, lo


def _dot01(sel, x, dn=NN):
    hi, mid, lo = _split3(x)
    return _dot(sel, hi, dn) + _dot(sel, mid, dn) + _dot(sel, lo, dn)


def _dot_x3(a, b, dn=NN):
    ah = a.astype(BF16)
    al = (a - ah.astype(F32)).astype(BF16)
    bh = b.astype(BF16)
    bl = (b - bh.astype(F32)).astype(BF16)
    return _dot(ah, bh, dn) + _dot(al, bh, dn) + _dot(ah, bl, dn)


def _log_sigmoid(x):
    return jnp.minimum(x, 0.0) - jnp.log1p(jnp.exp(-jnp.abs(x)))


def _sigmoid(x):
    return 0.5 * jnp.tanh(0.5 * x) + 0.5


def _tril_bf16(n):
    r = lax.broadcasted_iota(jnp.int32, (n, n), 0)
    c = lax.broadcasted_iota(jnp.int32, (n, n), 1)
    return jnp.where(c <= r, 1.0, 0.0).astype(BF16)


def _row_selector(lane_idx):
    r = lax.broadcasted_iota(jnp.int32, (8, LANES), 0)
    c = lax.broadcasted_iota(jnp.int32, (8, LANES), 1)
    return jnp.where((r == 0) & (c == lane_idx), 1.0, 0.0).astype(BF16)


def _pick_lane(x, lane_idx):
    lane = lax.broadcasted_iota(jnp.int32, x.shape, 1)
    return jnp.sum(jnp.where(lane == lane_idx, x, 0.0), axis=-1, keepdims=True)


def _ada_kernel(c_ref, w_ref, b_ref, o_ref):
    c = c_ref[...]
    cond = c * _sigmoid(c)
    o_ref[0] = _dot(cond.astype(BF16), w_ref[0].astype(BF16)) + b_ref[0]


def _ada_mod(c, ada_w, ada_b):
    depth, d, n = ada_w.shape
    b = c.shape[0]
    tn = 1024
    return pl.pallas_call(
        _ada_kernel,
        grid=(depth, n // tn),
        in_specs=[
            pl.BlockSpec((b, d), lambda l, j: (0, 0)),
            pl.BlockSpec((1, d, tn), lambda l, j: (l, 0, j)),
            pl.BlockSpec((1, 1, tn), lambda l, j: (l, 0, j)),
        ],
        out_specs=pl.BlockSpec((1, b, tn), lambda l, j: (l, 0, j)),
        out_shape=jax.ShapeDtypeStruct((depth, b, n), F32),
        compiler_params=_cp(("parallel", "parallel")),
        name="ada_mod",
    )(c, ada_w, ada_b.reshape(depth, 1, n))


def _norm_mod_rows(x, nw, sc, sh):
    var = jnp.mean(x * x, axis=-1, keepdims=True)
    y = x * lax.rsqrt(var + RMS_EPS) * nw
    return y * (1.0 + sc) + sh


def _rms_rows(y, nw):
    var = jnp.mean(y * y, axis=-1, keepdims=True)
    return y * lax.rsqrt(var + RMS_EPS) * nw


def _inproj_kernel(x_ref, nw_ref, sc_ref, sh_ref, w_ref, o_ref, h_scr, *, rows):
    @pl.when(pl.program_id(1) == 0)
    def _():
        def body(r, carry):
            sl = pl.ds(pl.multiple_of(r * rows, rows), rows)
            h = _norm_mod_rows(x_ref[sl, :], nw_ref[0], sc_ref[0], sh_ref[0])
            h_scr[sl, :] = h.astype(BF16)
            return carry
        lax.fori_loop(0, x_ref.shape[0] // rows, body, 0)

    o_ref[...] = _dot(h_scr[...], w_ref[0])


def _in_proj(x2, mod3, nw3, w_re, layer, seq, *, nw_idx, sc_idx, sh_idx):
    t, d = x2.shape
    n = w_re.shape[2]
    tm = min(1024, seq)
    tn = 1024
    per_b = seq // tm
    return pl.pallas_call(
        functools.partial(_inproj_kernel, rows=min(256, tm)),
        grid=(t // tm, n // tn),
        in_specs=[
            pl.BlockSpec((tm, d), lambda i, j: (i, 0)),
            pl.BlockSpec((1, 1, d), lambda i, j: (nw_idx, 0, 0)),
            pl.BlockSpec((1, 1, d), lambda i, j: (i // per_b, 0, sc_idx)),
            pl.BlockSpec((1, 1, d), lambda i, j: (i // per_b, 0, sh_idx)),
            pl.BlockSpec((1, d, tn), lambda i, j: (layer, 0, j)),
        ],
        out_specs=pl.BlockSpec((tm, tn), lambda i, j: (i, j)),
        out_shape=jax.ShapeDtypeStruct((t, n), F32),
        scratch_shapes=[pltpu.VMEM((tm, d), BF16)],
        compiler_params=_cp(("parallel", "arbitrary")),
        name="in_proj",
    )(x2, nw3, mod3, mod3, w_re)


ATT_HEADS = 2


def _attention_sweep(i, tile, q_augs, kaug_scr, vb_scr, o_ref):
    nh = len(q_augs)
    rows_q = pl.ds(pl.multiple_of(i * tile, tile), tile)
    r_io = lax.broadcasted_iota(jnp.int32, (tile, tile), 0)
    c_io = lax.broadcasted_iota(jnp.int32, (tile, tile), 1)
    init = []
    for hh in range(nh):
        s = _dot(q_augs[hh], kaug_scr[hh, rows_q, :], NT)
        s = jnp.where(c_io <= r_io, s, NEG_INF)
        m0 = jnp.max(s, axis=-1, keepdims=True)
        p = jnp.exp(s - m0)
        init += [m0, jnp.sum(p, axis=-1, keepdims=True), _dot(p.astype(BF16), vb_scr[hh, rows_q, :])]

    def body(j, carry):
        rows = pl.ds(pl.multiple_of(j * tile, tile), tile)
        out = []
        for hh in range(nh):
            m, l, acc = carry[3 * hh:3 * hh + 3]
            s = _dot(q_augs[hh], kaug_scr[hh, rows, :], NT)
            m_new = jnp.maximum(m, jnp.max(s, axis=-1, keepdims=True))
            alpha = jnp.exp(m - m_new)
            p = jnp.exp(s - m_new)
            l = alpha * l + jnp.sum(p, axis=-1, keepdims=True)
            acc = alpha * acc + _dot(p.astype(BF16), vb_scr[hh, rows, :])
            out += [m_new, l, acc]
        return tuple(out)

    fin = lax.fori_loop(0, i, body, tuple(init))
    for hh in range(nh):
        o_ref[:, hh * HEAD_DIM:(hh + 1) * HEAD_DIM] = (fin[3 * hh + 2] / fin[3 * hh + 1]).astype(o_ref.dtype)


def _rope(x, cos, sin):
    return x * cos + pltpu.roll(x, HEAD_DIM // 2, 1) * sin


def _moba_kernel(q_ref, k_ref, v_ref, cos_ref, sin_ref, o_ref, kaug_scr, vb_scr, km_scr, *, seq, tile, n_sel):
    blk = MOBA_BLOCK
    nblk = seq // blk
    nb8 = -(-nblk // 8) * 8
    i = pl.program_id(2)

    @pl.when(i == 0)
    def _():
        km_scr[...] = jnp.zeros_like(km_scr)
        lane_b = lax.broadcasted_iota(jnp.int32, (blk, LANES), 1)
        for hh in range(ATT_HEADS):
            cols = slice(hh * HEAD_DIM, (hh + 1) * HEAD_DIM)
            for j in range(nblk):
                rows = pl.ds(j * blk, blk)
                kk = _rope(k_ref[rows, cols], cos_ref[rows, :], sin_ref[rows, :])
                kaug_scr[hh, rows, 0:HEAD_DIM] = kk.astype(BF16)
                kaug_scr[hh, rows, HEAD_DIM:2 * HEAD_DIM] = jnp.where(lane_b == j, 1.0, 0.0).astype(BF16)
                km_scr[hh, j:j + 1, :] = jnp.mean(kk, axis=0, keepdims=True)
                vb_scr[hh, rows, :] = v_ref[rows, cols].astype(BF16)

    rows_q = pl.ds(pl.multiple_of(i * tile, tile), tile)
    blk_id = lax.broadcasted_iota(jnp.int32, (nb8, tile), 0)
    blk_f = blk_id.astype(F32)
    q_blk = i * (tile // blk) + lax.broadcasted_iota(jnp.int32, (nb8, tile), 1) // blk
    q_augs = []
    for hh in range(ATT_HEADS):
        cols = slice(hh * HEAD_DIM, (hh + 1) * HEAD_DIM)
        q = _rope(q_ref[:, cols], cos_ref[rows_q, :], sin_ref[rows_q, :])
        g = _dot_x3(km_scr[hh], q, NT)[0:nb8, :]
        g = jnp.where(blk_id < q_blk, g, NEG_INF)
        keep = blk_id == q_blk
        for _ in range(n_sel):
            m = jnp.max(g, axis=0, keepdims=True)
            idx = jnp.min(jnp.where(g == m, blk_f, float(LANES)), axis=0, keepdims=True)
            hit = blk_f == idx
            keep = keep | (hit & (m > NEG_INF))
            g = jnp.where(hit, NEG_INF, g)
        pen_t = jnp.where(keep, 0.0, MASK_PENALTY)
        pen_t = jnp.concatenate([pen_t, jnp.full((LANES - nb8, tile), MASK_PENALTY, F32)], axis=0)
        penalty = pen_t.T
        q_augs.append(jnp.concatenate([(q * (HEAD_DIM ** -0.5)).astype(BF16), penalty.astype(BF16)], axis=1))
    _attention_sweep(i, tile, q_augs, kaug_scr, vb_scr, o_ref)


def _moba(proj, cos, sin, batch, seq):
    tile = min(ATT_TILE, seq)
    nq = seq // tile
    nblk = seq // MOBA_BLOCK
    n_sel = max(1, min(MOBA_TOPK, nblk - 1))
    hb = HEAD_DIM
    hw = ATT_HEADS * hb
    return pl.pallas_call(
        functools.partial(_moba_kernel, seq=seq, tile=tile, n_sel=n_sel),
        grid=(batch, BRANCH_HEADS // ATT_HEADS, nq),
        in_specs=[
            pl.BlockSpec((tile, hw), lambda b, p, i: (b * nq + i, C_AQ // hw + p)),
            pl.BlockSpec((seq, hw), lambda b, p, i: (b, C_AK // hw + p)),
            pl.BlockSpec((seq, hw), lambda b, p, i: (b, C_AV // hw + p)),
            pl.BlockSpec((seq, hb), lambda b, p, i: (0, 0)),
            pl.BlockSpec((seq, hb), lambda b, p, i: (0, 0)),
        ],
        out_specs=pl.BlockSpec((tile, hw), lambda b, p, i: (b * nq + i, p)),
        out_shape=jax.ShapeDtypeStruct((batch * seq, BRANCH_WIDTH), BF16),
        scratch_shapes=[pltpu.VMEM((ATT_HEADS, seq, 2 * hb), BF16), pltpu.VMEM((ATT_HEADS, seq, hb), BF16),
                        pltpu.VMEM((ATT_HEADS, LANES, hb), F32)],
        compiler_params=_cp(("parallel", "parallel", "arbitrary")),
        name="moba",
    )(proj, proj, proj, cos, sin)


def _fox_kernel(q_ref, k_ref, v_ref, sm_ref, bias_ref, o_ref, kaug_scr, vb_scr, fcol_scr, *, seq, tile):
    p_idx = pl.program_id(1)
    i = pl.program_id(2)

    def bias_lanes(f, first):
        hi, mid, lo = _split3(f)
        lane = lax.broadcasted_iota(jnp.int32, (f.shape[0], LANES), 1)
        ones_first = 3 - first
        out = jnp.where((lane >= ones_first) & (lane < ones_first + 3), 1.0, 0.0)
        out = jnp.where(lane == first, hi.astype(F32), out)
        out = jnp.where(lane == first + 1, mid.astype(F32), out)
        out = jnp.where(lane == first + 2, lo.astype(F32), out)
        return out.astype(BF16)

    @pl.when(i == 0)
    def _():
        tril = _tril_bf16(CHUNK)
        carry = jnp.zeros((1, LANES), F32)
        for c in range(seq // CHUNK):
            rows = pl.ds(c * CHUNK, CHUNK)
            lf = _log_sigmoid(sm_ref[rows, :] + bias_ref[...])
            cs = _dot01(tril, lf) + carry
            fcol_scr[rows, :] = cs
            carry = cs[CHUNK - 1:CHUNK, :]
        for hh in range(ATT_HEADS):
            cols = slice(hh * HEAD_DIM, (hh + 1) * HEAD_DIM)
            f_lane = SM_FF + p_idx * ATT_HEADS + hh
            for c in range(seq // tile):
                rows = pl.ds(c * tile, tile)
                kaug_scr[hh, rows, 0:HEAD_DIM] = k_ref[rows, cols].astype(BF16)
                kaug_scr[hh, rows, HEAD_DIM:2 * HEAD_DIM] = bias_lanes(-_pick_lane(fcol_scr[rows, :], f_lane), 3)
                vb_scr[hh, rows, :] = v_ref[rows, cols].astype(BF16)

    rows_q = pl.ds(pl.multiple_of(i * tile, tile), tile)
    fq = fcol_scr[rows_q, :]
    q_augs = []
    for hh in range(ATT_HEADS):
        cols = slice(hh * HEAD_DIM, (hh + 1) * HEAD_DIM)
        f_t = _pick_lane(fq, SM_FF + p_idx * ATT_HEADS + hh)
        q_augs.append(jnp.concatenate([(q_ref[:, cols] * (HEAD_DIM ** -0.5)).astype(BF16), bias_lanes(f_t, 0)], axis=1))
    _attention_sweep(i, tile, q_augs, kaug_scr, vb_scr, o_ref)


def _fox(proj, small_bias, batch, seq):
    tile = min(ATT_TILE, seq)
    nq = seq // tile
    hb = HEAD_DIM
    hw = ATT_HEADS * hb
    return pl.pallas_call(
        functools.partial(_fox_kernel, seq=seq, tile=tile),
        grid=(batch, BRANCH_HEADS // ATT_HEADS, nq),
        in_specs=[
            pl.BlockSpec((tile, hw), lambda b, p, i: (b * nq + i, C_FQ // hw + p)),
            pl.BlockSpec((seq, hw), lambda b, p, i: (b, C_FK // hw + p)),
            pl.BlockSpec((seq, hw), lambda b, p, i: (b, C_FV // hw + p)),
            pl.BlockSpec((seq, LANES), lambda b, p, i: (b, C_SMALL // LANES)),
            pl.BlockSpec((1, LANES), lambda b, p, i: (0, 0)),
        ],
        out_specs=pl.BlockSpec((tile, hw), lambda b, p, i: (b * nq + i, p)),
        out_shape=jax.ShapeDtypeStruct((batch * seq, BRANCH_WIDTH), BF16),
        scratch_shapes=[pltpu.VMEM((ATT_HEADS, seq, 2 * hb), BF16), pltpu.VMEM((ATT_HEADS, seq, hb), BF16),
                        pltpu.VMEM((seq, LANES), F32)],
        compiler_params=_cp(("parallel", "parallel", "arbitrary")),
        name="fox",
    )(proj, proj, proj, proj, small_bias)


def _conv_silu_rows(x_ref, w_ref, b_ref, r0, n):
    k = w_ref.shape[0]
    x0 = x_ref[r0:r0 + n, :]
    y = x0 * w_ref[k - 1:k, :] + b_ref[...]
    for sft in range(1, k):
        if r0 >= sft:
            xs = x_ref[r0 - sft:r0 - sft + n, :]
        else:
            row = lax.broadcasted_iota(jnp.int32, x0.shape, 0)
            xs = jnp.where(row >= sft, pltpu.roll(x0, sft, 0), 0.0)
        y = y + xs * w_ref[k - 1 - sft:k - sft, :]
    return y * _sigmoid(y)


MLSTM_UNROLL = 4


def _mlstm_kernel(q_ref, k_ref, v_ref, o_ref, sm_ref, bias_ref, cwq_ref, cwk_ref, cbq_ref, cbk_ref,
                  nw_ref, y_ref, q_scr, k_scr, x_scr, g_scr, irow_scr, grow_scr, *, seq):
    h = pl.program_id(1)
    L = MLSTM_CHUNK
    nc = seq // L
    unroll = MLSTM_UNROLL if nc % MLSTM_UNROLL == 0 else 1
    i_lane = SM_I + h
    f_lane = SM_F + h

    tril = _tril_bf16(L)
    sel_i = _row_selector(i_lane)
    sel_f = _row_selector(f_lane)
    lane_s = lax.broadcasted_iota(jnp.int32, (L, LANES), 1)
    for c in range(nc):
        r0 = c * L
        rows = pl.ds(r0, L)
        q_scr[rows, :] = _conv_silu_rows(q_ref, cwq_ref, cbq_ref, r0, L)
        k_scr[rows, :] = _conv_silu_rows(k_ref, cwk_ref, cbk_ref, r0, L) * (HEAD_DIM ** -0.5)
        smb = sm_ref[rows, :] + bias_ref[...]
        xc = jnp.where(lane_s < SM_F, smb, _log_sigmoid(smb))
        gc = _dot01(tril, xc)
        x_scr[rows, :] = xc
        g_scr[rows, :] = gc
        irow_scr[:, r0:r0 + L] = _dot01(sel_i, xc, NT)
        grow_scr[:, r0:r0 + L] = _dot01(sel_f, gc, NT)

    r_io = lax.broadcasted_iota(jnp.int32, (L, L), 0)
    c_io = lax.broadcasted_iota(jnp.int32, (L, L), 1)
    causal = c_io <= r_io

    def chunk(c, carry):
        C, n, m = carry
        rows = pl.ds(pl.multiple_of(c * L, L), L)
        qc = q_scr[rows, :]
        kc = k_scr[rows, :]
        vb = v_ref[rows, :].astype(BF16)
        qb = qc.astype(BF16)
        g_col = _pick_lane(g_scr[rows, :], f_lane)
        i_col = _pick_lane(x_scr[rows, :], i_lane)
        g_row = grow_scr[0:1, rows]
        i_row = irow_scr[0:1, rows]
        dmat = jnp.where(causal, g_col - g_row + i_row, NEG_INF)
        a = g_col + m
        m_out = jnp.maximum(a, jnp.max(dmat, axis=-1, keepdims=True))
        w = jnp.exp(dmat - m_out) * _dot(qb, kc.astype(BF16), NT)
        inter = jnp.exp(a - m_out)
        num = inter * _dot(qb, C.astype(BF16)) + _dot(w.astype(BF16), vb)
        den = inter * jnp.sum(qc * n, axis=-1, keepdims=True) + jnp.sum(w, axis=-1, keepdims=True)
        hh = num / jnp.maximum(jnp.abs(den), jnp.exp(-m_out))
        g_last = g_row[:, L - 1:L]
        lw = g_last - g_col + i_col
        m_new = jnp.maximum(g_last + m, jnp.max(lw, axis=0, keepdims=True))
        wk = jnp.exp(lw - m_new)
        decay = jnp.exp(g_last + m - m_new)
        kw = kc * wk
        C = decay * C + _dot(kw.astype(BF16), vb, TN)
        n = decay * n + jnp.sum(kw, axis=0, keepdims=True)
        yn = _rms_rows(hh, nw_ref[0])
        y_ref[rows, :] = (yn * _sigmoid(o_ref[rows, :])).astype(y_ref.dtype)
        return C, n, m_new

    def step(it, carry):
        for u in range(unroll):
            carry = chunk(it * unroll + u, carry)
        return carry

    init = (jnp.zeros((HEAD_DIM, HEAD_DIM), F32), jnp.zeros((1, HEAD_DIM), F32), jnp.zeros((1, 1), F32))
    lax.fori_loop(0, nc // unroll, step, init)


def _mlstm(proj, small_bias, conv_w, conv_b, norm_w, batch, seq):
    hb = HEAD_DIM
    nh = BRANCH_HEADS
    blk = lambda c0: pl.BlockSpec((seq, hb), lambda b, h: (b, c0 // hb + h))
    return pl.pallas_call(
        functools.partial(_mlstm_kernel, seq=seq),
        grid=(batch, nh),
        in_specs=[
            blk(C_MQ), blk(C_MK), blk(C_MV), blk(C_MO),
            pl.BlockSpec((seq, LANES), lambda b, h: (b, C_SMALL // LANES)),
            pl.BlockSpec((1, LANES), lambda b, h: (0, 0)),
            pl.BlockSpec((MLSTM_CONV, hb), lambda b, h: (0, h)),
            pl.BlockSpec((MLSTM_CONV, hb), lambda b, h: (0, nh + h)),
            pl.BlockSpec((1, hb), lambda b, h: (0, h)),
            pl.BlockSpec((1, hb), lambda b, h: (0, nh + h)),
            pl.BlockSpec((1, 1, hb), lambda b, h: (h, 0, 0)),
        ],
        out_specs=pl.BlockSpec((seq, hb), lambda b, h: (b, h)),
        out_shape=jax.ShapeDtypeStruct((batch * seq, BRANCH_WIDTH), BF16),
        scratch_shapes=[
            pltpu.VMEM((seq, hb), F32), pltpu.VMEM((seq, hb), F32),
            pltpu.VMEM((seq, LANES), F32), pltpu.VMEM((seq, LANES), F32),
            pltpu.VMEM((8, seq), F32), pltpu.VMEM((8, seq), F32),
        ],
        compiler_params=_cp(("parallel", "parallel")),
        name="mlstm",
    )(proj, proj, proj, proj, proj, small_bias, conv_w, conv_w, conv_b.reshape(1, -1), conv_b.reshape(1, -1),
      norm_w.reshape(nh, 1, hb))


GLA_UNROLL = 4


def _gla_kernel(q_ref, k_ref, v_ref, r_ref, sm_ref, wa_ref, ba_ref, nw_ref, y_ref, la_scr, *, seq):
    L = GLA_CHUNK
    sub = GLA_SUB
    nc = seq // L
    unroll = GLA_UNROLL if nc % GLA_UNROLL == 0 else 1
    dk, dv = GLA_DK, GLA_DV

    for c in range(seq // CHUNK):
        rows = pl.ds(c * CHUNK, CHUNK)
        z = _dot_x3(sm_ref[rows, :], wa_ref[...]) + ba_ref[...]
        la_scr[rows, :] = _log_sigmoid(z) * (1.0 / GLA_TAU)
    tril = _tril_bf16(L)
    r_io = lax.broadcasted_iota(jnp.int32, (sub, L), 0)
    c_io = lax.broadcasted_iota(jnp.int32, (sub, L), 1)
    row_l = lax.broadcasted_iota(jnp.int32, (L, dk), 0)

    def chunk(c, states):
        rows = pl.ds(pl.multiple_of(c * L, L), L)
        bc2 = _dot01(tril, la_scr[rows, :])
        q2 = q_ref[rows, :] * (dk ** -0.5)
        k2 = k_ref[rows, :]
        new_states = []
        for hh in range(2):
            st = states[hh]
            lanes = slice(hh * dk, (hh + 1) * dk)
            b = bc2[:, lanes]
            qh = q2[:, lanes]
            kh = k2[:, lanes]
            vb = v_ref[rows, hh * dv:(hh + 1) * dv].astype(BF16)
            o = _dot((qh * jnp.exp(b)).astype(BF16), st.astype(BF16), NT)
            blocks = []
            for ib in range(L // sub):
                ref_row = b[ib * sub:ib * sub + 1, :]
                qi = qh[ib * sub:(ib + 1) * sub, :] * jnp.exp(b[ib * sub:(ib + 1) * sub, :] - ref_row)
                ki = kh * jnp.exp(jnp.where(row_l < (ib + 1) * sub, ref_row - b, 0.0))
                a = _dot(qi.astype(BF16), ki.astype(BF16), NT)
                blocks.append(jnp.where(c_io <= r_io + ib * sub, a, 0.0))
            attn = jnp.concatenate(blocks, axis=0)
            o = o + _dot(attn.astype(BF16), vb)
            b_last = b[L - 1:L, :]
            ke = kh * jnp.exp(b_last - b)
            st = jnp.exp(b_last) * st + _dot(vb, ke.astype(BF16), TN)
            new_states.append(st)
            yn = _rms_rows(o, nw_ref[0, hh:hh + 1, :])
            r = r_ref[rows, hh * dv:(hh + 1) * dv]
            y_ref[rows, hh * dv:(hh + 1) * dv] = (yn * (r * _sigmoid(r))).astype(y_ref.dtype)
        return tuple(new_states)

    def step(it, states):
        for u in range(unroll):
            states = chunk(it * unroll + u, states)
        return states

    init = (jnp.zeros((dv, dk), F32), jnp.zeros((dv, dk), F32))
    lax.fori_loop(0, nc // unroll, step, init)


def _gla(proj, w_a2_pad, b_a2, norm_w, batch, seq):
    return pl.pallas_call(
        functools.partial(_gla_kernel, seq=seq),
        grid=(batch, GLA_HEADS // 2),
        in_specs=[
            pl.BlockSpec((seq, LANES), lambda b, p: (b, C_GQ // LANES + p)),
            pl.BlockSpec((seq, LANES), lambda b, p: (b, C_GK // LANES + p)),
            pl.BlockSpec((seq, 2 * GLA_DV), lambda b, p: (b, C_GV // (2 * GLA_DV) + p)),
            pl.BlockSpec((seq, 2 * GLA_DV), lambda b, p: (b, C_GR // (2 * GLA_DV) + p)),
            pl.BlockSpec((seq, LANES), lambda b, p: (b, C_SMALL // LANES)),
            pl.BlockSpec((LANES, LANES), lambda b, p: (0, p)),
            pl.BlockSpec((1, LANES), lambda b, p: (0, p)),
            pl.BlockSpec((1, 2, GLA_DV), lambda b, p: (p, 0, 0)),
        ],
        out_specs=pl.BlockSpec((seq, 2 * GLA_DV), lambda b, p: (b, p)),
        out_shape=jax.ShapeDtypeStruct((batch * seq, GLA_HEADS * GLA_DV), BF16),
        scratch_shapes=[pltpu.VMEM((seq, LANES), F32)],
        compiler_params=_cp(("parallel", "parallel")),
        name="gla",
    )(proj, proj, proj, proj, proj, w_a2_pad, b_a2.reshape(1, -1), norm_w.reshape(GLA_HEADS // 2, 2, GLA_DV))


def _merge_kernel(ya_ref, yb_ref, yc_ref, yd_ref, g0_ref, g1_ref, g2_ref, g3_ref, w_ref, o_ref):
    ys = (ya_ref, yb_ref, yc_ref, yd_ref)
    gs = (g0_ref, g1_ref, g2_ref, g3_ref)
    acc = None
    for b in range(N_BRANCH):
        term = _sigmoid(gs[b][...]) * _dot(ys[b][...], w_ref[b])
        acc = term if acc is None else acc + term
    o_ref[...] = acc.astype(o_ref.dtype)


def _merge(ys, proj, w_branch_bf16, layer):
    t = proj.shape[0]
    d = D_MODEL
    tm, tn = 1024, 512
    tm = min(tm, t)
    yspec = pl.BlockSpec((tm, BRANCH_WIDTH), lambda i, j: (i, 0))
    gspec = lambda b: pl.BlockSpec((tm, tn), lambda i, j: (i, (C_GATE + b * d) // tn + j))
    return pl.pallas_call(
        _merge_kernel,
        grid=(t // tm, d // tn),
        in_specs=[yspec] * 4 + [gspec(b) for b in range(N_BRANCH)]
        + [pl.BlockSpec((N_BRANCH, BRANCH_WIDTH, tn), lambda i, j: (layer, 0, j))],
        out_specs=pl.BlockSpec((tm, tn), lambda i, j: (i, j)),
        out_shape=jax.ShapeDtypeStruct((t, d), BF16),
        compiler_params=_cp(("parallel", "parallel")),
        name="merge",
    )(*ys, proj, proj, proj, proj, w_branch_bf16)


def _outproj_kernel(m_ref, w_ref, x_ref, nw_ref, g_ref, o_ref):
    y = _dot(m_ref[...], w_ref[0])
    o_ref[...] = x_ref[...] + g_ref[0] * _rms_rows(y, nw_ref[0])


def _out_proj(merged, w_out_bf16, layer, x2, nw3, mod3, seq, *, nw_idx, g_idx):
    t, d = x2.shape
    tm = min(512, seq)
    per_b = seq // tm
    return pl.pallas_call(
        _outproj_kernel,
        grid=(t // tm,),
        in_specs=[
            pl.BlockSpec((tm, d), lambda i: (i, 0)),
            pl.BlockSpec((1, d, d), lambda i: (layer, 0, 0)),
            pl.BlockSpec((tm, d), lambda i: (i, 0)),
            pl.BlockSpec((1, 1, d), lambda i: (nw_idx, 0, 0)),
            pl.BlockSpec((1, 1, d), lambda i: (i // per_b, 0, g_idx)),
        ],
        out_specs=pl.BlockSpec((tm, d), lambda i: (i, 0)),
        out_shape=jax.ShapeDtypeStruct((t, d), F32),
        compiler_params=_cp(("parallel",)),
        name="out_proj",
    )(merged, w_out_bf16, x2, nw3, mod3)


def _router_kernel(x_ref, nw_ref, sc_ref, sh_ref, rw_ref, rb_ref, h_ref, r_ref):
    h = _norm_mod_rows(x_ref[...], nw_ref[0], sc_ref[0], sh_ref[0])
    h_ref[...] = h
    logits = _dot_x3(h, rw_ref[...]) + rb_ref[...]
    lane = lax.broadcasted_iota(jnp.int32, logits.shape, 1)
    lane_f = lane.astype(F32)
    l = jnp.where(lane < N_EXPERTS, logits, NEG_INF)
    vals, idxs = [], []
    for _ in range(TOP_K):
        m = jnp.max(l, axis=-1, keepdims=True)
        idx = jnp.min(jnp.where(l == m, lane_f, float(LANES)), axis=-1, keepdims=True)
        vals.append(m)
        idxs.append(idx)
        l = jnp.where(lane_f == idx, NEG_INF, l)
    es = [jnp.exp(v - vals[0]) for v in vals]
    tot = es[0]
    for e in es[1:]:
        tot = tot + e
    out = jnp.zeros(logits.shape, F32)
    for r in range(TOP_K):
        out = jnp.where(lane == r, es[r] / tot, out)
        out = jnp.where(lane == TOP_K + r, idxs[r], out)
    r_ref[...] = out


def _router(x2, nw3, mod3, router_w_pad, router_b_pad, seq, *, nw_idx, sc_idx, sh_idx):
    t, d = x2.shape
    tm = min(256, seq)
    per_b = seq // tm
    return pl.pallas_call(
        _router_kernel,
        grid=(t // tm,),
        in_specs=[
            pl.BlockSpec((tm, d), lambda i: (i, 0)),
            pl.BlockSpec((1, 1, d), lambda i: (nw_idx, 0, 0)),
            pl.BlockSpec((1, 1, d), lambda i: (i // per_b, 0, sc_idx)),
            pl.BlockSpec((1, 1, d), lambda i: (i // per_b, 0, sh_idx)),
            pl.BlockSpec((d, LANES), lambda i: (0, 0)),
            pl.BlockSpec((1, LANES), lambda i: (0, 0)),
        ],
        out_specs=[pl.BlockSpec((tm, d), lambda i: (i, 0)), pl.BlockSpec((tm, LANES), lambda i: (i, 0))],
        out_shape=[jax.ShapeDtypeStruct((t, d), F32), jax.ShapeDtypeStruct((t, LANES), F32)],
        compiler_params=_cp(("parallel",)),
        name="router",
    )(x2, nw3, mod3, mod3, router_w_pad, router_b_pad)


def _gather_kernel(nused_ref, idx_hbm, table_hbm, o_hbm, idx_smem, zero_scr, isem, sems, zsem, *, rows, nb):
    b = pl.program_id(0)
    slot = b % 2

    def row_copy(tok, dst_row, s):
        return pltpu.make_async_copy(table_hbm.at[pl.ds(tok, 1), :], o_hbm.at[pl.ds(dst_row, 1), :], sems.at[s])

    def drain(s):
        def body(r, carry):
            row_copy(0, 0, s).wait()
            return carry
        lax.fori_loop(0, rows, body, 0, unroll=8)

    @pl.when(b == 0)
    def _():
        zero_scr[...] = jnp.zeros_like(zero_scr)

    @pl.when(b < nused_ref[0])
    def _():
        icp = pltpu.make_async_copy(idx_hbm.at[b], idx_smem, isem)
        icp.start()
        icp.wait()
        base = b * rows

        def issue(r, carry):
            row_copy(idx_smem[r], base + r, slot).start()
            return carry
        lax.fori_loop(0, rows, issue, 0, unroll=8)

    @pl.when((b > 0) & (b - 1 < nused_ref[0]))
    def _():
        drain(1 - slot)

    @pl.when((b == nb - 1) & (b < nused_ref[0]))
    def _():
        drain(slot)

    @pl.when(b >= nused_ref[0])
    def _():
        zcp = pltpu.make_async_copy(zero_scr, o_hbm.at[pl.ds(b * rows, rows), :], zsem)
        zcp.start()
        zcp.wait()


def _gather_rows(table, idx, n_used):
    rows = MOE_ROW_BLOCK
    n, d = idx.shape[0], table.shape[1]
    nb = n // rows
    grid_spec = pltpu.PrefetchScalarGridSpec(
        num_scalar_prefetch=1,
        grid=(nb,),
        in_specs=[pl.BlockSpec(memory_space=pl.ANY), pl.BlockSpec(memory_space=pl.ANY)],
        out_specs=pl.BlockSpec(memory_space=pl.ANY),
        scratch_shapes=[pltpu.SMEM((rows,), jnp.int32), pltpu.VMEM((rows, d), table.dtype),
                        pltpu.SemaphoreType.DMA, pltpu.SemaphoreType.DMA((2,)), pltpu.SemaphoreType.DMA],
    )
    return pl.pallas_call(
        functools.partial(_gather_kernel, rows=rows, nb=nb),
        grid_spec=grid_spec,
        out_shape=jax.ShapeDtypeStruct((n, d), table.dtype),
        compiler_params=_cp(("arbitrary",)),
        name="gather_rows",
    )(n_used, idx.reshape(nb, rows), table)


def _deinterleave_kernel(w_ref, o_ref):
    n = MXU_DIM
    r = lax.broadcasted_iota(jnp.int32, (n, n), 0)
    c = lax.broadcasted_iota(jnp.int32, (n, n), 1)
    src = jnp.where(c < n // 2, 2 * c, 2 * (c - n // 2) + 1)
    perm = jnp.where(r == src, 1.0, 0.0).astype(BF16)
    for m in range(w_ref.shape[1] // n):
        cols = slice(m * n, (m + 1) * n)
        o_ref[:, cols] = _dot(w_ref[:, cols].astype(BF16), perm).astype(BF16)


def _deinterleave_cast(w2):
    rows, n = w2.shape
    tr = 1024
    return pl.pallas_call(
        _deinterleave_kernel,
        grid=(rows // tr,),
        in_specs=[pl.BlockSpec((tr, n), lambda i: (i, 0))],
        out_specs=pl.BlockSpec((tr, n), lambda i: (i, 0)),
        out_shape=jax.ShapeDtypeStruct((rows, n), BF16),
        compiler_params=_cp(("parallel",)),
        name="deinterleave_cast",
    )(w2)


def _expert_kernel(be_ref, nused_ref, x_ref, wu_ref, bu_ref, wd_ref, bd_ref, o_ref):
    b = pl.program_id(0)

    @pl.when(b < nused_ref[0])
    def _():
        x = x_ref[...].astype(BF16)
        hu = _dot(x, wu_ref[0]) + bu_ref[0]
        half = MXU_DIM // 2
        acts = []
        for m in range(hu.shape[1] // MXU_DIM):
            g = jnp.minimum(hu[:, m * MXU_DIM:m * MXU_DIM + half], SWIGLU_LIMIT)
            lin = jnp.clip(hu[:, m * MXU_DIM + half:(m + 1) * MXU_DIM], -SWIGLU_LIMIT, SWIGLU_LIMIT)
            acts.append((g * _sigmoid(SWIGLU_ALPHA * g) * (lin + 1.0)).astype(BF16))
        act = jnp.concatenate(acts, axis=1)
        o_ref[...] = _dot(act, wd_ref[0]) + bd_ref[0]

    @pl.when(b >= nused_ref[0])
    def _():
        o_ref[...] = jnp.zeros_like(o_ref)


def _experts(x_sorted, block_e, n_used, wu, bu, wd, bd, layer):
    p, d = x_sorted.shape
    rows = MOE_ROW_BLOCK
    nb = p // rows
    de = D_EXPERT

    def xmap(b, be, nu):
        return (jnp.minimum(b, nu[0] - 1), 0)

    def wmap(b, be, nu):
        return (layer * N_EXPERTS + be[b], 0, 0)

    grid_spec = pltpu.PrefetchScalarGridSpec(
        num_scalar_prefetch=2,
        grid=(nb,),
        in_specs=[
            pl.BlockSpec((rows, d), xmap),
            pl.BlockSpec((1, d, 2 * de), wmap),
            pl.BlockSpec((1, 1, 2 * de), wmap),
            pl.BlockSpec((1, de, d), wmap),
            pl.BlockSpec((1, 1, d), wmap),
        ],
        out_specs=pl.BlockSpec((rows, d), lambda b, be, nu: (b, 0)),
    )
    return pl.pallas_call(
        _expert_kernel,
        grid_spec=grid_spec,
        out_shape=jax.ShapeDtypeStruct((p, d), F32),
        compiler_params=_cp(("arbitrary",)),
        name="experts",
    )(block_e, n_used, x_sorted, wu, bu, wd, bd)


def _combine_kernel(y0_ref, y1_ref, y2_ref, y3_ref, r_ref, x_ref, nw_ref, g_ref, o_ref):
    gates = r_ref[...]
    y = None
    for k, y_ref in enumerate((y0_ref, y1_ref, y2_ref, y3_ref)):
        term = gates[:, k:k + 1] * y_ref[0]
        y = term if y is None else y + term
    o_ref[...] = x_ref[...] + g_ref[0] * _rms_rows(y, nw_ref[0])


def _combine(y4, route, x2, nw3, mod3, seq, *, nw_idx, g_idx):
    t, d = x2.shape
    tm = min(256, seq)
    per_b = seq // tm
    yspec = lambda k: pl.BlockSpec((1, tm, d), lambda i: (k, i, 0))
    return pl.pallas_call(
        _combine_kernel,
        grid=(t // tm,),
        in_specs=[yspec(k) for k in range(TOP_K)] + [
            pl.BlockSpec((tm, LANES), lambda i: (i, 0)),
            pl.BlockSpec((tm, d), lambda i: (i, 0)),
            pl.BlockSpec((1, 1, d), lambda i: (nw_idx, 0, 0)),
            pl.BlockSpec((1, 1, d), lambda i: (i // per_b, 0, g_idx)),
        ],
        out_specs=pl.BlockSpec((tm, d), lambda i: (i, 0)),
        out_shape=jax.ShapeDtypeStruct((t, d), F32),
        compiler_params=_cp(("parallel",)),
        name="combine",
    )(y4, y4, y4, y4, route, x2, nw3, mod3)


_W_IN_SEGMENTS = (
    (0, C_AQ, 3072),
    (3080, C_MO, 512),
    (3592, C_GQ, 1024),
    (4632, C_GR, 512),
    (5144, C_FQ, 1536),
    (6684, C_GATE, 8192),
)
_W_IN_SMALL = ((3072, SM_I, 8), (4616, SM_GA, GLA_RANK), (6680, SM_FF, BRANCH_HEADS))


def _reorder_kernel(w_ref, o_ref):
    rows = w_ref.shape[1]
    for src, dst, n in _W_IN_SEGMENTS:
        o_ref[0, :, dst:dst + n] = w_ref[0, :, src:src + n].astype(BF16)
    r = lax.broadcasted_iota(jnp.int32, (LANES, LANES), 0)
    c = lax.broadcasted_iota(jnp.int32, (LANES, LANES), 1)
    small = jnp.zeros((rows, LANES), F32)
    for src, lane0, n in _W_IN_SMALL:
        base = src // LANES * LANES
        sel = jnp.where((c >= lane0) & (c < lane0 + n) & (r == c - lane0 + (src - base)), 1.0, 0.0).astype(BF16)
        small = small + _dot(w_ref[0, :, base:base + LANES].astype(BF16), sel)
    o_ref[0, :, C_SMALL:C_SMALL + LANES] = small.astype(BF16)
    o_ref[0, :, C_SMALL + LANES:] = jnp.zeros((rows, N_PROJ - C_SMALL - LANES), BF16)


def _reorder_w_in(w_in):
    depth, d, n = w_in.shape
    tr = 256
    return pl.pallas_call(
        _reorder_kernel,
        grid=(depth, d // tr),
        in_specs=[pl.BlockSpec((1, tr, n), lambda l, i: (l, i, 0))],
        out_specs=pl.BlockSpec((1, tr, N_PROJ), lambda l, i: (l, i, 0)),
        out_shape=jax.ShapeDtypeStruct((depth, d, N_PROJ), BF16),
        compiler_params=_cp(("parallel", "parallel")),
        name="reorder_w_in",
    )(w_in)


def _routing_tables(route, t):
    tk = t * TOP_K
    rb = MOE_ROW_BLOCK
    flat_e = route[:, TOP_K:2 * TOP_K].astype(jnp.int32).reshape(-1)
    flat_ids = jnp.arange(tk, dtype=jnp.int32)
    e_sorted, order = lax.sort((flat_e, flat_ids), num_keys=1, is_stable=True)
    experts = jnp.arange(N_EXPERTS, dtype=jnp.int32)
    counts = jnp.sum((flat_e[:, None] == experts[None, :]).astype(jnp.int32), axis=0)
    padded = (counts + rb - 1) // rb * rb
    pad_end = jnp.cumsum(padded)
    pad_start = pad_end - padded
    start = jnp.cumsum(counts) - counts
    slot_sorted = pad_start[e_sorted] + flat_ids - start[e_sorted]
    _, inv_slot = lax.sort((order, slot_sorted), num_keys=1)
    n_blocks = -(-tk // rb) + N_EXPERTS
    blk_first = jnp.arange(n_blocks, dtype=jnp.int32) * rb
    block_e = jnp.minimum(jnp.sum((pad_end[None, :] <= blk_first[:, None]).astype(jnp.int32), axis=1),
                          N_EXPERTS - 1).astype(jnp.int32)
    pos = jnp.arange(n_blocks * rb, dtype=jnp.int32)
    e_pos = jnp.repeat(block_e, rb)
    r_pos = pos - pad_start[e_pos]
    src = jnp.clip(start[e_pos] + r_pos, 0, tk - 1)
    slot_tok = jnp.where(r_pos < counts[e_pos], order[src] // TOP_K, 0).astype(jnp.int32)
    n_used = (pad_end[-1] // rb).astype(jnp.int32).reshape(1)
    return slot_tok, inv_slot.astype(jnp.int32), block_e, n_used


def _rope_tables(seq):
    half = HEAD_DIM // 2
    inv = ROPE_THETA ** (-jnp.arange(half, dtype=F32) / half)
    ang = jnp.arange(seq).astype(F32)[:, None] * inv[None, :]
    cos, sin = jnp.cos(ang), jnp.sin(ang)
    return jnp.concatenate([cos, cos], axis=-1), jnp.concatenate([-sin, sin], axis=-1)


def _deinterleave_bias(b_up):
    depth, e, n = b_up.shape
    half = MXU_DIM // 2
    return b_up.reshape(depth, e, n // MXU_DIM, half, 2).transpose(0, 1, 2, 4, 3).reshape(depth, e, 1, n)


def kernel(x, c, ada_w, ada_b, norm_w, w_in, conv_w, conv_b, mlstm_i_b, mlstm_f_b, mlstm_norm_w, gla_w_a2, gla_b_a2, gla_norm_w, fox_f_b, w_branch, w_out, router_w, router_b, w_up, b_up, w_down, b_down):
    batch, seq, d = x.shape
    depth = ada_w.shape[0]
    t = batch * seq
    x2 = x.reshape(t, d)
    mod = _ada_mod(c, ada_w, ada_b)
    cos, sin = _rope_tables(seq)
    w_up_d = _deinterleave_cast(w_up.reshape(depth * N_EXPERTS * d, 2 * D_EXPERT))
    w_up_d = w_up_d.reshape(depth * N_EXPERTS, d, 2 * D_EXPERT)
    b_up_d = _deinterleave_bias(b_up).reshape(depth * N_EXPERTS, 1, 2 * D_EXPERT)
    w_down_b = w_down.astype(BF16).reshape(depth * N_EXPERTS, D_EXPERT, d)
    b_down_r = b_down.reshape(depth * N_EXPERTS, 1, d)
    w_branch_b = w_branch.astype(BF16).reshape(depth * N_BRANCH, BRANCH_WIDTH, d)
    w_out_b = w_out.astype(BF16)
    w_in_r = _reorder_w_in(w_in)

    for l in range(depth):
        mod3 = mod[l].reshape(batch, 1, 6 * d)
        nw3 = norm_w[l].reshape(4, 1, d)

        proj = _in_proj(x2, mod3, nw3, w_in_r, l, seq, nw_idx=0, sc_idx=1, sh_idx=0)
        small_bias = jnp.zeros((1, LANES), F32)
        small_bias = small_bias.at[0, SM_I:SM_I + BRANCH_HEADS].set(mlstm_i_b[l])
        small_bias = small_bias.at[0, SM_F:SM_F + BRANCH_HEADS].set(mlstm_f_b[l])
        small_bias = small_bias.at[0, SM_FF:SM_FF + BRANCH_HEADS].set(fox_f_b[l])
        w_a2_pad = jnp.zeros((LANES, GLA_HEADS * GLA_DK), F32).at[SM_GA:SM_GA + GLA_RANK].set(gla_w_a2[l])

        y_a = _moba(proj, cos, sin, batch, seq)
        y_b = _mlstm(proj, small_bias, conv_w[l], conv_b[l], mlstm_norm_w[l], batch, seq)
        y_c = _gla(proj, w_a2_pad, gla_b_a2[l], gla_norm_w[l], batch, seq)
        y_d = _fox(proj, small_bias, batch, seq)
        merged = _merge((y_a, y_b, y_c, y_d), proj, w_branch_b, l)
        x2 = _out_proj(merged, w_out_b, l, x2, nw3, mod3, seq, nw_idx=1, g_idx=2)

        rw_pad = jnp.zeros((d, LANES), F32).at[:, :N_EXPERTS].set(router_w[l])
        rb_pad = jnp.zeros((1, LANES), F32).at[0, :N_EXPERTS].set(router_b[l])
        h2, route = _router(x2, nw3, mod3, rw_pad, rb_pad, seq, nw_idx=2, sc_idx=4, sh_idx=3)
        slot_tok, inv_slot, block_e, n_used = _routing_tables(route, t)
        x_sorted = _gather_rows(h2, slot_tok, n_used)
        yb = _experts(x_sorted, block_e, n_used, w_up_d, b_up_d, w_down_b, b_down_r, l)
        n_all = jnp.full((1,), (t * TOP_K) // MOE_ROW_BLOCK, jnp.int32)
        inv_kmajor = inv_slot.reshape(t, TOP_K).T.reshape(-1)
        y4 = _gather_rows(yb, inv_kmajor, n_all).reshape(TOP_K, t, d)
        x2 = _combine(y4, route, x2, nw3, mod3, seq, nw_idx=3, g_idx=5)

    return x2.reshape(batch, seq, d)
```

```python
import functools

import jax
import jax.numpy as jnp
from jax import lax
from jax.experimental import pallas as pl
from jax.experimental.pallas import tpu as pltpu

F32 = jnp.float32
BF16 = jnp.bfloat16
NEG_INF = float("-inf")
MASK_PENALTY = -1e9

D_MODEL = 2048
HEAD_DIM = 128
N_BRANCH = 4
BRANCH_HEADS = 4
BRANCH_WIDTH = BRANCH_HEADS * HEAD_DIM

MOBA_BLOCK = 256
MOBA_TOPK = 3
ROPE_THETA = 10000.0

MLSTM_CHUNK = 128
MLSTM_CONV = 4

GLA_HEADS = 4
GLA_DK = 64
GLA_DV = 128
GLA_RANK = 16
GLA_TAU = 16.0
GLA_CHUNK = 64
GLA_SUB = 16

N_EXPERTS = 32
TOP_K = 4
D_EXPERT = D_MODEL // 2
SWIGLU_LIMIT = 7.0
SWIGLU_ALPHA = 1.702
MOE_ROW_BLOCK = 512

RMS_EPS = 1e-6

LANES = 128
MXU_DIM = 256
CHUNK = 128
ATT_TILE = 512

C_AQ, C_AK, C_AV = 0, 512, 1024
C_MQ, C_MK, C_MV, C_MO = 1536, 2048, 2560, 3072
C_GQ, C_GK, C_GV, C_GR = 3584, 3840, 4096, 4608
C_FQ, C_FK, C_FV = 5120, 5632, 6144
C_GATE = 6656
C_SMALL = 14848
SM_I, SM_F, SM_GA, SM_FF = 0, 4, 8, 24
N_PROJ = 15360

NN = (((1,), (0,)), ((), ()))
NT = (((1,), (1,)), ((), ()))
TN = (((0,), (0,)), ((), ()))

VMEM_LIMIT = 56 * 1024 * 1024


def _cp(sem, vmem=VMEM_LIMIT):
    return pltpu.CompilerParams(dimension_semantics=sem, vmem_limit_bytes=vmem)


def _dot(a, b, dn=NN):
    return lax.dot_general(a, b, dn, preferred_element_type=F32)


def _split3(x):
    hi = x.astype(BF16)
    r1 = x - hi.astype(F32)
    mid = r1.astype(BF16)
    lo = (r1 - mid.astype(F32)).astype(BF16)
    return hi, mid, lo


def _dot01(sel, x, dn=NN):
    hi, mid, lo = _split3(x)
    return _dot(sel, hi, dn) + _dot(sel, mid, dn) + _dot(sel, lo, dn)


def _dot_x3(a, b, dn=NN):
    ah = a.astype(BF16)
    al = (a - ah.astype(F32)).astype(BF16)
    bh = b.astype(BF16)
    bl = (b - bh.astype(F32)).astype(BF16)
    return _dot(ah, bh, dn) + _dot(al, bh, dn) + _dot(ah, bl, dn)


def _log_sigmoid(x):
    return jnp.minimum(x, 0.0) - jnp.log1p(jnp.exp(-jnp.abs(x)))


def _sigmoid(x):
    return 0.5 * jnp.tanh(0.5 * x) + 0.5


def _tril_bf16(n):
    r = lax.broadcasted_iota(jnp.int32, (n, n), 0)
    c = lax.broadcasted_iota(jnp.int32, (n, n), 1)
    return jnp.where(c <= r, 1.0, 0.0).astype(BF16)


def _row_selector(lane_idx):
    r = lax.broadcasted_iota(jnp.int32, (8, LANES), 0)
    c = lax.broadcasted_iota(jnp.int32, (8, LANES), 1)
    return jnp.where((r == 0) & (c == lane_idx), 1.0, 0.0).astype(BF16)


def _pick_lane(x, lane_idx):
    lane = lax.broadcasted_iota(jnp.int32, x.shape, 1)
    return jnp.sum(jnp.where(lane == lane_idx, x, 0.0), axis=-1, keepdims=True)


def _ada_kernel(c_ref, w_ref, b_ref, o_ref):
    c = c_ref[...]
    cond = c * _sigmoid(c)
    o_ref[0] = _dot(cond.astype(BF16), w_ref[0].astype(BF16)) + b_ref[0]


def _ada_mod(c, ada_w, ada_b):
    depth, d, n = ada_w.shape
    b = c.shape[0]
    tn = 1024
    return pl.pallas_call(
        _ada_kernel,
        grid=(depth, n // tn),
        in_specs=[
            pl.BlockSpec((b, d), lambda l, j: (0, 0)),
            pl.BlockSpec((1, d, tn), lambda l, j: (l, 0, j)),
            pl.BlockSpec((1, 1, tn), lambda l, j: (l, 0, j)),
        ],
        out_specs=pl.BlockSpec((1, b, tn), lambda l, j: (l, 0, j)),
        out_shape=jax.ShapeDtypeStruct((depth, b, n), F32),
        compiler_params=_cp(("parallel", "parallel")),
        name="ada_mod",
    )(c, ada_w, ada_b.reshape(depth, 1, n))


def _norm_mod_rows(x, nw, sc, sh):
    var = jnp.mean(x * x, axis=-1, keepdims=True)
    y = x * lax.rsqrt(var + RMS_EPS) * nw
    return y * (1.0 + sc) + sh


def _rms_rows(y, nw):
    var = jnp.mean(y * y, axis=-1, keepdims=True)
    return y * lax.rsqrt(var + RMS_EPS) * nw


def _inproj_kernel(x_ref, nw_ref, sc_ref, sh_ref, w_ref, o_ref, h_scr, *, rows):
    @pl.when(pl.program_id(1) == 0)
    def _():
        def body(r, carry):
            sl = pl.ds(pl.multiple_of(r * rows, rows), rows)
            h = _norm_mod_rows(x_ref[sl, :], nw_ref[0], sc_ref[0], sh_ref[0])
            h_scr[sl, :] = h.astype(BF16)
            return carry
        lax.fori_loop(0, x_ref.shape[0] // rows, body, 0)

    o_ref[...] = _dot(h_scr[...], w_ref[0])


def _in_proj(x2, mod3, nw3, w_re, layer, seq, *, nw_idx, sc_idx, sh_idx):
    t, d = x2.shape
    n = w_re.shape[2]
    tm = min(1024, seq)
    tn = 1024
    per_b = seq // tm
    return pl.pallas_call(
        functools.partial(_inproj_kernel, rows=min(256, tm)),
        grid=(t // tm, n // tn),
        in_specs=[
            pl.BlockSpec((tm, d), lambda i, j: (i, 0)),
            pl.BlockSpec((1, 1, d), lambda i, j: (nw_idx, 0, 0)),
            pl.BlockSpec((1, 1, d), lambda i, j: (i // per_b, 0, sc_idx)),
            pl.BlockSpec((1, 1, d), lambda i, j: (i // per_b, 0, sh_idx)),
            pl.BlockSpec((1, d, tn), lambda i, j: (layer, 0, j)),
        ],
        out_specs=pl.BlockSpec((tm, tn), lambda i, j: (i, j)),
        out_shape=jax.ShapeDtypeStruct((t, n), F32),
        scratch_shapes=[pltpu.VMEM((tm, d), BF16)],
        compiler_params=_cp(("parallel", "arbitrary")),
        name="in_proj",
    )(x2, nw3, mod3, mod3, w_re)


ATT_HEADS = 2


def _attention_sweep(i, tile, q_augs, kaug_scr, vb_scr, o_ref):
    nh = len(q_augs)
    rows_q = pl.ds(pl.multiple_of(i * tile, tile), tile)
    r_io = lax.broadcasted_iota(jnp.int32, (tile, tile), 0)
    c_io = lax.broadcasted_iota(jnp.int32, (tile, tile), 1)
    init = []
    for hh in range(nh):
        s = _dot(q_augs[hh], kaug_scr[hh, rows_q, :], NT)
        s = jnp.where(c_io <= r_io, s, NEG_INF)
        m0 = jnp.max(s, axis=-1, keepdims=True)
        p = jnp.exp(s - m0)
        init += [m0, jnp.sum(p, axis=-1, keepdims=True), _dot(p.astype(BF16), vb_scr[hh, rows_q, :])]

    def body(j, carry):
        rows = pl.ds(pl.multiple_of(j * tile, tile), tile)
        out = []
        for hh in range(nh):
            m, l, acc = carry[3 * hh:3 * hh + 3]
            s = _dot(q_augs[hh], kaug_scr[hh, rows, :], NT)
            m_new = jnp.maximum(m, jnp.max(s, axis=-1, keepdims=True))
            alpha = jnp.exp(m - m_new)
            p = jnp.exp(s - m_new)
            l = alpha * l + jnp.sum(p, axis=-1, keepdims=True)
            acc = alpha * acc + _dot(p.astype(BF16), vb_scr[hh, rows, :])
            out += [m_new, l, acc]
        return tuple(out)

    fin = lax.fori_loop(0, i, body, tuple(init))
    for hh in range(nh):
        o_ref[:, hh * HEAD_DIM:(hh + 1) * HEAD_DIM] = (fin[3 * hh + 2] / fin[3 * hh + 1]).astype(o_ref.dtype)


def _rope(x, cos, sin):
    return x * cos + pltpu.roll(x, HEAD_DIM // 2, 1) * sin


def _moba_kernel(q_ref, k_ref, v_ref, cos_ref, sin_ref, o_ref, kaug_scr, vb_scr, km_scr, *, seq, tile, n_sel):
    blk = MOBA_BLOCK
    nblk = seq // blk
    nb8 = -(-nblk // 8) * 8
    i = pl.program_id(2)

    @pl.when(i == 0)
    def _():
        km_scr[...] = jnp.zeros_like(km_scr)
        lane_b = lax.broadcasted_iota(jnp.int32, (blk, LANES), 1)
        for hh in range(ATT_HEADS):
            cols = slice(hh * HEAD_DIM, (hh + 1) * HEAD_DIM)
            for j in range(nblk):
                rows = pl.ds(j * blk, blk)
                kk = _rope(k_ref[rows, cols], cos_ref[rows, :], sin_ref[rows, :])
                kaug_scr[hh, rows, 0:HEAD_DIM] = kk.astype(BF16)
                kaug_scr[hh, rows, HEAD_DIM:2 * HEAD_DIM] = jnp.where(lane_b == j, 1.0, 0.0).astype(BF16)
                km_scr[hh, j:j + 1, :] = jnp.mean(kk, axis=0, keepdims=True)
                vb_scr[hh, rows, :] = v_ref[rows, cols].astype(BF16)

    rows_q = pl.ds(pl.multiple_of(i * tile, tile), tile)
    blk_id = lax.broadcasted_iota(jnp.int32, (nb8, tile), 0)
    blk_f = blk_id.astype(F32)
    q_blk = i * (tile // blk) + lax.broadcasted_iota(jnp.int32, (nb8, tile), 1) // blk
    q_augs = []
    for hh in range(ATT_HEADS):
        cols = slice(hh * HEAD_DIM, (hh + 1) * HEAD_DIM)
        q = _rope(q_ref[:, cols], cos_ref[rows_q, :], sin_ref[rows_q, :])
        g = _dot_x3(km_scr[hh], q, NT)[0:nb8, :]
        g = jnp.where(blk_id < q_blk, g, NEG_INF)
        keep = blk_id == q_blk
        for _ in range(n_sel):
            m = jnp.max(g, axis=0, keepdims=True)
            idx = jnp.min(jnp.where(g == m, blk_f, float(LANES)), axis=0, keepdims=True)
            hit = blk_f == idx
            keep = keep | (hit & (m > NEG_INF))
            g = jnp.where(hit, NEG_INF, g)
        pen_t = jnp.where(keep, 0.0, MASK_PENALTY)
        pen_t = jnp.concatenate([pen_t, jnp.full((LANES - nb8, tile), MASK_PENALTY, F32)], axis=0)
        penalty = pen_t.T
        q_augs.append(jnp.concatenate([(q * (HEAD_DIM ** -0.5)).astype(BF16), penalty.astype(BF16)], axis=1))
    _attention_sweep(i, tile, q_augs, kaug_scr, vb_scr, o_ref)


def _moba(proj, cos, sin, batch, seq):
    tile = min(ATT_TILE, seq)
    nq = seq // tile
    nblk = seq // MOBA_BLOCK
    n_sel = max(1, min(MOBA_TOPK, nblk - 1))
    hb = HEAD_DIM
    hw = ATT_HEADS * hb
    return pl.pallas_call(
        functools.partial(_moba_kernel, seq=seq, tile=tile, n_sel=n_sel),
        grid=(batch, BRANCH_HEADS // ATT_HEADS, nq),
        in_specs=[
            pl.BlockSpec((tile, hw), lambda b, p, i: (b * nq + i, C_AQ // hw + p)),
            pl.BlockSpec((seq, hw), lambda b, p, i: (b, C_AK // hw + p)),
            pl.BlockSpec((seq, hw), lambda b, p, i: (b, C_AV // hw + p)),
            pl.BlockSpec((seq, hb), lambda b, p, i: (0, 0)),
            pl.BlockSpec((seq, hb), lambda b, p, i: (0, 0)),
        ],
        out_specs=pl.BlockSpec((tile, hw), lambda b, p, i: (b * nq + i, p)),
        out_shape=jax.ShapeDtypeStruct((batch * seq, BRANCH_WIDTH), BF16),
        scratch_shapes=[pltpu.VMEM((ATT_HEADS, seq, 2 * hb), BF16), pltpu.VMEM((ATT_HEADS, seq, hb), BF16),
                        pltpu.VMEM((ATT_HEADS, LANES, hb), F32)],
        compiler_params=_cp(("parallel", "parallel", "arbitrary")),
        name="moba",
    )(proj, proj, proj, cos, sin)


def _fox_kernel(q_ref, k_ref, v_ref, sm_ref, bias_ref, o_ref, kaug_scr, vb_scr, fcol_scr, *, seq, tile):
    p_idx = pl.program_id(1)
    i = pl.program_id(2)

    def bias_lanes(f, first):
        hi, mid, lo = _split3(f)
        lane = lax.broadcasted_iota(jnp.int32, (f.shape[0], LANES), 1)
        ones_first = 3 - first
        out = jnp.where((lane >= ones_first) & (lane < ones_first + 3), 1.0, 0.0)
        out = jnp.where(lane == first, hi.astype(F32), out)
        out = jnp.where(lane == first + 1, mid.astype(F32), out)
        out = jnp.where(lane == first + 2, lo.astype(F32), out)
        return out.astype(BF16)

    @pl.when(i == 0)
    def _():
        tril = _tril_bf16(CHUNK)
        carry = jnp.zeros((1, LANES), F32)
        for c in range(seq // CHUNK):
            rows = pl.ds(c * CHUNK, CHUNK)
            lf = _log_sigmoid(sm_ref[rows, :] + bias_ref[...])
            cs = _dot01(tril, lf) + carry
            fcol_scr[rows, :] = cs
            carry = cs[CHUNK - 1:CHUNK, :]
        for hh in range(ATT_HEADS):
            cols = slice(hh * HEAD_DIM, (hh + 1) * HEAD_DIM)
            f_lane = SM_FF + p_idx * ATT_HEADS + hh
            for c in range(seq // tile):
                rows = pl.ds(c * tile, tile)
                kaug_scr[hh, rows, 0:HEAD_DIM] = k_ref[rows, cols].astype(BF16)
                kaug_scr[hh, rows, HEAD_DIM:2 * HEAD_DIM] = bias_lanes(-_pick_lane(fcol_scr[rows, :], f_lane), 3)
                vb_scr[hh, rows, :] = v_ref[rows, cols].astype(BF16)

    rows_q = pl.ds(pl.multiple_of(i * tile, tile), tile)
    fq = fcol_scr[rows_q, :]
    q_augs = []
    for hh in range(ATT_HEADS):
        cols = slice(hh * HEAD_DIM, (hh + 1) * HEAD_DIM)
        f_t = _pick_lane(fq, SM_FF + p_idx * ATT_HEADS + hh)
        q_augs.append(jnp.concatenate([(q_ref[:, cols] * (HEAD_DIM ** -0.5)).astype(BF16), bias_lanes(f_t, 0)], axis=1))
    _attention_sweep(i, tile, q_augs, kaug_scr, vb_scr, o_ref)


def _fox(proj, small_bias, batch, seq):
    tile = min(ATT_TILE, seq)
    nq = seq // tile
    hb = HEAD_DIM
    hw = ATT_HEADS * hb
    return pl.pallas_call(
        functools.partial(_fox_kernel, seq=seq, tile=tile),
        grid=(batch, BRANCH_HEADS // ATT_HEADS, nq),
        in_specs=[
            pl.BlockSpec((tile, hw), lambda b, p, i: (b * nq + i, C_FQ // hw + p)),
            pl.BlockSpec((seq, hw), lambda b, p, i: (b, C_FK // hw + p)),
            pl.BlockSpec((seq, hw), lambda b, p, i: (b, C_FV // hw + p)),
            pl.BlockSpec((seq, LANES), lambda b, p, i: (b, C_SMALL // LANES)),
            pl.BlockSpec((1, LANES), lambda b, p, i: (0, 0)),
        ],
        out_specs=pl.BlockSpec((tile, hw), lambda b, p, i: (b * nq + i, p)),
        out_shape=jax.ShapeDtypeStruct((batch * seq, BRANCH_WIDTH), BF16),
        scratch_shapes=[pltpu.VMEM((ATT_HEADS, seq, 2 * hb), BF16), pltpu.VMEM((ATT_HEADS, seq, hb), BF16),
                        pltpu.VMEM((seq, LANES), F32)],
        compiler_params=_cp(("parallel", "parallel", "arbitrary")),
        name="fox",
    )(proj, proj, proj, proj, small_bias)


def _conv_silu_rows(x_ref, w_ref, b_ref, r0, n):
    k = w_ref.shape[0]
    x0 = x_ref[r0:r0 + n, :]
    y = x0 * w_ref[k - 1:k, :] + b_ref[...]
    for sft in range(1, k):
        if r0 >= sft:
            xs = x_ref[r0 - sft:r0 - sft + n, :]
        else:
            row = lax.broadcasted_iota(jnp.int32, x0.shape, 0)
            xs = jnp.where(row >= sft, pltpu.roll(x0, sft, 0), 0.0)
        y = y + xs * w_ref[k - 1 - sft:k - sft, :]
    return y * _sigmoid(y)


MLSTM_UNROLL = 4


def _mlstm_kernel(q_ref, k_ref, v_ref, o_ref, sm_ref, bias_ref, cwq_ref, cwk_ref, cbq_ref, cbk_ref,
                  nw_ref, y_ref, q_scr, k_scr, x_scr, g_scr, irow_scr, grow_scr, *, seq):
    h = pl.program_id(1)
    L = MLSTM_CHUNK
    nc = seq // L
    unroll = MLSTM_UNROLL if nc % MLSTM_UNROLL == 0 else 1
    i_lane = SM_I + h
    f_lane = SM_F + h

    tril = _tril_bf16(L)
    sel_i = _row_selector(i_lane)
    sel_f = _row_selector(f_lane)
    lane_s = lax.broadcasted_iota(jnp.int32, (L, LANES), 1)
    for c in range(nc):
        r0 = c * L
        rows = pl.ds(r0, L)
        q_scr[rows, :] = _conv_silu_rows(q_ref, cwq_ref, cbq_ref, r0, L)
        k_scr[rows, :] = _conv_silu_rows(k_ref, cwk_ref, cbk_ref, r0, L) * (HEAD_DIM ** -0.5)
        smb = sm_ref[rows, :] + bias_ref[...]
        xc = jnp.where(lane_s < SM_F, smb, _log_sigmoid(smb))
        gc = _dot01(tril, xc)
        x_scr[rows, :] = xc
        g_scr[rows, :] = gc
        irow_scr[:, r0:r0 + L] = _dot01(sel_i, xc, NT)
        grow_scr[:, r0:r0 + L] = _dot01(sel_f, gc, NT)

    r_io = lax.broadcasted_iota(jnp.int32, (L, L), 0)
    c_io = lax.broadcasted_iota(jnp.int32, (L, L), 1)
    causal = c_io <= r_io

    def chunk(c, carry):
        C, n, m = carry
        rows = pl.ds(pl.multiple_of(c * L, L), L)
        qc = q_scr[rows, :]
        kc = k_scr[rows, :]
        vb = v_ref[rows, :].astype(BF16)
        qb = qc.astype(BF16)
        g_col = _pick_lane(g_scr[rows, :], f_lane)
        i_col = _pick_lane(x_scr[rows, :], i_lane)
        g_row = grow_scr[0:1, rows]
        i_row = irow_scr[0:1, rows]
        dmat = jnp.where(causal, g_col - g_row + i_row, NEG_INF)
        a = g_col + m
        m_out = jnp.maximum(a, jnp.max(dmat, axis=-1, keepdims=True))
        w = jnp.exp(dmat - m_out) * _dot(qb, kc.astype(BF16), NT)
        inter = jnp.exp(a - m_out)
        num = inter * _dot(qb, C.astype(BF16)) + _dot(w.astype(BF16), vb)
        den = inter * jnp.sum(qc * n, axis=-1, keepdims=True) + jnp.sum(w, axis=-1, keepdims=True)
        hh = num / jnp.maximum(jnp.abs(den), jnp.exp(-m_out))
        g_last = g_row[:, L - 1:L]
        lw = g_last - g_col + i_col
        m_new = jnp.maximum(g_last + m, jnp.max(lw, axis=0, keepdims=True))
        wk = jnp.exp(lw - m_new)
        decay = jnp.exp(g_last + m - m_new)
        kw = kc * wk
        C = decay * C + _dot(kw.astype(BF16), vb, TN)
        n = decay * n + jnp.sum(kw, axis=0, keepdims=True)
        yn = _rms_rows(hh, nw_ref[0])
        y_ref[rows, :] = (yn * _sigmoid(o_ref[rows, :])).astype(y_ref.dtype)
        return C, n, m_new

    def step(it, carry):
        for u in range(unroll):
            carry = chunk(it * unroll + u, carry)
        return carry

    init = (jnp.zeros((HEAD_DIM, HEAD_DIM), F32), jnp.zeros((1, HEAD_DIM), F32), jnp.zeros((1, 1), F32))
    lax.fori_loop(0, nc // unroll, step, init)


def _mlstm(proj, small_bias, conv_w, conv_b, norm_w, batch, seq):
    hb = HEAD_DIM
    nh = BRANCH_HEADS
    blk = lambda c0: pl.BlockSpec((seq, hb), lambda b, h: (b, c0 // hb + h))
    return pl.pallas_call(
        functools.partial(_mlstm_kernel, seq=seq),
        grid=(batch, nh),
        in_specs=[
            blk(C_MQ), blk(C_MK), blk(C_MV), blk(C_MO),
            pl.BlockSpec((seq, LANES), lambda b, h: (b, C_SMALL // LANES)),
            pl.BlockSpec((1, LANES), lambda b, h: (0, 0)),
            pl.BlockSpec((MLSTM_CONV, hb), lambda b, h: (0, h)),
            pl.BlockSpec((MLSTM_CONV, hb), lambda b, h: (0, nh + h)),
            pl.BlockSpec((1, hb), lambda b, h: (0, h)),
            pl.BlockSpec((1, hb), lambda b, h: (0, nh + h)),
            pl.BlockSpec((1, 1, hb), lambda b, h: (h, 0, 0)),
        ],
        out_specs=pl.BlockSpec((seq, hb), lambda b, h: (b, h)),
        out_shape=jax.ShapeDtypeStruct((batch * seq, BRANCH_WIDTH), BF16),
        scratch_shapes=[
            pltpu.VMEM((seq, hb), F32), pltpu.VMEM((seq, hb), F32),
            pltpu.VMEM((seq, LANES), F32), pltpu.VMEM((seq, LANES), F32),
            pltpu.VMEM((8, seq), F32), pltpu.VMEM((8, seq), F32),
        ],
        compiler_params=_cp(("parallel", "parallel")),
        name="mlstm",
    )(proj, proj, proj, proj, proj, small_bias, conv_w, conv_w, conv_b.reshape(1, -1), conv_b.reshape(1, -1),
      norm_w.reshape(nh, 1, hb))


GLA_UNROLL = 4


def _gla_kernel(q_ref, k_ref, v_ref, r_ref, sm_ref, wa_ref, ba_ref, nw_ref, y_ref, la_scr, *, seq):
    L = GLA_CHUNK
    sub = GLA_SUB
    nc = seq // L
    unroll = GLA_UNROLL if nc % GLA_UNROLL == 0 else 1
    dk, dv = GLA_DK, GLA_DV

    for c in range(seq // CHUNK):
        rows = pl.ds(c * CHUNK, CHUNK)
        z = _dot_x3(sm_ref[rows, :], wa_ref[...]) + ba_ref[...]
        la_scr[rows, :] = _log_sigmoid(z) * (1.0 / GLA_TAU)
    tril = _tril_bf16(L)
    r_io = lax.broadcasted_iota(jnp.int32, (sub, L), 0)
    c_io = lax.broadcasted_iota(jnp.int32, (sub, L), 1)
    row_l = lax.broadcasted_iota(jnp.int32, (L, dk), 0)

    def chunk(c, states):
        rows = pl.ds(pl.multiple_of(c * L, L), L)
        bc2 = _dot01(tril, la_scr[rows, :])
        q2 = q_ref[rows, :] * (dk ** -0.5)
        k2 = k_ref[rows, :]
        new_states = []
        for hh in range(2):
            st = states[hh]
            lanes = slice(hh * dk, (hh + 1) * dk)
            b = bc2[:, lanes]
            qh = q2[:, lanes]
            kh = k2[:, lanes]
            vb = v_ref[rows, hh * dv:(hh + 1) * dv].astype(BF16)
            o = _dot((qh * jnp.exp(b)).astype(BF16), st.astype(BF16), NT)
            blocks = []
            for ib in range(L // sub):
                ref_row = b[ib * sub:ib * sub + 1, :]
                qi = qh[ib * sub:(ib + 1) * sub, :] * jnp.exp(b[ib * sub:(ib + 1) * sub, :] - ref_row)
                ki = kh * jnp.exp(jnp.where(row_l < (ib + 1) * sub, ref_row - b, 0.0))
                a = _dot(qi.astype(BF16), ki.astype(BF16), NT)
                blocks.append(jnp.where(c_io <= r_io + ib * sub, a, 0.0))
            attn = jnp.concatenate(blocks, axis=0)
            o = o + _dot(attn.astype(BF16), vb)
            b_last = b[L - 1:L, :]
            ke = kh * jnp.exp(b_last - b)
            st = jnp.exp(b_last) * st + _dot(vb, ke.astype(BF16), TN)
            new_states.append(st)
            yn = _rms_rows(o, nw_ref[0, hh:hh + 1, :])
            r = r_ref[rows, hh * dv:(hh + 1) * dv]
            y_ref[rows, hh * dv:(hh + 1) * dv] = (yn * (r * _sigmoid(r))).astype(y_ref.dtype)
        return tuple(new_states)

    def step(it, states):
        for u in range(unroll):
            states = chunk(it * unroll + u, states)
        return states

    init = (jnp.zeros((dv, dk), F32), jnp.zeros((dv, dk), F32))
    lax.fori_loop(0, nc // unroll, step, init)


def _gla(proj, w_a2_pad, b_a2, norm_w, batch, seq):
    return pl.pallas_call(
        functools.partial(_gla_kernel, seq=seq),
        grid=(batch, GLA_HEADS // 2),
        in_specs=[
            pl.BlockSpec((seq, LANES), lambda b, p: (b, C_GQ // LANES + p)),
            pl.BlockSpec((seq, LANES), lambda b, p: (b, C_GK // LANES + p)),
            pl.BlockSpec((seq, 2 * GLA_DV), lambda b, p: (b, C_GV // (2 * GLA_DV) + p)),
            pl.BlockSpec((seq, 2 * GLA_DV), lambda b, p: (b, C_GR // (2 * GLA_DV) + p)),
            pl.BlockSpec((seq, LANES), lambda b, p: (b, C_SMALL // LANES)),
            pl.BlockSpec((LANES, LANES), lambda b, p: (0, p)),
            pl.BlockSpec((1, LANES), lambda b, p: (0, p)),
            pl.BlockSpec((1, 2, GLA_DV), lambda b, p: (p, 0, 0)),
        ],
        out_specs=pl.BlockSpec((seq, 2 * GLA_DV), lambda b, p: (b, p)),
        out_shape=jax.ShapeDtypeStruct((batch * seq, GLA_HEADS * GLA_DV), BF16),
        scratch_shapes=[pltpu.VMEM((seq, LANES), F32)],
        compiler_params=_cp(("parallel", "parallel")),
        name="gla",
    )(proj, proj, proj, proj, proj, w_a2_pad, b_a2.reshape(1, -1), norm_w.reshape(GLA_HEADS // 2, 2, GLA_DV))


def _merge_kernel(ya_ref, yb_ref, yc_ref, yd_ref, g0_ref, g1_ref, g2_ref, g3_ref, w_ref, o_ref):
    ys = (ya_ref, yb_ref, yc_ref, yd_ref)
    gs = (g0_ref, g1_ref, g2_ref, g3_ref)
    acc = None
    for b in range(N_BRANCH):
        term = _sigmoid(gs[b][...]) * _dot(ys[b][...], w_ref[b])
        acc = term if acc is None else acc + term
    o_ref[...] = acc.astype(o_ref.dtype)


def _merge(ys, proj, w_branch_bf16, layer):
    t = proj.shape[0]
    d = D_MODEL
    tm, tn = 1024, 512
    tm = min(tm, t)
    yspec = pl.BlockSpec((tm, BRANCH_WIDTH), lambda i, j: (i, 0))
    gspec = lambda b: pl.BlockSpec((tm, tn), lambda i, j: (i, (C_GATE + b * d) // tn + j))
    return pl.pallas_call(
        _merge_kernel,
        grid=(t // tm, d // tn),
        in_specs=[yspec] * 4 + [gspec(b) for b in range(N_BRANCH)]
        + [pl.BlockSpec((N_BRANCH, BRANCH_WIDTH, tn), lambda i, j: (layer, 0, j))],
        out_specs=pl.BlockSpec((tm, tn), lambda i, j: (i, j)),
        out_shape=jax.ShapeDtypeStruct((t, d), BF16),
        compiler_params=_cp(("parallel", "parallel")),
        name="merge",
    )(*ys, proj, proj, proj, proj, w_branch_bf16)


def _outproj_kernel(m_ref, w_ref, x_ref, nw_ref, g_ref, o_ref):
    y = _dot(m_ref[...], w_ref[0])
    o_ref[...] = x_ref[...] + g_ref[0] * _rms_rows(y, nw_ref[0])


def _out_proj(merged, w_out_bf16, layer, x2, nw3, mod3, seq, *, nw_idx, g_idx):
    t, d = x2.shape
    tm = min(512, seq)
    per_b = seq // tm
    return pl.pallas_call(
        _outproj_kernel,
        grid=(t // tm,),
        in_specs=[
            pl.BlockSpec((tm, d), lambda i: (i, 0)),
            pl.BlockSpec((1, d, d), lambda i: (layer, 0, 0)),
            pl.BlockSpec((tm, d), lambda i: (i, 0)),
            pl.BlockSpec((1, 1, d), lambda i: (nw_idx, 0, 0)),
            pl.BlockSpec((1, 1, d), lambda i: (i // per_b, 0, g_idx)),
        ],
        out_specs=pl.BlockSpec((tm, d), lambda i: (i, 0)),
        out_shape=jax.ShapeDtypeStruct((t, d), F32),
        compiler_params=_cp(("parallel",)),
        name="out_proj",
    )(merged, w_out_bf16, x2, nw3, mod3)


def _router_kernel(x_ref, nw_ref, sc_ref, sh_ref, rw_ref, rb_ref, h_ref, r_ref, et_ref):
    h = _norm_mod_rows(x_ref[...], nw_ref[0], sc_ref[0], sh_ref[0])
    h_ref[...] = h
    logits = _dot_x3(h, rw_ref[...]) + rb_ref[...]
    lane = lax.broadcasted_iota(jnp.int32, logits.shape, 1)
    lane_f = lane.astype(F32)
    l = jnp.where(lane < N_EXPERTS, logits, NEG_INF)
    vals, idxs = [], []
    for _ in range(TOP_K):
        m = jnp.max(l, axis=-1, keepdims=True)
        idx = jnp.min(jnp.where(l == m, lane_f, float(LANES)), axis=-1, keepdims=True)
        vals.append(m)
        idxs.append(idx)
        l = jnp.where(lane_f == idx, NEG_INF, l)
    es = [jnp.exp(v - vals[0]) for v in vals]
    tot = es[0]
    for e in es[1:]:
        tot = tot + e
    out = jnp.zeros(logits.shape, F32)
    for r in range(TOP_K):
        out = jnp.where(lane == r, es[r] / tot, out)
        out = jnp.where(lane == TOP_K + r, idxs[r], out)
    r_ref[...] = out
    et_ref[...] = out.T[TOP_K:TOP_K + 8, :].astype(jnp.int32)


def _router(x2, nw3, mod3, router_w_pad, router_b_pad, seq, *, nw_idx, sc_idx, sh_idx):
    t, d = x2.shape
    tm = min(256, seq)
    per_b = seq // tm
    return pl.pallas_call(
        _router_kernel,
        grid=(t // tm,),
        in_specs=[
            pl.BlockSpec((tm, d), lambda i: (i, 0)),
            pl.BlockSpec((1, 1, d), lambda i: (nw_idx, 0, 0)),
            pl.BlockSpec((1, 1, d), lambda i: (i // per_b, 0, sc_idx)),
            pl.BlockSpec((1, 1, d), lambda i: (i // per_b, 0, sh_idx)),
            pl.BlockSpec((d, LANES), lambda i: (0, 0)),
            pl.BlockSpec((1, LANES), lambda i: (0, 0)),
        ],
        out_specs=[pl.BlockSpec((tm, d), lambda i: (i, 0)), pl.BlockSpec((tm, LANES), lambda i: (i, 0)),
                   pl.BlockSpec((8, tm), lambda i: (0, i))],
        out_shape=[jax.ShapeDtypeStruct((t, d), F32), jax.ShapeDtypeStruct((t, LANES), F32),
                   jax.ShapeDtypeStruct((8, t), jnp.int32)],
        compiler_params=_cp(("parallel",)),
        name="router",
    )(x2, nw3, mod3, mod3, router_w_pad, router_b_pad)


def _gather_kernel(nused_ref, idx_hbm, table_hbm, o_hbm, idx_smem, buf, isem, rsems, wsems, *, rows, nb):
    b = pl.program_id(0)
    slot = b % 2
    n_used = nused_ref[0]

    def row_copy(tok, r, s):
        return pltpu.make_async_copy(table_hbm.at[pl.ds(tok, 1), :], buf.at[s, pl.ds(r, 1), :], rsems.at[s])

    def writeback(blk, s):
        return pltpu.make_async_copy(buf.at[s], o_hbm.at[pl.ds(blk * rows, rows), :], wsems.at[s])

    def drain_rows(s):
        def body(r, carry):
            row_copy(0, 0, s).wait()
            return carry
        lax.fori_loop(0, rows, body, 0, unroll=8)

    @pl.when(b >= 2)
    def _():
        writeback(0, slot).wait()

    @pl.when(b < n_used)
    def _():
        icp = pltpu.make_async_copy(idx_hbm.at[b], idx_smem, isem)
        icp.start()
        icp.wait()

        def issue(r, carry):
            row_copy(idx_smem[r], r, slot).start()
            return carry
        lax.fori_loop(0, rows, issue, 0, unroll=8)

    @pl.when(b >= n_used)
    def _():
        buf[slot] = jnp.zeros(buf.shape[1:], buf.dtype)

    @pl.when(b >= 1)
    def _():
        @pl.when(b - 1 < n_used)
        def _():
            drain_rows(1 - slot)
        writeback(b - 1, 1 - slot).start()

    @pl.when(b == nb - 1)
    def _():
        @pl.when(b < n_used)
        def _():
            drain_rows(slot)
        writeback(b, slot).start()
        writeback(b, slot).wait()

        @pl.when(b >= 1)
        def _():
            writeback(0, 1 - slot).wait()


def _gather_rows(table, idx, n_used):
    rows = MOE_ROW_BLOCK
    n, d = idx.shape[0], table.shape[1]
    nb = n // rows
    grid_spec = pltpu.PrefetchScalarGridSpec(
        num_scalar_prefetch=1,
        grid=(nb,),
        in_specs=[pl.BlockSpec(memory_space=pl.ANY), pl.BlockSpec(memory_space=pl.ANY)],
        out_specs=pl.BlockSpec(memory_space=pl.ANY),
        scratch_shapes=[pltpu.SMEM((rows,), jnp.int32), pltpu.VMEM((2, rows, d), table.dtype),
                        pltpu.SemaphoreType.DMA, pltpu.SemaphoreType.DMA((2,)), pltpu.SemaphoreType.DMA((2,))],
    )
    return pl.pallas_call(
        functools.partial(_gather_kernel, rows=rows, nb=nb),
        grid_spec=grid_spec,
        out_shape=jax.ShapeDtypeStruct((n, d), table.dtype),
        compiler_params=_cp(("arbitrary",)),
        name="gather_rows",
    )(n_used, idx.reshape(nb, rows), table)


def _deinterleave_kernel(w_ref, o_ref):
    n = MXU_DIM
    r = lax.broadcasted_iota(jnp.int32, (n, n), 0)
    c = lax.broadcasted_iota(jnp.int32, (n, n), 1)
    src = jnp.where(c < n // 2, 2 * c, 2 * (c - n // 2) + 1)
    perm = jnp.where(r == src, 1.0, 0.0).astype(BF16)
    for m in range(w_ref.shape[1] // n):
        cols = slice(m * n, (m + 1) * n)
        o_ref[:, cols] = _dot(w_ref[:, cols].astype(BF16), perm).astype(BF16)


def _deinterleave_cast(w2):
    rows, n = w2.shape
    tr = 1024
    return pl.pallas_call(
        _deinterleave_kernel,
        grid=(rows // tr,),
        in_specs=[pl.BlockSpec((tr, n), lambda i: (i, 0))],
        out_specs=pl.BlockSpec((tr, n), lambda i: (i, 0)),
        out_shape=jax.ShapeDtypeStruct((rows, n), BF16),
        compiler_params=_cp(("parallel",)),
        name="deinterleave_cast",
    )(w2)


def _expert_kernel(be_ref, nused_ref, x_ref, wu_ref, bu_ref, wd_ref, bd_ref, o_ref):
    b = pl.program_id(0)

    @pl.when(b < nused_ref[0])
    def _():
        x = x_ref[...].astype(BF16)
        hu = _dot(x, wu_ref[0]) + bu_ref[0]
        half = MXU_DIM // 2
        acts = []
        for m in range(hu.shape[1] // MXU_DIM):
            g = jnp.minimum(hu[:, m * MXU_DIM:m * MXU_DIM + half], SWIGLU_LIMIT)
            lin = jnp.clip(hu[:, m * MXU_DIM + half:(m + 1) * MXU_DIM], -SWIGLU_LIMIT, SWIGLU_LIMIT)
            acts.append((g * _sigmoid(SWIGLU_ALPHA * g) * (lin + 1.0)).astype(BF16))
        act = jnp.concatenate(acts, axis=1)
        o_ref[...] = _dot(act, wd_ref[0]) + bd_ref[0]

    @pl.when(b >= nused_ref[0])
    def _():
        o_ref[...] = jnp.zeros_like(o_ref)


def _experts(x_sorted, block_e, n_used, wu, bu, wd, bd, layer):
    p, d = x_sorted.shape
    rows = MOE_ROW_BLOCK
    nb = p // rows
    de = D_EXPERT

    def xmap(b, be, nu):
        return (jnp.minimum(b, nu[0] - 1), 0)

    def wmap(b, be, nu):
        return (layer * N_EXPERTS + be[b], 0, 0)

    grid_spec = pltpu.PrefetchScalarGridSpec(
        num_scalar_prefetch=2,
        grid=(nb,),
        in_specs=[
            pl.BlockSpec((rows, d), xmap),
            pl.BlockSpec((1, d, 2 * de), wmap),
            pl.BlockSpec((1, 1, 2 * de), wmap),
            pl.BlockSpec((1, de, d), wmap),
            pl.BlockSpec((1, 1, d), wmap),
        ],
        out_specs=pl.BlockSpec((rows, d), lambda b, be, nu: (b, 0)),
    )
    return pl.pallas_call(
        _expert_kernel,
        grid_spec=grid_spec,
        out_shape=jax.ShapeDtypeStruct((p, d), F32),
        compiler_params=_cp(("arbitrary",)),
        name="experts",
    )(block_e, n_used, x_sorted, wu, bu, wd, bd)


def _combine_kernel(y0_ref, y1_ref, y2_ref, y3_ref, r_ref, x_ref, nw_ref, g_ref, o_ref):
    gates = r_ref[...]
    y = None
    for k, y_ref in enumerate((y0_ref, y1_ref, y2_ref, y3_ref)):
        term = gates[:, k:k + 1] * y_ref[0]
        y = term if y is None else y + term
    o_ref[...] = x_ref[...] + g_ref[0] * _rms_rows(y, nw_ref[0])


def _combine(y4, route, x2, nw3, mod3, seq, *, nw_idx, g_idx):
    t, d = x2.shape
    tm = min(256, seq)
    per_b = seq // tm
    yspec = lambda k: pl.BlockSpec((1, tm, d), lambda i: (k, i, 0))
    return pl.pallas_call(
        _combine_kernel,
        grid=(t // tm,),
        in_specs=[yspec(k) for k in range(TOP_K)] + [
            pl.BlockSpec((tm, LANES), lambda i: (i, 0)),
            pl.BlockSpec((tm, d), lambda i: (i, 0)),
            pl.BlockSpec((1, 1, d), lambda i: (nw_idx, 0, 0)),
            pl.BlockSpec((1, 1, d), lambda i: (i // per_b, 0, g_idx)),
        ],
        out_specs=pl.BlockSpec((tm, d), lambda i: (i, 0)),
        out_shape=jax.ShapeDtypeStruct((t, d), F32),
        compiler_params=_cp(("parallel",)),
        name="combine",
    )(y4, y4, y4, y4, route, x2, nw3, mod3)


_W_IN_SEGMENTS = (
    (0, C_AQ, 3072),
    (3080, C_MO, 512),
    (3592, C_GQ, 1024),
    (4632, C_GR, 512),
    (5144, C_FQ, 1536),
    (6684, C_GATE, 8192),
)
_W_IN_SMALL = ((3072, SM_I, 8), (4616, SM_GA, GLA_RANK), (6680, SM_FF, BRANCH_HEADS))


def _reorder_kernel(w_ref, o_ref):
    rows = w_ref.shape[1]
    for src, dst, n in _W_IN_SEGMENTS:
        o_ref[0, :, dst:dst + n] = w_ref[0, :, src:src + n].astype(BF16)
    r = lax.broadcasted_iota(jnp.int32, (LANES, LANES), 0)
    c = lax.broadcasted_iota(jnp.int32, (LANES, LANES), 1)
    small = jnp.zeros((rows, LANES), F32)
    for src, lane0, n in _W_IN_SMALL:
        base = src // LANES * LANES
        sel = jnp.where((c >= lane0) & (c < lane0 + n) & (r == c - lane0 + (src - base)), 1.0, 0.0).astype(BF16)
        small = small + _dot(w_ref[0, :, base:base + LANES].astype(BF16), sel)
    o_ref[0, :, C_SMALL:C_SMALL + LANES] = small.astype(BF16)
    o_ref[0, :, C_SMALL + LANES:] = jnp.zeros((rows, N_PROJ - C_SMALL - LANES), BF16)


def _reorder_w_in(w_in):
    depth, d, n = w_in.shape
    tr = 256
    return pl.pallas_call(
        _reorder_kernel,
        grid=(depth, d // tr),
        in_specs=[pl.BlockSpec((1, tr, n), lambda l, i: (l, i, 0))],
        out_specs=pl.BlockSpec((1, tr, N_PROJ), lambda l, i: (l, i, 0)),
        out_shape=jax.ShapeDtypeStruct((depth, d, N_PROJ), BF16),
        compiler_params=_cp(("parallel", "parallel")),
        name="reorder_w_in",
    )(w_in)


def _routing_tables(experts_t, t):
    tk = t * TOP_K
    rb = MOE_ROW_BLOCK
    flat_e = experts_t[:TOP_K].reshape(-1)
    flat_ids = jnp.arange(tk, dtype=jnp.int32)
    e_sorted, order = lax.sort((flat_e, flat_ids), num_keys=1, is_stable=True)
    experts = jnp.arange(N_EXPERTS, dtype=jnp.int32)
    counts = jnp.sum((flat_e[:, None] == experts[None, :]).astype(jnp.int32), axis=0)
    padded = (counts + rb - 1) // rb * rb
    pad_end = jnp.cumsum(padded)
    pad_start = pad_end - padded
    start = jnp.cumsum(counts) - counts
    slot_sorted = pad_start[e_sorted] + flat_ids - start[e_sorted]
    _, inv_slot = lax.sort((order, slot_sorted), num_keys=1)
    n_blocks = -(-tk // rb) + N_EXPERTS
    blk_first = jnp.arange(n_blocks, dtype=jnp.int32) * rb
    block_e = jnp.minimum(jnp.sum((pad_end[None, :] <= blk_first[:, None]).astype(jnp.int32), axis=1),
                          N_EXPERTS - 1).astype(jnp.int32)
    pos = jnp.arange(n_blocks * rb, dtype=jnp.int32)
    e_pos = jnp.repeat(block_e, rb)
    r_pos = pos - pad_start[e_pos]
    src = jnp.clip(start[e_pos] + r_pos, 0, tk - 1)
    slot_tok = jnp.where(r_pos < counts[e_pos], order[src] % t, 0).astype(jnp.int32)
    n_used = (pad_end[-1] // rb).astype(jnp.int32).reshape(1)
    return slot_tok, inv_slot.astype(jnp.int32), block_e, n_used


def _rope_tables(seq):
    half = HEAD_DIM // 2
    inv = ROPE_THETA ** (-jnp.arange(half, dtype=F32) / half)
    ang = jnp.arange(seq).astype(F32)[:, None] * inv[None, :]
    cos, sin = jnp.cos(ang), jnp.sin(ang)
    return jnp.concatenate([cos, cos], axis=-1), jnp.concatenate([-sin, sin], axis=-1)


def _deinterleave_bias(b_up):
    depth, e, n = b_up.shape
    half = MXU_DIM // 2
    return b_up.reshape(depth, e, n // MXU_DIM, half, 2).transpose(0, 1, 2, 4, 3).reshape(depth, e, 1, n)


def kernel(x, c, ada_w, ada_b, norm_w, w_in, conv_w, conv_b, mlstm_i_b, mlstm_f_b, mlstm_norm_w, gla_w_a2, gla_b_a2, gla_norm_w, fox_f_b, w_branch, w_out, router_w, router_b, w_up, b_up, w_down, b_down):
    batch, seq, d = x.shape
    depth = ada_w.shape[0]
    t = batch * seq
    x2 = x.reshape(t, d)
    mod = _ada_mod(c, ada_w, ada_b)
    cos, sin = _rope_tables(seq)
    w_up_d = _deinterleave_cast(w_up.reshape(depth * N_EXPERTS * d, 2 * D_EXPERT))
    w_up_d = w_up_d.reshape(depth * N_EXPERTS, d, 2 * D_EXPERT)
    b_up_d = _deinterleave_bias(b_up).reshape(depth * N_EXPERTS, 1, 2 * D_EXPERT)
    w_down_b = w_down.astype(BF16).reshape(depth * N_EXPERTS, D_EXPERT, d)
    b_down_r = b_down.reshape(depth * N_EXPERTS, 1, d)
    w_branch_b = w_branch.astype(BF16).reshape(depth * N_BRANCH, BRANCH_WIDTH, d)
    w_out_b = w_out.astype(BF16)
    w_in_r = _reorder_w_in(w_in)

    for l in range(depth):
        mod3 = mod[l].reshape(batch, 1, 6 * d)
        nw3 = norm_w[l].reshape(4, 1, d)

        proj = _in_proj(x2, mod3, nw3, w_in_r, l, seq, nw_idx=0, sc_idx=1, sh_idx=0)
        small_bias = jnp.zeros((1, LANES), F32)
        small_bias = small_bias.at[0, SM_I:SM_I + BRANCH_HEADS].set(mlstm_i_b[l])
        small_bias = small_bias.at[0, SM_F:SM_F + BRANCH_HEADS].set(mlstm_f_b[l])
        small_bias = small_bias.at[0, SM_FF:SM_FF + BRANCH_HEADS].set(fox_f_b[l])
        w_a2_pad = jnp.zeros((LANES, GLA_HEADS * GLA_DK), F32).at[SM_GA:SM_GA + GLA_RANK].set(gla_w_a2[l])

        y_a = _moba(proj, cos, sin, batch, seq)
        y_b = _mlstm(proj, small_bias, conv_w[l], conv_b[l], mlstm_norm_w[l], batch, seq)
        y_c = _gla(proj, w_a2_pad, gla_b_a2[l], gla_norm_w[l], batch, seq)
        y_d = _fox(proj, small_bias, batch, seq)
        merged = _merge((y_a, y_b, y_c, y_d), proj, w_branch_b, l)
        x2 = _out_proj(merged, w_out_b, l, x2, nw3, mod3, seq, nw_idx=1, g_idx=2)

        rw_pad = jnp.zeros((d, LANES), F32).at[:, :N_EXPERTS].set(router_w[l])
        rb_pad = jnp.zeros((1, LANES), F32).at[0, :N_EXPERTS].set(router_b[l])
        h2, route, experts_t = _router(x2, nw3, mod3, rw_pad, rb_pad, seq, nw_idx=2, sc_idx=4, sh_idx=3)
        slot_tok, inv_slot, block_e, n_used = _routing_tables(experts_t, t)
        x_sorted = _gather_rows(h2, slot_tok, n_used)
        yb = _experts(x_sorted, block_e, n_used, w_up_d, b_up_d, w_down_b, b_down_r, l)
        n_all = jnp.full((1,), (t * TOP_K) // MOE_ROW_BLOCK, jnp.int32)
        y4 = _gather_rows(yb, inv_slot, n_all).reshape(TOP_K, t, d)
        x2 = _combine(y4, route, x2, nw3, mod3, seq, nw_idx=3, g_idx=5)

    return x2.reshape(batch, seq, d)
```

```python
import functools

import jax
import jax.numpy as jnp
from jax import lax
from jax.experimental import pallas as pl
from jax.experimental.pallas import tpu as pltpu

F32 = jnp.float32
BF16 = jnp.bfloat16
NEG_INF = float("-inf")
MASK_PENALTY = -1e9

D_MODEL = 2048
HEAD_DIM = 128
N_BRANCH = 4
BRANCH_HEADS = 4
BRANCH_WIDTH = BRANCH_HEADS * HEAD_DIM

MOBA_BLOCK = 256
MOBA_TOPK = 3
ROPE_THETA = 10000.0

MLSTM_CHUNK = 128
MLSTM_CONV = 4

GLA_HEADS = 4
GLA_DK = 64
GLA_DV = 128
GLA_RANK = 16
GLA_TAU = 16.0
GLA_CHUNK = 64
GLA_SUB = 16

N_EXPERTS = 32
TOP_K = 4
D_EXPERT = D_MODEL // 2
SWIGLU_LIMIT = 7.0
SWIGLU_ALPHA = 1.702
MOE_ROW_BLOCK = 512

RMS_EPS = 1e-6

LANES = 128
MXU_DIM = 256
CHUNK = 128
ATT_TILE = 512

C_AQ, C_AK, C_AV = 0, 512, 1024
C_MQ, C_MK, C_MV, C_MO = 1536, 2048, 2560, 3072
C_GQ, C_GK, C_GV, C_GR = 3584, 3840, 4096, 4608
C_FQ, C_FK, C_FV = 5120, 5632, 6144
C_GATE = 6656
C_SMALL = 14848
SM_I, SM_F, SM_GA, SM_FF = 0, 4, 8, 24
N_PROJ = 15360

NN = (((1,), (0,)), ((), ()))
NT = (((1,), (1,)), ((), ()))
TN = (((0,), (0,)), ((), ()))

VMEM_LIMIT = 56 * 1024 * 1024


def _cp(sem, vmem=VMEM_LIMIT):
    return pltpu.CompilerParams(dimension_semantics=sem, vmem_limit_bytes=vmem)


def _dot(a, b, dn=NN):
    return lax.dot_general(a, b, dn, preferred_element_type=F32)


def _split3(x):
    hi = x.astype(BF16)
    r1 = x - hi.astype(F32)
    mid = r1.astype(BF16)
    lo = (r1 - mid.astype(F32)).astype(BF16)
    return hi, mid, lo


def _dot01(sel, x, dn=NN):
    hi, mid, lo = _split3(x)
    return _dot(sel, hi, dn) + _dot(sel, mid, dn) + _dot(sel, lo, dn)


def _dot_x3(a, b, dn=NN):
    ah = a.astype(BF16)
    al = (a - ah.astype(F32)).astype(BF16)
    bh = b.astype(BF16)
    bl = (b - bh.astype(F32)).astype(BF16)
    return _dot(ah, bh, dn) + _dot(al, bh, dn) + _dot(ah, bl, dn)


def _log_sigmoid(x):
    return jnp.minimum(x, 0.0) - jnp.log1p(jnp.exp(-jnp.abs(x)))


def _sigmoid(x):
    return 0.5 * jnp.tanh(0.5 * x) + 0.5


def _tril_bf16(n):
    r = lax.broadcasted_iota(jnp.int32, (n, n), 0)
    c = lax.broadcasted_iota(jnp.int32, (n, n), 1)
    return jnp.where(c <= r, 1.0, 0.0).astype(BF16)


def _row_selector(lane_idx):
    r = lax.broadcasted_iota(jnp.int32, (8, LANES), 0)
    c = lax.broadcasted_iota(jnp.int32, (8, LANES), 1)
    return jnp.where((r == 0) & (c == lane_idx), 1.0, 0.0).astype(BF16)


def _pick_lane(x, lane_idx):
    lane = lax.broadcasted_iota(jnp.int32, x.shape, 1)
    return jnp.sum(jnp.where(lane == lane_idx, x, 0.0), axis=-1, keepdims=True)


def _ada_kernel(c_ref, w_ref, b_ref, o_ref):
    c = c_ref[...]
    cond = c * _sigmoid(c)
    o_ref[0] = _dot(cond.astype(BF16), w_ref[0].astype(BF16)) + b_ref[0]


def _ada_mod(c, ada_w, ada_b):
    depth, d, n = ada_w.shape
    b = c.shape[0]
    tn = 1024
    return pl.pallas_call(
        _ada_kernel,
        grid=(depth, n // tn),
        in_specs=[
            pl.BlockSpec((b, d), lambda l, j: (0, 0)),
            pl.BlockSpec((1, d, tn), lambda l, j: (l, 0, j)),
            pl.BlockSpec((1, 1, tn), lambda l, j: (l, 0, j)),
        ],
        out_specs=pl.BlockSpec((1, b, tn), lambda l, j: (l, 0, j)),
        out_shape=jax.ShapeDtypeStruct((depth, b, n), F32),
        compiler_params=_cp(("parallel", "parallel")),
        name="ada_mod",
    )(c, ada_w, ada_b.reshape(depth, 1, n))


def _norm_mod_rows(x, nw, sc, sh):
    var = jnp.mean(x * x, axis=-1, keepdims=True)
    y = x * lax.rsqrt(var + RMS_EPS) * nw
    return y * (1.0 + sc) + sh


def _rms_rows(y, nw):
    var = jnp.mean(y * y, axis=-1, keepdims=True)
    return y * lax.rsqrt(var + RMS_EPS) * nw


def _inproj_kernel(x_ref, nw_ref, sc_ref, sh_ref, w_ref, o_ref, h_scr, *, rows):
    @pl.when(pl.program_id(1) == 0)
    def _():
        def body(r, carry):
            sl = pl.ds(pl.multiple_of(r * rows, rows), rows)
            h = _norm_mod_rows(x_ref[sl, :], nw_ref[0], sc_ref[0], sh_ref[0])
            h_scr[sl, :] = h.astype(BF16)
            return carry
        lax.fori_loop(0, x_ref.shape[0] // rows, body, 0)

    o_ref[...] = _dot(h_scr[...], w_ref[0])


def _in_proj(x2, mod3, nw3, w_re, layer, seq, *, nw_idx, sc_idx, sh_idx):
    t, d = x2.shape
    n = w_re.shape[2]
    tm = min(1024, seq)
    tn = 1024
    per_b = seq // tm
    return pl.pallas_call(
        functools.partial(_inproj_kernel, rows=min(256, tm)),
        grid=(t // tm, n // tn),
        in_specs=[
            pl.BlockSpec((tm, d), lambda i, j: (i, 0)),
            pl.BlockSpec((1, 1, d), lambda i, j: (nw_idx, 0, 0)),
            pl.BlockSpec((1, 1, d), lambda i, j: (i // per_b, 0, sc_idx)),
            pl.BlockSpec((1, 1, d), lambda i, j: (i // per_b, 0, sh_idx)),
            pl.BlockSpec((1, d, tn), lambda i, j: (layer, 0, j)),
        ],
        out_specs=pl.BlockSpec((tm, tn), lambda i, j: (i, j)),
        out_shape=jax.ShapeDtypeStruct((t, n), F32),
        scratch_shapes=[pltpu.VMEM((tm, d), BF16)],
        compiler_params=_cp(("parallel", "arbitrary")),
        name="in_proj",
    )(x2, nw3, mod3, mod3, w_re)


ATT_HEADS = 2


def _attention_sweep(i, tile, q_augs, kaug_scr, vb_scr, o_ref):
    nh = len(q_augs)
    rows_q = pl.ds(pl.multiple_of(i * tile, tile), tile)
    r_io = lax.broadcasted_iota(jnp.int32, (tile, tile), 0)
    c_io = lax.broadcasted_iota(jnp.int32, (tile, tile), 1)
    init = []
    for hh in range(nh):
        s = _dot(q_augs[hh], kaug_scr[hh, rows_q, :], NT)
        s = jnp.where(c_io <= r_io, s, NEG_INF)
        m0 = jnp.max(s, axis=-1, keepdims=True)
        p = jnp.exp(s - m0)
        init += [m0, jnp.sum(p, axis=-1, keepdims=True), _dot(p.astype(BF16), vb_scr[hh, rows_q, :])]

    def body(j, carry):
        rows = pl.ds(pl.multiple_of(j * tile, tile), tile)
        out = []
        for hh in range(nh):
            m, l, acc = carry[3 * hh:3 * hh + 3]
            s = _dot(q_augs[hh], kaug_scr[hh, rows, :], NT)
            m_new = jnp.maximum(m, jnp.max(s, axis=-1, keepdims=True))
            alpha = jnp.exp(m - m_new)
            p = jnp.exp(s - m_new)
            l = alpha * l + jnp.sum(p, axis=-1, keepdims=True)
            acc = alpha * acc + _dot(p.astype(BF16), vb_scr[hh, rows, :])
            out += [m_new, l, acc]
        return tuple(out)

    fin = lax.fori_loop(0, i, body, tuple(init))
    for hh in range(nh):
        o_ref[:, hh * HEAD_DIM:(hh + 1) * HEAD_DIM] = (fin[3 * hh + 2] / fin[3 * hh + 1]).astype(o_ref.dtype)


def _rope(x, cos, sin):
    return x * cos + pltpu.roll(x, HEAD_DIM // 2, 1) * sin


def _moba_kernel(q_ref, k_ref, v_ref, cos_ref, sin_ref, o_ref, kaug_scr, vb_scr, km_scr, *, seq, tile, n_sel):
    blk = MOBA_BLOCK
    nblk = seq // blk
    nb8 = -(-nblk // 8) * 8
    i = pl.program_id(2)

    @pl.when(i == 0)
    def _():
        km_scr[...] = jnp.zeros_like(km_scr)
        lane_b = lax.broadcasted_iota(jnp.int32, (blk, LANES), 1)
        for hh in range(ATT_HEADS):
            cols = slice(hh * HEAD_DIM, (hh + 1) * HEAD_DIM)
            for j in range(nblk):
                rows = pl.ds(j * blk, blk)
                kk = _rope(k_ref[rows, cols], cos_ref[rows, :], sin_ref[rows, :])
                kaug_scr[hh, rows, 0:HEAD_DIM] = kk.astype(BF16)
                kaug_scr[hh, rows, HEAD_DIM:2 * HEAD_DIM] = jnp.where(lane_b == j, 1.0, 0.0).astype(BF16)
                km_scr[hh, j:j + 1, :] = jnp.mean(kk, axis=0, keepdims=True)
                vb_scr[hh, rows, :] = v_ref[rows, cols].astype(BF16)

    rows_q = pl.ds(pl.multiple_of(i * tile, tile), tile)
    blk_id = lax.broadcasted_iota(jnp.int32, (nb8, tile), 0)
    blk_f = blk_id.astype(F32)
    q_blk = i * (tile // blk) + lax.broadcasted_iota(jnp.int32, (nb8, tile), 1) // blk
    q_augs = []
    for hh in range(ATT_HEADS):
        cols = slice(hh * HEAD_DIM, (hh + 1) * HEAD_DIM)
        q = _rope(q_ref[:, cols], cos_ref[rows_q, :], sin_ref[rows_q, :])
        g = _dot_x3(km_scr[hh], q, NT)[0:nb8, :]
        g = jnp.where(blk_id < q_blk, g, NEG_INF)
        keep = blk_id == q_blk
        for _ in range(n_sel):
            m = jnp.max(g, axis=0, keepdims=True)
            idx = jnp.min(jnp.where(g == m, blk_f, float(LANES)), axis=0, keepdims=True)
            hit = blk_f == idx
            keep = keep | (hit & (m > NEG_INF))
            g = jnp.where(hit, NEG_INF, g)
        pen_t = jnp.where(keep, 0.0, MASK_PENALTY)
        pen_t = jnp.concatenate([pen_t, jnp.full((LANES - nb8, tile), MASK_PENALTY, F32)], axis=0)
        penalty = pen_t.T
        q_augs.append(jnp.concatenate([(q * (HEAD_DIM ** -0.5)).astype(BF16), penalty.astype(BF16)], axis=1))
    _attention_sweep(i, tile, q_augs, kaug_scr, vb_scr, o_ref)


def _moba(proj, cos, sin, batch, seq):
    tile = min(ATT_TILE, seq)
    nq = seq // tile
    nblk = seq // MOBA_BLOCK
    n_sel = max(1, min(MOBA_TOPK, nblk - 1))
    hb = HEAD_DIM
    hw = ATT_HEADS * hb
    return pl.pallas_call(
        functools.partial(_moba_kernel, seq=seq, tile=tile, n_sel=n_sel),
        grid=(batch, BRANCH_HEADS // ATT_HEADS, nq),
        in_specs=[
            pl.BlockSpec((tile, hw), lambda b, p, i: (b * nq + i, C_AQ // hw + p)),
            pl.BlockSpec((seq, hw), lambda b, p, i: (b, C_AK // hw + p)),
            pl.BlockSpec((seq, hw), lambda b, p, i: (b, C_AV // hw + p)),
            pl.BlockSpec((seq, hb), lambda b, p, i: (0, 0)),
            pl.BlockSpec((seq, hb), lambda b, p, i: (0, 0)),
        ],
        out_specs=pl.BlockSpec((tile, hw), lambda b, p, i: (b * nq + i, p)),
        out_shape=jax.ShapeDtypeStruct((batch * seq, BRANCH_WIDTH), BF16),
        scratch_shapes=[pltpu.VMEM((ATT_HEADS, seq, 2 * hb), BF16), pltpu.VMEM((ATT_HEADS, seq, hb), BF16),
                        pltpu.VMEM((ATT_HEADS, LANES, hb), F32)],
        compiler_params=_cp(("parallel", "parallel", "arbitrary")),
        name="moba",
    )(proj, proj, proj, cos, sin)


def _fox_kernel(q_ref, k_ref, v_ref, sm_ref, bias_ref, o_ref, kaug_scr, vb_scr, fcol_scr, *, seq, tile):
    p_idx = pl.program_id(1)
    i = pl.program_id(2)

    def bias_lanes(f, first):
        hi, mid, lo = _split3(f)
        lane = lax.broadcasted_iota(jnp.int32, (f.shape[0], LANES), 1)
        ones_first = 3 - first
        out = jnp.where((lane >= ones_first) & (lane < ones_first + 3), 1.0, 0.0)
        out = jnp.where(lane == first, hi.astype(F32), out)
        out = jnp.where(lane == first + 1, mid.astype(F32), out)
        out = jnp.where(lane == first + 2, lo.astype(F32), out)
        return out.astype(BF16)

    @pl.when(i == 0)
    def _():
        tril = _tril_bf16(CHUNK)
        carry = jnp.zeros((1, LANES), F32)
        for c in range(seq // CHUNK):
            rows = pl.ds(c * CHUNK, CHUNK)
            lf = _log_sigmoid(sm_ref[rows, :] + bias_ref[...])
            cs = _dot01(tril, lf) + carry
            fcol_scr[rows, :] = cs
            carry = cs[CHUNK - 1:CHUNK, :]
        for hh in range(ATT_HEADS):
            cols = slice(hh * HEAD_DIM, (hh + 1) * HEAD_DIM)
            f_lane = SM_FF + p_idx * ATT_HEADS + hh
            for c in range(seq // tile):
                rows = pl.ds(c * tile, tile)
                kaug_scr[hh, rows, 0:HEAD_DIM] = k_ref[rows, cols].astype(BF16)
                kaug_scr[hh, rows, HEAD_DIM:2 * HEAD_DIM] = bias_lanes(-_pick_lane(fcol_scr[rows, :], f_lane), 3)
                vb_scr[hh, rows, :] = v_ref[rows, cols].astype(BF16)

    rows_q = pl.ds(pl.multiple_of(i * tile, tile), tile)
    fq = fcol_scr[rows_q, :]
    q_augs = []
    for hh in range(ATT_HEADS):
        cols = slice(hh * HEAD_DIM, (hh + 1) * HEAD_DIM)
        f_t = _pick_lane(fq, SM_FF + p_idx * ATT_HEADS + hh)
        q_augs.append(jnp.concatenate([(q_ref[:, cols] * (HEAD_DIM ** -0.5)).astype(BF16), bias_lanes(f_t, 0)], axis=1))
    _attention_sweep(i, tile, q_augs, kaug_scr, vb_scr, o_ref)


def _fox(proj, small_bias, batch, seq):
    tile = min(ATT_TILE, seq)
    nq = seq // tile
    hb = HEAD_DIM
    hw = ATT_HEADS * hb
    return pl.pallas_call(
        functools.partial(_fox_kernel, seq=seq, tile=tile),
        grid=(batch, BRANCH_HEADS // ATT_HEADS, nq),
        in_specs=[
            pl.BlockSpec((tile, hw), lambda b, p, i: (b * nq + i, C_FQ // hw + p)),
            pl.BlockSpec((seq, hw), lambda b, p, i: (b, C_FK // hw + p)),
            pl.BlockSpec((seq, hw), lambda b, p, i: (b, C_FV // hw + p)),
            pl.BlockSpec((seq, LANES), lambda b, p, i: (b, C_SMALL // LANES)),
            pl.BlockSpec((1, LANES), lambda b, p, i: (0, 0)),
        ],
        out_specs=pl.BlockSpec((tile, hw), lambda b, p, i: (b * nq + i, p)),
        out_shape=jax.ShapeDtypeStruct((batch * seq, BRANCH_WIDTH), BF16),
        scratch_shapes=[pltpu.VMEM((ATT_HEADS, seq, 2 * hb), BF16), pltpu.VMEM((ATT_HEADS, seq, hb), BF16),
                        pltpu.VMEM((seq, LANES), F32)],
        compiler_params=_cp(("parallel", "parallel", "arbitrary")),
        name="fox",
    )(proj, proj, proj, proj, small_bias)


def _conv_silu_rows(x_ref, w_ref, b_ref, r0, n):
    k = w_ref.shape[0]
    x0 = x_ref[r0:r0 + n, :]
    y = x0 * w_ref[k - 1:k, :] + b_ref[...]
    for sft in range(1, k):
        if r0 >= sft:
            xs = x_ref[r0 - sft:r0 - sft + n, :]
        else:
            row = lax.broadcasted_iota(jnp.int32, x0.shape, 0)
            xs = jnp.where(row >= sft, pltpu.roll(x0, sft, 0), 0.0)
        y = y + xs * w_ref[k - 1 - sft:k - sft, :]
    return y * _sigmoid(y)


MLSTM_UNROLL = 16


def _mlstm_kernel(q_ref, k_ref, v_ref, o_ref, sm_ref, bias_ref, cwq_ref, cwk_ref, cbq_ref, cbk_ref,
                  nw_ref, y_ref, q_scr, k_scr, x_scr, g_scr, irow_scr, grow_scr, *, seq):
    h = pl.program_id(1)
    L = MLSTM_CHUNK
    nc = seq // L
    unroll = MLSTM_UNROLL if nc % MLSTM_UNROLL == 0 else (nc if nc < MLSTM_UNROLL else 1)
    i_lane = SM_I + h
    f_lane = SM_F + h

    tril = _tril_bf16(L)
    sel_i = _row_selector(i_lane)
    sel_f = _row_selector(f_lane)
    lane_s = lax.broadcasted_iota(jnp.int32, (L, LANES), 1)
    for c in range(nc):
        r0 = c * L
        rows = pl.ds(r0, L)
        q_scr[rows, :] = _conv_silu_rows(q_ref, cwq_ref, cbq_ref, r0, L)
        k_scr[rows, :] = _conv_silu_rows(k_ref, cwk_ref, cbk_ref, r0, L) * (HEAD_DIM ** -0.5)
        smb = sm_ref[rows, :] + bias_ref[...]
        xc = jnp.where(lane_s < SM_F, smb, _log_sigmoid(smb))
        gc = _dot01(tril, xc)
        x_scr[rows, :] = xc
        g_scr[rows, :] = gc
        irow_scr[:, r0:r0 + L] = _dot01(sel_i, xc, NT)
        grow_scr[:, r0:r0 + L] = _dot01(sel_f, gc, NT)

    r_io = lax.broadcasted_iota(jnp.int32, (L, L), 0)
    c_io = lax.broadcasted_iota(jnp.int32, (L, L), 1)
    causal = c_io <= r_io

    def chunk(c, carry):
        C, n, m = carry
        rows = pl.ds(pl.multiple_of(c * L, L), L)
        qc = q_scr[rows, :]
        kc = k_scr[rows, :]
        vb = v_ref[rows, :].astype(BF16)
        qb = qc.astype(BF16)
        g_col = _pick_lane(g_scr[rows, :], f_lane)
        i_col = _pick_lane(x_scr[rows, :], i_lane)
        g_row = grow_scr[0:1, rows]
        i_row = irow_scr[0:1, rows]
        dmat = jnp.where(causal, g_col - g_row + i_row, NEG_INF)
        a = g_col + m
        m_out = jnp.maximum(a, jnp.max(dmat, axis=-1, keepdims=True))
        w = jnp.exp(dmat - m_out) * _dot(qb, kc.astype(BF16), NT)
        inter = jnp.exp(a - m_out)
        num = inter * _dot(qb, C.astype(BF16)) + _dot(w.astype(BF16), vb)
        den = inter * jnp.sum(qc * n, axis=-1, keepdims=True) + jnp.sum(w, axis=-1, keepdims=True)
        hh = num / jnp.maximum(jnp.abs(den), jnp.exp(-m_out))
        g_last = g_row[:, L - 1:L]
        lw = g_last - g_col + i_col
        m_new = jnp.maximum(g_last + m, jnp.max(lw, axis=0, keepdims=True))
        wk = jnp.exp(lw - m_new)
        decay = jnp.exp(g_last + m - m_new)
        kw = kc * wk
        C = decay * C + _dot(kw.astype(BF16), vb, TN)
        n = decay * n + jnp.sum(kw, axis=0, keepdims=True)
        yn = _rms_rows(hh, nw_ref[0])
        y_ref[rows, :] = (yn * _sigmoid(o_ref[rows, :])).astype(y_ref.dtype)
        return C, n, m_new

    def step(it, carry):
        for u in range(unroll):
            carry = chunk(it * unroll + u, carry)
        return carry

    init = (jnp.zeros((HEAD_DIM, HEAD_DIM), F32), jnp.zeros((1, HEAD_DIM), F32), jnp.zeros((1, 1), F32))
    lax.fori_loop(0, nc // unroll, step, init)


def _mlstm(proj, small_bias, conv_w, conv_b, norm_w, batch, seq):
    hb = HEAD_DIM
    nh = BRANCH_HEADS
    blk = lambda c0: pl.BlockSpec((seq, hb), lambda b, h: (b, c0 // hb + h))
    return pl.pallas_call(
        functools.partial(_mlstm_kernel, seq=seq),
        grid=(batch, nh),
        in_specs=[
            blk(C_MQ), blk(C_MK), blk(C_MV), blk(C_MO),
            pl.BlockSpec((seq, LANES), lambda b, h: (b, C_SMALL // LANES)),
            pl.BlockSpec((1, LANES), lambda b, h: (0, 0)),
            pl.BlockSpec((MLSTM_CONV, hb), lambda b, h: (0, h)),
            pl.BlockSpec((MLSTM_CONV, hb), lambda b, h: (0, nh + h)),
            pl.BlockSpec((1, hb), lambda b, h: (0, h)),
            pl.BlockSpec((1, hb), lambda b, h: (0, nh + h)),
            pl.BlockSpec((1, 1, hb), lambda b, h: (h, 0, 0)),
        ],
        out_specs=pl.BlockSpec((seq, hb), lambda b, h: (b, h)),
        out_shape=jax.ShapeDtypeStruct((batch * seq, BRANCH_WIDTH), BF16),
        scratch_shapes=[
            pltpu.VMEM((seq, hb), F32), pltpu.VMEM((seq, hb), F32),
            pltpu.VMEM((seq, LANES), F32), pltpu.VMEM((seq, LANES), F32),
            pltpu.VMEM((8, seq), F32), pltpu.VMEM((8, seq), F32),
        ],
        compiler_params=_cp(("parallel", "parallel")),
        name="mlstm",
    )(proj, proj, proj, proj, proj, small_bias, conv_w, conv_w, conv_b.reshape(1, -1), conv_b.reshape(1, -1),
      norm_w.reshape(nh, 1, hb))


GLA_UNROLL = 8


def _gla_kernel(q_ref, k_ref, v_ref, r_ref, sm_ref, wa_ref, ba_ref, nw_ref, y_ref, la_scr, *, seq):
    L = GLA_CHUNK
    sub = GLA_SUB
    nc = seq // L
    unroll = GLA_UNROLL if nc % GLA_UNROLL == 0 else (nc if nc < GLA_UNROLL else 1)
    dk, dv = GLA_DK, GLA_DV

    for c in range(seq // CHUNK):
        rows = pl.ds(c * CHUNK, CHUNK)
        z = _dot_x3(sm_ref[rows, :], wa_ref[...]) + ba_ref[...]
        la_scr[rows, :] = _log_sigmoid(z) * (1.0 / GLA_TAU)
    tril = _tril_bf16(L)
    r_io = lax.broadcasted_iota(jnp.int32, (sub, L), 0)
    c_io = lax.broadcasted_iota(jnp.int32, (sub, L), 1)
    row_l = lax.broadcasted_iota(jnp.int32, (L, dk), 0)

    def chunk(c, states):
        rows = pl.ds(pl.multiple_of(c * L, L), L)
        bc2 = _dot01(tril, la_scr[rows, :])
        q2 = q_ref[rows, :] * (dk ** -0.5)
        k2 = k_ref[rows, :]
        new_states = []
        for hh in range(2):
            st = states[hh]
            lanes = slice(hh * dk, (hh + 1) * dk)
            b = bc2[:, lanes]
            qh = q2[:, lanes]
            kh = k2[:, lanes]
            vb = v_ref[rows, hh * dv:(hh + 1) * dv].astype(BF16)
            o = _dot((qh * jnp.exp(b)).astype(BF16), st.astype(BF16), NT)
            blocks = []
            for ib in range(L // sub):
                ref_row = b[ib * sub:ib * sub + 1, :]
                qi = qh[ib * sub:(ib + 1) * sub, :] * jnp.exp(b[ib * sub:(ib + 1) * sub, :] - ref_row)
                ki = kh * jnp.exp(jnp.where(row_l < (ib + 1) * sub, ref_row - b, 0.0))
                a = _dot(qi.astype(BF16), ki.astype(BF16), NT)
                blocks.append(jnp.where(c_io <= r_io + ib * sub, a, 0.0))
            attn = jnp.concatenate(blocks, axis=0)
            o = o + _dot(attn.astype(BF16), vb)
            b_last = b[L - 1:L, :]
            ke = kh * jnp.exp(b_last - b)
            st = jnp.exp(b_last) * st + _dot(vb, ke.astype(BF16), TN)
            new_states.append(st)
            yn = _rms_rows(o, nw_ref[0, hh:hh + 1, :])
            r = r_ref[rows, hh * dv:(hh + 1) * dv]
            y_ref[rows, hh * dv:(hh + 1) * dv] = (yn * (r * _sigmoid(r))).astype(y_ref.dtype)
        return tuple(new_states)

    def step(it, states):
        for u in range(unroll):
            states = chunk(it * unroll + u, states)
        return states

    init = (jnp.zeros((dv, dk), F32), jnp.zeros((dv, dk), F32))
    lax.fori_loop(0, nc // unroll, step, init)


def _gla(proj, w_a2_pad, b_a2, norm_w, batch, seq):
    return pl.pallas_call(
        functools.partial(_gla_kernel, seq=seq),
        grid=(batch, GLA_HEADS // 2),
        in_specs=[
            pl.BlockSpec((seq, LANES), lambda b, p: (b, C_GQ // LANES + p)),
            pl.BlockSpec((seq, LANES), lambda b, p: (b, C_GK // LANES + p)),
            pl.BlockSpec((seq, 2 * GLA_DV), lambda b, p: (b, C_GV // (2 * GLA_DV) + p)),
            pl.BlockSpec((seq, 2 * GLA_DV), lambda b, p: (b, C_GR // (2 * GLA_DV) + p)),
            pl.BlockSpec((seq, LANES), lambda b, p: (b, C_SMALL // LANES)),
            pl.BlockSpec((LANES, LANES), lambda b, p: (0, p)),
            pl.BlockSpec((1, LANES), lambda b, p: (0, p)),
            pl.BlockSpec((1, 2, GLA_DV), lambda b, p: (p, 0, 0)),
        ],
        out_specs=pl.BlockSpec((seq, 2 * GLA_DV), lambda b, p: (b, p)),
        out_shape=jax.ShapeDtypeStruct((batch * seq, GLA_HEADS * GLA_DV), BF16),
        scratch_shapes=[pltpu.VMEM((seq, LANES), F32)],
        compiler_params=_cp(("parallel", "parallel")),
        name="gla",
    )(proj, proj, proj, proj, proj, w_a2_pad, b_a2.reshape(1, -1), norm_w.reshape(GLA_HEADS // 2, 2, GLA_DV))


def _merge_kernel(ya_ref, yb_ref, yc_ref, yd_ref, g0_ref, g1_ref, g2_ref, g3_ref, w_ref, o_ref):
    ys = (ya_ref, yb_ref, yc_ref, yd_ref)
    gs = (g0_ref, g1_ref, g2_ref, g3_ref)
    acc = None
    for b in range(N_BRANCH):
        term = _sigmoid(gs[b][...]) * _dot(ys[b][...], w_ref[b])
        acc = term if acc is None else acc + term
    o_ref[...] = acc.astype(o_ref.dtype)


def _merge(ys, proj, w_branch_bf16, layer):
    t = proj.shape[0]
    d = D_MODEL
    tm, tn = 1024, 512
    tm = min(tm, t)
    yspec = pl.BlockSpec((tm, BRANCH_WIDTH), lambda i, j: (i, 0))
    gspec = lambda b: pl.BlockSpec((tm, tn), lambda i, j: (i, (C_GATE + b * d) // tn + j))
    return pl.pallas_call(
        _merge_kernel,
        grid=(t // tm, d // tn),
        in_specs=[yspec] * 4 + [gspec(b) for b in range(N_BRANCH)]
        + [pl.BlockSpec((N_BRANCH, BRANCH_WIDTH, tn), lambda i, j: (layer, 0, j))],
        out_specs=pl.BlockSpec((tm, tn), lambda i, j: (i, j)),
        out_shape=jax.ShapeDtypeStruct((t, d), BF16),
        compiler_params=_cp(("parallel", "parallel")),
        name="merge",
    )(*ys, proj, proj, proj, proj, w_branch_bf16)


def _outproj_kernel(m_ref, w_ref, x_ref, nw_ref, g_ref, o_ref):
    y = _dot(m_ref[...], w_ref[0])
    o_ref[...] = x_ref[...] + g_ref[0] * _rms_rows(y, nw_ref[0])


def _out_proj(merged, w_out_bf16, layer, x2, nw3, mod3, seq, *, nw_idx, g_idx):
    t, d = x2.shape
    tm = min(512, seq)
    per_b = seq // tm
    return pl.pallas_call(
        _outproj_kernel,
        grid=(t // tm,),
        in_specs=[
            pl.BlockSpec((tm, d), lambda i: (i, 0)),
            pl.BlockSpec((1, d, d), lambda i: (layer, 0, 0)),
            pl.BlockSpec((tm, d), lambda i: (i, 0)),
            pl.BlockSpec((1, 1, d), lambda i: (nw_idx, 0, 0)),
            pl.BlockSpec((1, 1, d), lambda i: (i // per_b, 0, g_idx)),
        ],
        out_specs=pl.BlockSpec((tm, d), lambda i: (i, 0)),
        out_shape=jax.ShapeDtypeStruct((t, d), F32),
        compiler_params=_cp(("parallel",)),
        name="out_proj",
    )(merged, w_out_bf16, x2, nw3, mod3)


def _router_kernel(x_ref, nw_ref, sc_ref, sh_ref, rw_ref, rb_ref, h_ref, r_ref, et_ref):
    h = _norm_mod_rows(x_ref[...], nw_ref[0], sc_ref[0], sh_ref[0])
    h_ref[...] = h
    logits = _dot_x3(h, rw_ref[...]) + rb_ref[...]
    lane = lax.broadcasted_iota(jnp.int32, logits.shape, 1)
    lane_f = lane.astype(F32)
    l = jnp.where(lane < N_EXPERTS, logits, NEG_INF)
    vals, idxs = [], []
    for _ in range(TOP_K):
        m = jnp.max(l, axis=-1, keepdims=True)
        idx = jnp.min(jnp.where(l == m, lane_f, float(LANES)), axis=-1, keepdims=True)
        vals.append(m)
        idxs.append(idx)
        l = jnp.where(lane_f == idx, NEG_INF, l)
    es = [jnp.exp(v - vals[0]) for v in vals]
    tot = es[0]
    for e in es[1:]:
        tot = tot + e
    out = jnp.zeros(logits.shape, F32)
    for r in range(TOP_K):
        out = jnp.where(lane == r, es[r] / tot, out)
        out = jnp.where(lane == TOP_K + r, idxs[r], out)
    r_ref[...] = out
    et_ref[...] = out.T[TOP_K:TOP_K + 8, :].astype(jnp.int32)


def _router(x2, nw3, mod3, router_w_pad, router_b_pad, seq, *, nw_idx, sc_idx, sh_idx):
    t, d = x2.shape
    tm = min(256, seq)
    per_b = seq // tm
    return pl.pallas_call(
        _router_kernel,
        grid=(t // tm,),
        in_specs=[
            pl.BlockSpec((tm, d), lambda i: (i, 0)),
            pl.BlockSpec((1, 1, d), lambda i: (nw_idx, 0, 0)),
            pl.BlockSpec((1, 1, d), lambda i: (i // per_b, 0, sc_idx)),
            pl.BlockSpec((1, 1, d), lambda i: (i // per_b, 0, sh_idx)),
            pl.BlockSpec((d, LANES), lambda i: (0, 0)),
            pl.BlockSpec((1, LANES), lambda i: (0, 0)),
        ],
        out_specs=[pl.BlockSpec((tm, d), lambda i: (i, 0)), pl.BlockSpec((tm, LANES), lambda i: (i, 0)),
                   pl.BlockSpec((8, tm), lambda i: (0, i))],
        out_shape=[jax.ShapeDtypeStruct((t, d), F32), jax.ShapeDtypeStruct((t, LANES), F32),
                   jax.ShapeDtypeStruct((8, t), jnp.int32)],
        compiler_params=_cp(("parallel",)),
        name="router",
    )(x2, nw3, mod3, mod3, router_w_pad, router_b_pad)


def _gather_kernel(nused_ref, idx_hbm, table_hbm, o_ref, idx_smem, isem, sem, *, rows):
    b = pl.program_id(0)

    def row_copy(tok, r):
        return pltpu.make_async_copy(table_hbm.at[pl.ds(tok, 1), :], o_ref.at[pl.ds(r, 1), :], sem)

    @pl.when(b < nused_ref[0])
    def _():
        icp = pltpu.make_async_copy(idx_hbm.at[b], idx_smem, isem)
        icp.start()
        icp.wait()

        def issue(r, carry):
            row_copy(idx_smem[r], r).start()
            return carry
        lax.fori_loop(0, rows, issue, 0, unroll=8)

        def drain(r, carry):
            row_copy(0, r).wait()
            return carry
        lax.fori_loop(0, rows, drain, 0, unroll=8)

    @pl.when(b >= nused_ref[0])
    def _():
        o_ref[...] = jnp.zeros_like(o_ref)


def _gather_rows(table, idx, n_used):
    rows = MOE_ROW_BLOCK
    n, d = idx.shape[0], table.shape[1]
    nb = n // rows
    grid_spec = pltpu.PrefetchScalarGridSpec(
        num_scalar_prefetch=1,
        grid=(nb,),
        in_specs=[pl.BlockSpec(memory_space=pl.ANY), pl.BlockSpec(memory_space=pl.ANY)],
        out_specs=pl.BlockSpec((rows, d), lambda b, nu: (b, 0)),
        scratch_shapes=[pltpu.SMEM((rows,), jnp.int32), pltpu.SemaphoreType.DMA, pltpu.SemaphoreType.DMA],
    )
    return pl.pallas_call(
        functools.partial(_gather_kernel, rows=rows),
        grid_spec=grid_spec,
        out_shape=jax.ShapeDtypeStruct((n, d), table.dtype),
        compiler_params=_cp(("arbitrary",)),
        name="gather_rows",
    )(n_used, idx.reshape(nb, rows), table)


def _deinterleave_kernel(w_ref, o_ref):
    n = MXU_DIM
    r = lax.broadcasted_iota(jnp.int32, (n, n), 0)
    c = lax.broadcasted_iota(jnp.int32, (n, n), 1)
    src = jnp.where(c < n // 2, 2 * c, 2 * (c - n // 2) + 1)
    perm = jnp.where(r == src, 1.0, 0.0).astype(BF16)
    for m in range(w_ref.shape[1] // n):
        cols = slice(m * n, (m + 1) * n)
        o_ref[:, cols] = _dot(w_ref[:, cols].astype(BF16), perm).astype(BF16)


def _deinterleave_cast(w2):
    rows, n = w2.shape
    tr = 1024
    return pl.pallas_call(
        _deinterleave_kernel,
        grid=(rows // tr,),
        in_specs=[pl.BlockSpec((tr, n), lambda i: (i, 0))],
        out_specs=pl.BlockSpec((tr, n), lambda i: (i, 0)),
        out_shape=jax.ShapeDtypeStruct((rows, n), BF16),
        compiler_params=_cp(("parallel",)),
        name="deinterleave_cast",
    )(w2)


def _expert_kernel(be_ref, nused_ref, x_ref, wu_ref, bu_ref, wd_ref, bd_ref, o_ref):
    b = pl.program_id(0)

    @pl.when(b < nused_ref[0])
    def _():
        x = x_ref[...].astype(BF16)
        hu = _dot(x, wu_ref[0]) + bu_ref[0]
        half = MXU_DIM // 2
        acts = []
        for m in range(hu.shape[1] // MXU_DIM):
            g = jnp.minimum(hu[:, m * MXU_DIM:m * MXU_DIM + half], SWIGLU_LIMIT)
            lin = jnp.clip(hu[:, m * MXU_DIM + half:(m + 1) * MXU_DIM], -SWIGLU_LIMIT, SWIGLU_LIMIT)
            acts.append((g * _sigmoid(SWIGLU_ALPHA * g) * (lin + 1.0)).astype(BF16))
        act = jnp.concatenate(acts, axis=1)
        o_ref[...] = _dot(act, wd_ref[0]) + bd_ref[0]

    @pl.when(b >= nused_ref[0])
    def _():
        o_ref[...] = jnp.zeros_like(o_ref)


def _experts(x_sorted, block_e, n_used, wu, bu, wd, bd, layer):
    p, d = x_sorted.shape
    rows = MOE_ROW_BLOCK
    nb = p // rows
    de = D_EXPERT

    def xmap(b, be, nu):
        return (jnp.minimum(b, nu[0] - 1), 0)

    def wmap(b, be, nu):
        return (layer * N_EXPERTS + be[b], 0, 0)

    grid_spec = pltpu.PrefetchScalarGridSpec(
        num_scalar_prefetch=2,
        grid=(nb,),
        in_specs=[
            pl.BlockSpec((rows, d), xmap),
            pl.BlockSpec((1, d, 2 * de), wmap),
            pl.BlockSpec((1, 1, 2 * de), wmap),
            pl.BlockSpec((1, de, d), wmap),
            pl.BlockSpec((1, 1, d), wmap),
        ],
        out_specs=pl.BlockSpec((rows, d), lambda b, be, nu: (b, 0)),
    )
    return pl.pallas_call(
        _expert_kernel,
        grid_spec=grid_spec,
        out_shape=jax.ShapeDtypeStruct((p, d), F32),
        compiler_params=_cp(("arbitrary",)),
        name="experts",
    )(block_e, n_used, x_sorted, wu, bu, wd, bd)


def _combine_kernel(idx_hbm, yb_hbm, r_ref, x_ref, nw_ref, g_ref, o_ref, ybuf, idx_smem, isems, rsems, *, tm, nt):
    i = pl.program_id(0)
    slot = i % 2
    n_rows = TOP_K * tm

    def idx_copy(tile, s):
        return pltpu.make_async_copy(idx_hbm.at[tile], idx_smem.at[s], isems.at[s])

    def row_copy(src_row, r, s):
        return pltpu.make_async_copy(yb_hbm.at[pl.ds(src_row, 1), :], ybuf.at[s, pl.ds(r, 1), :], rsems.at[s])

    def issue_rows(s):
        def body(r, carry):
            row_copy(idx_smem[s, r], r, s).start()
            return carry
        lax.fori_loop(0, n_rows, body, 0, unroll=8)

    @pl.when(i == 0)
    def _():
        idx_copy(0, 0).start()
        idx_copy(0, 0).wait()
        issue_rows(0)

        @pl.when(nt > 1)
        def _():
            idx_copy(1, 1).start()

    @pl.when(i + 1 < nt)
    def _():
        idx_copy(0, 1 - slot).wait()
        issue_rows(1 - slot)

    @pl.when(i + 2 < nt)
    def _():
        idx_copy(i + 2, slot).start()

    def drain(r, carry):
        row_copy(0, 0, slot).wait()
        return carry
    lax.fori_loop(0, n_rows, drain, 0, unroll=8)

    gates = r_ref[...]
    y = None
    for k in range(TOP_K):
        term = gates[:, k:k + 1] * ybuf[slot, k * tm:(k + 1) * tm, :]
        y = term if y is None else y + term
    o_ref[...] = x_ref[...] + g_ref[0] * _rms_rows(y, nw_ref[0])


def _combine(yb, inv_slot, route, x2, nw3, mod3, seq, *, nw_idx, g_idx):
    t, d = x2.shape
    tm = min(256, seq)
    nt = t // tm
    per_b = seq // tm
    idx_tiles = inv_slot.reshape(TOP_K, nt, tm).transpose(1, 0, 2).reshape(nt, TOP_K * tm)
    return pl.pallas_call(
        functools.partial(_combine_kernel, tm=tm, nt=nt),
        grid=(nt,),
        in_specs=[
            pl.BlockSpec(memory_space=pl.ANY),
            pl.BlockSpec(memory_space=pl.ANY),
            pl.BlockSpec((tm, LANES), lambda i: (i, 0)),
            pl.BlockSpec((tm, d), lambda i: (i, 0)),
            pl.BlockSpec((1, 1, d), lambda i: (nw_idx, 0, 0)),
            pl.BlockSpec((1, 1, d), lambda i: (i // per_b, 0, g_idx)),
        ],
        out_specs=pl.BlockSpec((tm, d), lambda i: (i, 0)),
        out_shape=jax.ShapeDtypeStruct((t, d), F32),
        scratch_shapes=[pltpu.VMEM((2, TOP_K * tm, d), F32), pltpu.SMEM((2, TOP_K * tm), jnp.int32),
                        pltpu.SemaphoreType.DMA((2,)), pltpu.SemaphoreType.DMA((2,))],
        compiler_params=_cp(("arbitrary",)),
        name="combine",
    )(idx_tiles, yb, route, x2, nw3, mod3)


_W_IN_SEGMENTS = (
    (0, C_AQ, 3072),
    (3080, C_MO, 512),
    (3592, C_GQ, 1024),
    (4632, C_GR, 512),
    (5144, C_FQ, 1536),
    (6684, C_GATE, 8192),
)
_W_IN_SMALL = ((3072, SM_I, 8), (4616, SM_GA, GLA_RANK), (6680, SM_FF, BRANCH_HEADS))


def _reorder_kernel(w_ref, o_ref):
    rows = w_ref.shape[1]
    for src, dst, n in _W_IN_SEGMENTS:
        o_ref[0, :, dst:dst + n] = w_ref[0, :, src:src + n].astype(BF16)
    r = lax.broadcasted_iota(jnp.int32, (LANES, LANES), 0)
    c = lax.broadcasted_iota(jnp.int32, (LANES, LANES), 1)
    small = jnp.zeros((rows, LANES), F32)
    for src, lane0, n in _W_IN_SMALL:
        base = src // LANES * LANES
        sel = jnp.where((c >= lane0) & (c < lane0 + n) & (r == c - lane0 + (src - base)), 1.0, 0.0).astype(BF16)
        small = small + _dot(w_ref[0, :, base:base + LANES].astype(BF16), sel)
    o_ref[0, :, C_SMALL:C_SMALL + LANES] = small.astype(BF16)
    o_ref[0, :, C_SMALL + LANES:] = jnp.zeros((rows, N_PROJ - C_SMALL - LANES), BF16)


def _reorder_w_in(w_in):
    depth, d, n = w_in.shape
    tr = 256
    return pl.pallas_call(
        _reorder_kernel,
        grid=(depth, d // tr),
        in_specs=[pl.BlockSpec((1, tr, n), lambda l, i: (l, i, 0))],
        out_specs=pl.BlockSpec((1, tr, N_PROJ), lambda l, i: (l, i, 0)),
        out_shape=jax.ShapeDtypeStruct((depth, d, N_PROJ), BF16),
        compiler_params=_cp(("parallel", "parallel")),
        name="reorder_w_in",
    )(w_in)


def _routing_tables(experts_t, t):
    tk = t * TOP_K
    rb = MOE_ROW_BLOCK
    flat_e = experts_t[:TOP_K].reshape(-1)
    flat_ids = jnp.arange(tk, dtype=jnp.int32)
    e_sorted, order = lax.sort((flat_e, flat_ids), num_keys=1, is_stable=True)
    experts = jnp.arange(N_EXPERTS, dtype=jnp.int32)
    counts = jnp.sum((flat_e[:, None] == experts[None, :]).astype(jnp.int32), axis=0)
    padded = (counts + rb - 1) // rb * rb
    pad_end = jnp.cumsum(padded)
    pad_start = pad_end - padded
    start = jnp.cumsum(counts) - counts
    slot_sorted = pad_start[e_sorted] + flat_ids - start[e_sorted]
    _, inv_slot = lax.sort((order, slot_sorted), num_keys=1)
    n_blocks = -(-tk // rb) + N_EXPERTS
    blk_first = jnp.arange(n_blocks, dtype=jnp.int32) * rb
    block_e = jnp.minimum(jnp.sum((pad_end[None, :] <= blk_first[:, None]).astype(jnp.int32), axis=1),
                          N_EXPERTS - 1).astype(jnp.int32)
    pos = jnp.arange(n_blocks * rb, dtype=jnp.int32)
    e_pos = jnp.repeat(block_e, rb)
    r_pos = pos - pad_start[e_pos]
    src = jnp.clip(start[e_pos] + r_pos, 0, tk - 1)
    slot_tok = jnp.where(r_pos < counts[e_pos], order[src] % t, 0).astype(jnp.int32)
    n_used = (pad_end[-1] // rb).astype(jnp.int32).reshape(1)
    return slot_tok, inv_slot.astype(jnp.int32), block_e, n_used


def _rope_tables(seq):
    half = HEAD_DIM // 2
    inv = ROPE_THETA ** (-jnp.arange(half, dtype=F32) / half)
    ang = jnp.arange(seq).astype(F32)[:, None] * inv[None, :]
    cos, sin = jnp.cos(ang), jnp.sin(ang)
    return jnp.concatenate([cos, cos], axis=-1), jnp.concatenate([-sin, sin], axis=-1)


def _deinterleave_bias(b_up):
    depth, e, n = b_up.shape
    half = MXU_DIM // 2
    return b_up.reshape(depth, e, n // MXU_DIM, half, 2).transpose(0, 1, 2, 4, 3).reshape(depth, e, 1, n)


def kernel(x, c, ada_w, ada_b, norm_w, w_in, conv_w, conv_b, mlstm_i_b, mlstm_f_b, mlstm_norm_w, gla_w_a2, gla_b_a2, gla_norm_w, fox_f_b, w_branch, w_out, router_w, router_b, w_up, b_up, w_down, b_down):
    batch, seq, d = x.shape
    depth = ada_w.shape[0]
    t = batch * seq
    x2 = x.reshape(t, d)
    mod = _ada_mod(c, ada_w, ada_b)
    cos, sin = _rope_tables(seq)
    w_up_d = _deinterleave_cast(w_up.reshape(depth * N_EXPERTS * d, 2 * D_EXPERT))
    w_up_d = w_up_d.reshape(depth * N_EXPERTS, d, 2 * D_EXPERT)
    b_up_d = _deinterleave_bias(b_up).reshape(depth * N_EXPERTS, 1, 2 * D_EXPERT)
    w_down_b = w_down.astype(BF16).reshape(depth * N_EXPERTS, D_EXPERT, d)
    b_down_r = b_down.reshape(depth * N_EXPERTS, 1, d)
    w_branch_b = w_branch.astype(BF16).reshape(depth * N_BRANCH, BRANCH_WIDTH, d)
    w_out_b = w_out.astype(BF16)
    w_in_r = _reorder_w_in(w_in)

    for l in range(depth):
        mod3 = mod[l].reshape(batch, 1, 6 * d)
        nw3 = norm_w[l].reshape(4, 1, d)

        proj = _in_proj(x2, mod3, nw3, w_in_r, l, seq, nw_idx=0, sc_idx=1, sh_idx=0)
        small_bias = jnp.zeros((1, LANES), F32)
        small_bias = small_bias.at[0, SM_I:SM_I + BRANCH_HEADS].set(mlstm_i_b[l])
        small_bias = small_bias.at[0, SM_F:SM_F + BRANCH_HEADS].set(mlstm_f_b[l])
        small_bias = small_bias.at[0, SM_FF:SM_FF + BRANCH_HEADS].set(fox_f_b[l])
        w_a2_pad = jnp.zeros((LANES, GLA_HEADS * GLA_DK), F32).at[SM_GA:SM_GA + GLA_RANK].set(gla_w_a2[l])

        y_a = _moba(proj, cos, sin, batch, seq)
        y_b = _mlstm(proj, small_bias, conv_w[l], conv_b[l], mlstm_norm_w[l], batch, seq)
        y_c = _gla(proj, w_a2_pad, gla_b_a2[l], gla_norm_w[l], batch, seq)
        y_d = _fox(proj, small_bias, batch, seq)
        merged = _merge((y_a, y_b, y_c, y_d), proj, w_branch_b, l)
        x2 = _out_proj(merged, w_out_b, l, x2, nw3, mod3, seq, nw_idx=1, g_idx=2)

        rw_pad = jnp.zeros((d, LANES), F32).at[:, :N_EXPERTS].set(router_w[l])
        rb_pad = jnp.zeros((1, LANES), F32).at[0, :N_EXPERTS].set(router_b[l])
        h2, route, experts_t = _router(x2, nw3, mod3, rw_pad, rb_pad, seq, nw_idx=2, sc_idx=4, sh_idx=3)
        slot_tok, inv_slot, block_e, n_used = _routing_tables(experts_t, t)
        x_sorted = _gather_rows(h2, slot_tok, n_used)
        yb = _experts(x_sorted, block_e, n_used, w_up_d, b_up_d, w_down_b, b_down_r, l)
        x2 = _combine(yb, inv_slot, route, x2, nw3, mod3, seq, nw_idx=3, g_idx=5)

    return x2.reshape(batch, seq, d)
```

```python
import functools

import jax
import jax.numpy as jnp
from jax import lax
from jax.experimental import pallas as pl
from jax.experimental.pallas import tpu as pltpu

F32 = jnp.float32
BF16 = jnp.bfloat16
NEG_INF = float("-inf")
MASK_PENALTY = -1e9

D_MODEL = 2048
HEAD_DIM = 128
N_BRANCH = 4
BRANCH_HEADS = 4
BRANCH_WIDTH = BRANCH_HEADS * HEAD_DIM

MOBA_BLOCK = 256
MOBA_TOPK = 3
ROPE_THETA = 10000.0

MLSTM_CHUNK = 128
MLSTM_CONV = 4

GLA_HEADS = 4
GLA_DK = 64
GLA_DV = 128
GLA_RANK = 16
GLA_TAU = 16.0
GLA_CHUNK = 64
GLA_SUB = 16

N_EXPERTS = 32
TOP_K = 4
D_EXPERT = D_MODEL // 2
SWIGLU_LIMIT = 7.0
SWIGLU_ALPHA = 1.702
MOE_ROW_BLOCK = 512

RMS_EPS = 1e-6

LANES = 128
MXU_DIM = 256
CHUNK = 128
ATT_TILE = 512

C_AQ, C_AK, C_AV = 0, 512, 1024
C_MQ, C_MK, C_MV, C_MO = 1536, 2048, 2560, 3072
C_GQ, C_GK, C_GV, C_GR = 3584, 3840, 4096, 4608
C_FQ, C_FK, C_FV = 5120, 5632, 6144
C_GATE = 6656
C_SMALL = 14848
SM_I, SM_F, SM_GA, SM_FF = 0, 4, 8, 24
N_PROJ = 15360

NN = (((1,), (0,)), ((), ()))
NT = (((1,), (1,)), ((), ()))
TN = (((0,), (0,)), ((), ()))

VMEM_LIMIT = 56 * 1024 * 1024


def _cp(sem, vmem=VMEM_LIMIT):
    return pltpu.CompilerParams(dimension_semantics=sem, vmem_limit_bytes=vmem)


def _dot(a, b, dn=NN):
    return lax.dot_general(a, b, dn, preferred_element_type=F32)


def _split3(x):
    hi = x.astype(BF16)
    r1 = x - hi.astype(F32)
    mid = r1.astype(BF16)
    lo = (r1 - mid.astype(F32)).astype(BF16)
    return hi, mid, lo


def _dot01(sel, x, dn=NN):
    hi, mid, lo = _split3(x)
    return _dot(sel, hi, dn) + _dot(sel, mid, dn) + _dot(sel, lo, dn)


def _dot_x3(a, b, dn=NN):
    ah = a.astype(BF16)
    al = (a - ah.astype(F32)).astype(BF16)
    bh = b.astype(BF16)
    bl = (b - bh.astype(F32)).astype(BF16)
    return _dot(ah, bh, dn) + _dot(al, bh, dn) + _dot(ah, bl, dn)


def _log_sigmoid(x):
    return jnp.minimum(x, 0.0) - jnp.log1p(jnp.exp(-jnp.abs(x)))


def _sigmoid(x):
    return 0.5 * jnp.tanh(0.5 * x) + 0.5


def _tril_bf16(n):
    r = lax.broadcasted_iota(jnp.int32, (n, n), 0)
    c = lax.broadcasted_iota(jnp.int32, (n, n), 1)
    return jnp.where(c <= r, 1.0, 0.0).astype(BF16)


def _row_selector(lane_idx):
    r = lax.broadcasted_iota(jnp.int32, (8, LANES), 0)
    c = lax.broadcasted_iota(jnp.int32, (8, LANES), 1)
    return jnp.where((r == 0) & (c == lane_idx), 1.0, 0.0).astype(BF16)


def _pick_lane(x, lane_idx):
    lane = lax.broadcasted_iota(jnp.int32, x.shape, 1)
    return jnp.sum(jnp.where(lane == lane_idx, x, 0.0), axis=-1, keepdims=True)


def _ada_kernel(c_ref, w_ref, b_ref, o_ref):
    c = c_ref[...]
    cond = c * _sigmoid(c)
    o_ref[0] = _dot(cond.astype(BF16), w_ref[0].astype(BF16)) + b_ref[0]


def _ada_mod(c, ada_w, ada_b):
    depth, d, n = ada_w.shape
    b = c.shape[0]
    tn = 1024
    return pl.pallas_call(
        _ada_kernel,
        grid=(depth, n // tn),
        in_specs=[
            pl.BlockSpec((b, d), lambda l, j: (0, 0)),
            pl.BlockSpec((1, d, tn), lambda l, j: (l, 0, j)),
            pl.BlockSpec((1, 1, tn), lambda l, j: (l, 0, j)),
        ],
        out_specs=pl.BlockSpec((1, b, tn), lambda l, j: (l, 0, j)),
        out_shape=jax.ShapeDtypeStruct((depth, b, n), F32),
        compiler_params=_cp(("parallel", "parallel")),
        name="ada_mod",
    )(c, ada_w, ada_b.reshape(depth, 1, n))


def _norm_mod_rows(x, nw, sc, sh):
    var = jnp.mean(x * x, axis=-1, keepdims=True)
    y = x * lax.rsqrt(var + RMS_EPS) * nw
    return y * (1.0 + sc) + sh


def _rms_rows(y, nw):
    var = jnp.mean(y * y, axis=-1, keepdims=True)
    return y * lax.rsqrt(var + RMS_EPS) * nw


def _inproj_kernel(x_ref, nw_ref, sc_ref, sh_ref, w_ref, o_ref, h_scr, *, rows):
    @pl.when(pl.program_id(1) == 0)
    def _():
        def body(r, carry):
            sl = pl.ds(pl.multiple_of(r * rows, rows), rows)
            h = _norm_mod_rows(x_ref[sl, :], nw_ref[0], sc_ref[0], sh_ref[0])
            h_scr[sl, :] = h.astype(BF16)
            return carry
        lax.fori_loop(0, x_ref.shape[0] // rows, body, 0)

    o_ref[...] = _dot(h_scr[...], w_ref[0])


def _in_proj(x2, mod3, nw3, w_re, layer, seq, *, nw_idx, sc_idx, sh_idx):
    t, d = x2.shape
    n = w_re.shape[2]
    tm = min(1024, seq)
    tn = 1024
    per_b = seq // tm
    return pl.pallas_call(
        functools.partial(_inproj_kernel, rows=min(256, tm)),
        grid=(t // tm, n // tn),
        in_specs=[
            pl.BlockSpec((tm, d), lambda i, j: (i, 0)),
            pl.BlockSpec((1, 1, d), lambda i, j: (nw_idx, 0, 0)),
            pl.BlockSpec((1, 1, d), lambda i, j: (i // per_b, 0, sc_idx)),
            pl.BlockSpec((1, 1, d), lambda i, j: (i // per_b, 0, sh_idx)),
            pl.BlockSpec((1, d, tn), lambda i, j: (layer, 0, j)),
        ],
        out_specs=pl.BlockSpec((tm, tn), lambda i, j: (i, j)),
        out_shape=jax.ShapeDtypeStruct((t, n), F32),
        scratch_shapes=[pltpu.VMEM((tm, d), BF16)],
        compiler_params=_cp(("parallel", "arbitrary")),
        name="in_proj",
    )(x2, nw3, mod3, mod3, w_re)


ATT_HEADS = 2


def _attention_sweep(i, tile, q_augs, kaug_scr, vb_scr, o_ref):
    nh = len(q_augs)
    rows_q = pl.ds(pl.multiple_of(i * tile, tile), tile)
    r_io = lax.broadcasted_iota(jnp.int32, (tile, tile), 0)
    c_io = lax.broadcasted_iota(jnp.int32, (tile, tile), 1)
    init = []
    for hh in range(nh):
        s = _dot(q_augs[hh], kaug_scr[hh, rows_q, :], NT)
        s = jnp.where(c_io <= r_io, s, NEG_INF)
        m0 = jnp.max(s, axis=-1, keepdims=True)
        p = jnp.exp(s - m0)
        init += [m0, jnp.sum(p, axis=-1, keepdims=True), _dot(p.astype(BF16), vb_scr[hh, rows_q, :])]

    def body(j, carry):
        rows = pl.ds(pl.multiple_of(j * tile, tile), tile)
        out = []
        for hh in range(nh):
            m, l, acc = carry[3 * hh:3 * hh + 3]
            s = _dot(q_augs[hh], kaug_scr[hh, rows, :], NT)
            m_new = jnp.maximum(m, jnp.max(s, axis=-1, keepdims=True))
            alpha = jnp.exp(m - m_new)
            p = jnp.exp(s - m_new)
            l = alpha * l + jnp.sum(p, axis=-1, keepdims=True)
            acc = alpha * acc + _dot(p.astype(BF16), vb_scr[hh, rows, :])
            out += [m_new, l, acc]
        return tuple(out)

    fin = lax.fori_loop(0, i, body, tuple(init))
    for hh in range(nh):
        o_ref[:, hh * HEAD_DIM:(hh + 1) * HEAD_DIM] = (fin[3 * hh + 2] / fin[3 * hh + 1]).astype(o_ref.dtype)


def _rope(x, cos, sin):
    return x * cos + pltpu.roll(x, HEAD_DIM // 2, 1) * sin


def _moba_kernel(q_ref, k_ref, v_ref, cos_ref, sin_ref, o_ref, kaug_scr, vb_scr, km_scr, *, seq, tile, n_sel):
    blk = MOBA_BLOCK
    nblk = seq // blk
    nb8 = -(-nblk // 8) * 8
    i = pl.program_id(2)

    @pl.when(i == 0)
    def _():
        km_scr[...] = jnp.zeros_like(km_scr)
        lane_b = lax.broadcasted_iota(jnp.int32, (blk, LANES), 1)
        for hh in range(ATT_HEADS):
            cols = slice(hh * HEAD_DIM, (hh + 1) * HEAD_DIM)
            for j in range(nblk):
                rows = pl.ds(j * blk, blk)
                kk = _rope(k_ref[rows, cols], cos_ref[rows, :], sin_ref[rows, :])
                kaug_scr[hh, rows, 0:HEAD_DIM] = kk.astype(BF16)
                kaug_scr[hh, rows, HEAD_DIM:2 * HEAD_DIM] = jnp.where(lane_b == j, 1.0, 0.0).astype(BF16)
                km_scr[hh, j:j + 1, :] = jnp.mean(kk, axis=0, keepdims=True)
                vb_scr[hh, rows, :] = v_ref[rows, cols].astype(BF16)

    rows_q = pl.ds(pl.multiple_of(i * tile, tile), tile)
    blk_id = lax.broadcasted_iota(jnp.int32, (nb8, tile), 0)
    blk_f = blk_id.astype(F32)
    q_blk = i * (tile // blk) + lax.broadcasted_iota(jnp.int32, (nb8, tile), 1) // blk
    q_augs = []
    for hh in range(ATT_HEADS):
        cols = slice(hh * HEAD_DIM, (hh + 1) * HEAD_DIM)
        q = _rope(q_ref[:, cols], cos_ref[rows_q, :], sin_ref[rows_q, :])
        g = _dot_x3(km_scr[hh], q, NT)[0:nb8, :]
        g = jnp.where(blk_id < q_blk, g, NEG_INF)
        keep = blk_id == q_blk
        for _ in range(n_sel):
            m = jnp.max(g, axis=0, keepdims=True)
            idx = jnp.min(jnp.where(g == m, blk_f, float(LANES)), axis=0, keepdims=True)
            hit = blk_f == idx
            keep = keep | (hit & (m > NEG_INF))
            g = jnp.where(hit, NEG_INF, g)
        pen_t = jnp.where(keep, 0.0, MASK_PENALTY)
        pen_t = jnp.concatenate([pen_t, jnp.full((LANES - nb8, tile), MASK_PENALTY, F32)], axis=0)
        penalty = pen_t.T
        q_augs.append(jnp.concatenate([(q * (HEAD_DIM ** -0.5)).astype(BF16), penalty.astype(BF16)], axis=1))
    _attention_sweep(i, tile, q_augs, kaug_scr, vb_scr, o_ref)


def _moba(proj, cos, sin, batch, seq):
    tile = min(ATT_TILE, seq)
    nq = seq // tile
    nblk = seq // MOBA_BLOCK
    n_sel = max(1, min(MOBA_TOPK, nblk - 1))
    hb = HEAD_DIM
    hw = ATT_HEADS * hb
    return pl.pallas_call(
        functools.partial(_moba_kernel, seq=seq, tile=tile, n_sel=n_sel),
        grid=(batch, BRANCH_HEADS // ATT_HEADS, nq),
        in_specs=[
            pl.BlockSpec((tile, hw), lambda b, p, i: (b * nq + i, C_AQ // hw + p)),
            pl.BlockSpec((seq, hw), lambda b, p, i: (b, C_AK // hw + p)),
            pl.BlockSpec((seq, hw), lambda b, p, i: (b, C_AV // hw + p)),
            pl.BlockSpec((seq, hb), lambda b, p, i: (0, 0)),
            pl.BlockSpec((seq, hb), lambda b, p, i: (0, 0)),
        ],
        out_specs=pl.BlockSpec((tile, hw), lambda b, p, i: (b * nq + i, p)),
        out_shape=jax.ShapeDtypeStruct((batch * seq, BRANCH_WIDTH), BF16),
        scratch_shapes=[pltpu.VMEM((ATT_HEADS, seq, 2 * hb), BF16), pltpu.VMEM((ATT_HEADS, seq, hb), BF16),
                        pltpu.VMEM((ATT_HEADS, LANES, hb), F32)],
        compiler_params=_cp(("parallel", "parallel", "arbitrary")),
        name="moba",
    )(proj, proj, proj, cos, sin)


def _fox_kernel(q_ref, k_ref, v_ref, sm_ref, bias_ref, o_ref, kaug_scr, vb_scr, fcol_scr, *, seq, tile):
    p_idx = pl.program_id(1)
    i = pl.program_id(2)

    def bias_lanes(f, first):
        hi, mid, lo = _split3(f)
        lane = lax.broadcasted_iota(jnp.int32, (f.shape[0], LANES), 1)
        ones_first = 3 - first
        out = jnp.where((lane >= ones_first) & (lane < ones_first + 3), 1.0, 0.0)
        out = jnp.where(lane == first, hi.astype(F32), out)
        out = jnp.where(lane == first + 1, mid.astype(F32), out)
        out = jnp.where(lane == first + 2, lo.astype(F32), out)
        return out.astype(BF16)

    @pl.when(i == 0)
    def _():
        tril = _tril_bf16(CHUNK)
        carry = jnp.zeros((1, LANES), F32)
        for c in range(seq // CHUNK):
            rows = pl.ds(c * CHUNK, CHUNK)
            lf = _log_sigmoid(sm_ref[rows, :] + bias_ref[...])
            cs = _dot01(tril, lf) + carry
            fcol_scr[rows, :] = cs
            carry = cs[CHUNK - 1:CHUNK, :]
        for hh in range(ATT_HEADS):
            cols = slice(hh * HEAD_DIM, (hh + 1) * HEAD_DIM)
            f_lane = SM_FF + p_idx * ATT_HEADS + hh
            for c in range(seq // tile):
                rows = pl.ds(c * tile, tile)
                kaug_scr[hh, rows, 0:HEAD_DIM] = k_ref[rows, cols].astype(BF16)
                kaug_scr[hh, rows, HEAD_DIM:2 * HEAD_DIM] = bias_lanes(-_pick_lane(fcol_scr[rows, :], f_lane), 3)
                vb_scr[hh, rows, :] = v_ref[rows, cols].astype(BF16)

    rows_q = pl.ds(pl.multiple_of(i * tile, tile), tile)
    fq = fcol_scr[rows_q, :]
    q_augs = []
    for hh in range(ATT_HEADS):
        cols = slice(hh * HEAD_DIM, (hh + 1) * HEAD_DIM)
        f_t = _pick_lane(fq, SM_FF + p_idx * ATT_HEADS + hh)
        q_augs.append(jnp.concatenate([(q_ref[:, cols] * (HEAD_DIM ** -0.5)).astype(BF16), bias_lanes(f_t, 0)], axis=1))
    _attention_sweep(i, tile, q_augs, kaug_scr, vb_scr, o_ref)


def _fox(proj, small_bias, batch, seq):
    tile = min(ATT_TILE, seq)
    nq = seq // tile
    hb = HEAD_DIM
    hw = ATT_HEADS * hb
    return pl.pallas_call(
        functools.partial(_fox_kernel, seq=seq, tile=tile),
        grid=(batch, BRANCH_HEADS // ATT_HEADS, nq),
        in_specs=[
            pl.BlockSpec((tile, hw), lambda b, p, i: (b * nq + i, C_FQ // hw + p)),
            pl.BlockSpec((seq, hw), lambda b, p, i: (b, C_FK // hw + p)),
            pl.BlockSpec((seq, hw), lambda b, p, i: (b, C_FV // hw + p)),
            pl.BlockSpec((seq, LANES), lambda b, p, i: (b, C_SMALL // LANES)),
            pl.BlockSpec((1, LANES), lambda b, p, i: (0, 0)),
        ],
        out_specs=pl.BlockSpec((tile, hw), lambda b, p, i: (b * nq + i, p)),
        out_shape=jax.ShapeDtypeStruct((batch * seq, BRANCH_WIDTH), BF16),
        scratch_shapes=[pltpu.VMEM((ATT_HEADS, seq, 2 * hb), BF16), pltpu.VMEM((ATT_HEADS, seq, hb), BF16),
                        pltpu.VMEM((seq, LANES), F32)],
        compiler_params=_cp(("parallel", "parallel", "arbitrary")),
        name="fox",
    )(proj, proj, proj, proj, small_bias)


def _conv_silu_rows(x_ref, w_ref, b_ref, r0, n):
    k = w_ref.shape[0]
    x0 = x_ref[r0:r0 + n, :]
    y = x0 * w_ref[k - 1:k, :] + b_ref[...]
    for sft in range(1, k):
        if r0 >= sft:
            xs = x_ref[r0 - sft:r0 - sft + n, :]
        else:
            row = lax.broadcasted_iota(jnp.int32, x0.shape, 0)
            xs = jnp.where(row >= sft, pltpu.roll(x0, sft, 0), 0.0)
        y = y + xs * w_ref[k - 1 - sft:k - sft, :]
    return y * _sigmoid(y)


MLSTM_UNROLL = 16


def _mlstm_kernel(q_ref, k_ref, v_ref, o_ref, sm_ref, bias_ref, cwq_ref, cwk_ref, cbq_ref, cbk_ref,
                  nw_ref, y_ref, q_scr, k_scr, x_scr, g_scr, irow_scr, grow_scr, *, seq):
    h = pl.program_id(1)
    L = MLSTM_CHUNK
    nc = seq // L
    unroll = MLSTM_UNROLL if nc % MLSTM_UNROLL == 0 else (nc if nc < MLSTM_UNROLL else 1)
    i_lane = SM_I + h
    f_lane = SM_F + h

    tril = _tril_bf16(L)
    sel_i = _row_selector(i_lane)
    sel_f = _row_selector(f_lane)
    lane_s = lax.broadcasted_iota(jnp.int32, (L, LANES), 1)
    for c in range(nc):
        r0 = c * L
        rows = pl.ds(r0, L)
        q_scr[rows, :] = _conv_silu_rows(q_ref, cwq_ref, cbq_ref, r0, L)
        k_scr[rows, :] = _conv_silu_rows(k_ref, cwk_ref, cbk_ref, r0, L) * (HEAD_DIM ** -0.5)
        smb = sm_ref[rows, :] + bias_ref[...]
        xc = jnp.where(lane_s < SM_F, smb, _log_sigmoid(smb))
        gc = _dot01(tril, xc)
        x_scr[rows, :] = xc
        g_scr[rows, :] = gc
        irow_scr[:, r0:r0 + L] = _dot01(sel_i, xc, NT)
        grow_scr[:, r0:r0 + L] = _dot01(sel_f, gc, NT)

    r_io = lax.broadcasted_iota(jnp.int32, (L, L), 0)
    c_io = lax.broadcasted_iota(jnp.int32, (L, L), 1)
    causal = c_io <= r_io

    def chunk(c, carry):
        C, n, m = carry
        rows = pl.ds(pl.multiple_of(c * L, L), L)
        qc = q_scr[rows, :]
        kc = k_scr[rows, :]
        vb = v_ref[rows, :].astype(BF16)
        qb = qc.astype(BF16)
        g_col = _pick_lane(g_scr[rows, :], f_lane)
        i_col = _pick_lane(x_scr[rows, :], i_lane)
        g_row = grow_scr[0:1, rows]
        i_row = irow_scr[0:1, rows]
        dmat = jnp.where(causal, g_col - g_row + i_row, NEG_INF)
        a = g_col + m
        m_out = jnp.maximum(a, jnp.max(dmat, axis=-1, keepdims=True))
        w = jnp.exp(dmat - m_out) * _dot(qb, kc.astype(BF16), NT)
        inter = jnp.exp(a - m_out)
        num = inter * _dot(qb, C.astype(BF16)) + _dot(w.astype(BF16), vb)
        den = inter * jnp.sum(qc * n, axis=-1, keepdims=True) + jnp.sum(w, axis=-1, keepdims=True)
        hh = num / jnp.maximum(jnp.abs(den), jnp.exp(-m_out))
        g_last = g_row[:, L - 1:L]
        lw = g_last - g_col + i_col
        m_new = jnp.maximum(g_last + m, jnp.max(lw, axis=0, keepdims=True))
        wk = jnp.exp(lw - m_new)
        decay = jnp.exp(g_last + m - m_new)
        kw = kc * wk
        C = decay * C + _dot(kw.astype(BF16), vb, TN)
        n = decay * n + jnp.sum(kw, axis=0, keepdims=True)
        yn = _rms_rows(hh, nw_ref[0])
        y_ref[rows, :] = (yn * _sigmoid(o_ref[rows, :])).astype(y_ref.dtype)
        return C, n, m_new

    def step(it, carry):
        for u in range(unroll):
            carry = chunk(it * unroll + u, carry)
        return carry

    init = (jnp.zeros((HEAD_DIM, HEAD_DIM), F32), jnp.zeros((1, HEAD_DIM), F32), jnp.zeros((1, 1), F32))
    lax.fori_loop(0, nc // unroll, step, init)


def _mlstm(proj, small_bias, conv_w, conv_b, norm_w, batch, seq):
    hb = HEAD_DIM
    nh = BRANCH_HEADS
    blk = lambda c0: pl.BlockSpec((seq, hb), lambda b, h: (b, c0 // hb + h))
    return pl.pallas_call(
        functools.partial(_mlstm_kernel, seq=seq),
        grid=(batch, nh),
        in_specs=[
            blk(C_MQ), blk(C_MK), blk(C_MV), blk(C_MO),
            pl.BlockSpec((seq, LANES), lambda b, h: (b, C_SMALL // LANES)),
            pl.BlockSpec((1, LANES), lambda b, h: (0, 0)),
            pl.BlockSpec((MLSTM_CONV, hb), lambda b, h: (0, h)),
            pl.BlockSpec((MLSTM_CONV, hb), lambda b, h: (0, nh + h)),
            pl.BlockSpec((1, hb), lambda b, h: (0, h)),
            pl.BlockSpec((1, hb), lambda b, h: (0, nh + h)),
            pl.BlockSpec((1, 1, hb), lambda b, h: (h, 0, 0)),
        ],
        out_specs=pl.BlockSpec((seq, hb), lambda b, h: (b, h)),
        out_shape=jax.ShapeDtypeStruct((batch * seq, BRANCH_WIDTH), BF16),
        scratch_shapes=[
            pltpu.VMEM((seq, hb), F32), pltpu.VMEM((seq, hb), F32),
            pltpu.VMEM((seq, LANES), F32), pltpu.VMEM((seq, LANES), F32),
            pltpu.VMEM((8, seq), F32), pltpu.VMEM((8, seq), F32),
        ],
        compiler_params=_cp(("parallel", "parallel")),
        name="mlstm",
    )(proj, proj, proj, proj, proj, small_bias, conv_w, conv_w, conv_b.reshape(1, -1), conv_b.reshape(1, -1),
      norm_w.reshape(nh, 1, hb))


GLA_UNROLL = 8


def _gla_kernel(q_ref, k_ref, v_ref, r_ref, sm_ref, wa_ref, ba_ref, nw_ref, y_ref, la_scr, *, seq):
    L = GLA_CHUNK
    sub = GLA_SUB
    nc = seq // L
    unroll = GLA_UNROLL if nc % GLA_UNROLL == 0 else (nc if nc < GLA_UNROLL else 1)
    dk, dv = GLA_DK, GLA_DV

    for c in range(seq // CHUNK):
        rows = pl.ds(c * CHUNK, CHUNK)
        z = _dot_x3(sm_ref[rows, :], wa_ref[...]) + ba_ref[...]
        la_scr[rows, :] = _log_sigmoid(z) * (1.0 / GLA_TAU)
    tril = _tril_bf16(L)
    r_io = lax.broadcasted_iota(jnp.int32, (sub, L), 0)
    c_io = lax.broadcasted_iota(jnp.int32, (sub, L), 1)
    row_l = lax.broadcasted_iota(jnp.int32, (L, dk), 0)

    def chunk(c, states):
        rows = pl.ds(pl.multiple_of(c * L, L), L)
        bc2 = _dot01(tril, la_scr[rows, :])
        q2 = q_ref[rows, :] * (dk ** -0.5)
        k2 = k_ref[rows, :]
        new_states = []
        for hh in range(2):
            st = states[hh]
            lanes = slice(hh * dk, (hh + 1) * dk)
            b = bc2[:, lanes]
            qh = q2[:, lanes]
            kh = k2[:, lanes]
            vb = v_ref[rows, hh * dv:(hh + 1) * dv].astype(BF16)
            o = _dot((qh * jnp.exp(b)).astype(BF16), st.astype(BF16), NT)
            blocks = []
            for ib in range(L // sub):
                ref_row = b[ib * sub:ib * sub + 1, :]
                qi = qh[ib * sub:(ib + 1) * sub, :] * jnp.exp(b[ib * sub:(ib + 1) * sub, :] - ref_row)
                ki = kh * jnp.exp(jnp.where(row_l < (ib + 1) * sub, ref_row - b, 0.0))
                a = _dot(qi.astype(BF16), ki.astype(BF16), NT)
                blocks.append(jnp.where(c_io <= r_io + ib * sub, a, 0.0))
            attn = jnp.concatenate(blocks, axis=0)
            o = o + _dot(attn.astype(BF16), vb)
            b_last = b[L - 1:L, :]
            ke = kh * jnp.exp(b_last - b)
            st = jnp.exp(b_last) * st + _dot(vb, ke.astype(BF16), TN)
            new_states.append(st)
            yn = _rms_rows(o, nw_ref[0, hh:hh + 1, :])
            r = r_ref[rows, hh * dv:(hh + 1) * dv]
            y_ref[rows, hh * dv:(hh + 1) * dv] = (yn * (r * _sigmoid(r))).astype(y_ref.dtype)
        return tuple(new_states)

    def step(it, states):
        for u in range(unroll):
            states = chunk(it * unroll + u, states)
        return states

    init = (jnp.zeros((dv, dk), F32), jnp.zeros((dv, dk), F32))
    lax.fori_loop(0, nc // unroll, step, init)


def _gla(proj, w_a2_pad, b_a2, norm_w, batch, seq):
    return pl.pallas_call(
        functools.partial(_gla_kernel, seq=seq),
        grid=(batch, GLA_HEADS // 2),
        in_specs=[
            pl.BlockSpec((seq, LANES), lambda b, p: (b, C_GQ // LANES + p)),
            pl.BlockSpec((seq, LANES), lambda b, p: (b, C_GK // LANES + p)),
            pl.BlockSpec((seq, 2 * GLA_DV), lambda b, p: (b, C_GV // (2 * GLA_DV) + p)),
            pl.BlockSpec((seq, 2 * GLA_DV), lambda b, p: (b, C_GR // (2 * GLA_DV) + p)),
            pl.BlockSpec((seq, LANES), lambda b, p: (b, C_SMALL // LANES)),
            pl.BlockSpec((LANES, LANES), lambda b, p: (0, p)),
            pl.BlockSpec((1, LANES), lambda b, p: (0, p)),
            pl.BlockSpec((1, 2, GLA_DV), lambda b, p: (p, 0, 0)),
        ],
        out_specs=pl.BlockSpec((seq, 2 * GLA_DV), lambda b, p: (b, p)),
        out_shape=jax.ShapeDtypeStruct((batch * seq, GLA_HEADS * GLA_DV), BF16),
        scratch_shapes=[pltpu.VMEM((seq, LANES), F32)],
        compiler_params=_cp(("parallel", "parallel")),
        name="gla",
    )(proj, proj, proj, proj, proj, w_a2_pad, b_a2.reshape(1, -1), norm_w.reshape(GLA_HEADS // 2, 2, GLA_DV))


def _merge_kernel(ya_ref, yb_ref, yc_ref, yd_ref, g0_ref, g1_ref, g2_ref, g3_ref, w_ref, o_ref):
    ys = (ya_ref, yb_ref, yc_ref, yd_ref)
    gs = (g0_ref, g1_ref, g2_ref, g3_ref)
    acc = None
    for b in range(N_BRANCH):
        term = _sigmoid(gs[b][...]) * _dot(ys[b][...], w_ref[b])
        acc = term if acc is None else acc + term
    o_ref[...] = acc.astype(o_ref.dtype)


def _merge(ys, proj, w_branch_bf16, layer):
    t = proj.shape[0]
    d = D_MODEL
    tm, tn = 1024, 512
    tm = min(tm, t)
    yspec = pl.BlockSpec((tm, BRANCH_WIDTH), lambda i, j: (i, 0))
    gspec = lambda b: pl.BlockSpec((tm, tn), lambda i, j: (i, (C_GATE + b * d) // tn + j))
    return pl.pallas_call(
        _merge_kernel,
        grid=(t // tm, d // tn),
        in_specs=[yspec] * 4 + [gspec(b) for b in range(N_BRANCH)]
        + [pl.BlockSpec((N_BRANCH, BRANCH_WIDTH, tn), lambda i, j: (layer, 0, j))],
        out_specs=pl.BlockSpec((tm, tn), lambda i, j: (i, j)),
        out_shape=jax.ShapeDtypeStruct((t, d), BF16),
        compiler_params=_cp(("parallel", "parallel")),
        name="merge",
    )(*ys, proj, proj, proj, proj, w_branch_bf16)


def _outproj_kernel(m_ref, w_ref, x_ref, nw_ref, g_ref, o_ref):
    y = _dot(m_ref[...], w_ref[0])
    o_ref[...] = x_ref[...] + g_ref[0] * _rms_rows(y, nw_ref[0])


def _out_proj(merged, w_out_bf16, layer, x2, nw3, mod3, seq, *, nw_idx, g_idx):
    t, d = x2.shape
    tm = min(512, seq)
    per_b = seq // tm
    return pl.pallas_call(
        _outproj_kernel,
        grid=(t // tm,),
        in_specs=[
            pl.BlockSpec((tm, d), lambda i: (i, 0)),
            pl.BlockSpec((1, d, d), lambda i: (layer, 0, 0)),
            pl.BlockSpec((tm, d), lambda i: (i, 0)),
            pl.BlockSpec((1, 1, d), lambda i: (nw_idx, 0, 0)),
            pl.BlockSpec((1, 1, d), lambda i: (i // per_b, 0, g_idx)),
        ],
        out_specs=pl.BlockSpec((tm, d), lambda i: (i, 0)),
        out_shape=jax.ShapeDtypeStruct((t, d), F32),
        compiler_params=_cp(("parallel",)),
        name="out_proj",
    )(merged, w_out_bf16, x2, nw3, mod3)


def _router_kernel(x_ref, nw_ref, sc_ref, sh_ref, rw_ref, rb_ref, h_ref, r_ref, et_ref):
    h = _norm_mod_rows(x_ref[...], nw_ref[0], sc_ref[0], sh_ref[0])
    h_ref[...] = h
    logits = _dot_x3(h, rw_ref[...]) + rb_ref[...]
    lane = lax.broadcasted_iota(jnp.int32, logits.shape, 1)
    lane_f = lane.astype(F32)
    l = jnp.where(lane < N_EXPERTS, logits, NEG_INF)
    vals, idxs = [], []
    for _ in range(TOP_K):
        m = jnp.max(l, axis=-1, keepdims=True)
        idx = jnp.min(jnp.where(l == m, lane_f, float(LANES)), axis=-1, keepdims=True)
        vals.append(m)
        idxs.append(idx)
        l = jnp.where(lane_f == idx, NEG_INF, l)
    es = [jnp.exp(v - vals[0]) for v in vals]
    tot = es[0]
    for e in es[1:]:
        tot = tot + e
    out = jnp.zeros(logits.shape, F32)
    for r in range(TOP_K):
        out = jnp.where(lane == r, es[r] / tot, out)
        out = jnp.where(lane == TOP_K + r, idxs[r], out)
    r_ref[...] = out
    et_ref[...] = out.T[TOP_K:TOP_K + 8, :].astype(jnp.int32)


def _router(x2, nw3, mod3, router_w_pad, router_b_pad, seq, *, nw_idx, sc_idx, sh_idx):
    t, d = x2.shape
    tm = min(256, seq)
    per_b = seq // tm
    return pl.pallas_call(
        _router_kernel,
        grid=(t // tm,),
        in_specs=[
            pl.BlockSpec((tm, d), lambda i: (i, 0)),
            pl.BlockSpec((1, 1, d), lambda i: (nw_idx, 0, 0)),
            pl.BlockSpec((1, 1, d), lambda i: (i // per_b, 0, sc_idx)),
            pl.BlockSpec((1, 1, d), lambda i: (i // per_b, 0, sh_idx)),
            pl.BlockSpec((d, LANES), lambda i: (0, 0)),
            pl.BlockSpec((1, LANES), lambda i: (0, 0)),
        ],
        out_specs=[pl.BlockSpec((tm, d), lambda i: (i, 0)), pl.BlockSpec((tm, LANES), lambda i: (i, 0)),
                   pl.BlockSpec((8, tm), lambda i: (0, i))],
        out_shape=[jax.ShapeDtypeStruct((t, d), F32), jax.ShapeDtypeStruct((t, LANES), F32),
                   jax.ShapeDtypeStruct((8, t), jnp.int32)],
        compiler_params=_cp(("parallel",)),
        name="router",
    )(x2, nw3, mod3, mod3, router_w_pad, router_b_pad)


def _deinterleave_kernel(w_ref, o_ref):
    n = MXU_DIM
    r = lax.broadcasted_iota(jnp.int32, (n, n), 0)
    c = lax.broadcasted_iota(jnp.int32, (n, n), 1)
    src = jnp.where(c < n // 2, 2 * c, 2 * (c - n // 2) + 1)
    perm = jnp.where(r == src, 1.0, 0.0).astype(BF16)
    for m in range(w_ref.shape[1] // n):
        cols = slice(m * n, (m + 1) * n)
        o_ref[:, cols] = _dot(w_ref[:, cols].astype(BF16), perm).astype(BF16)


def _deinterleave_cast(w2):
    rows, n = w2.shape
    tr = 1024
    return pl.pallas_call(
        _deinterleave_kernel,
        grid=(rows // tr,),
        in_specs=[pl.BlockSpec((tr, n), lambda i: (i, 0))],
        out_specs=pl.BlockSpec((tr, n), lambda i: (i, 0)),
        out_shape=jax.ShapeDtypeStruct((rows, n), BF16),
        compiler_params=_cp(("parallel",)),
        name="deinterleave_cast",
    )(w2)


def _expert_kernel(be_ref, nused_ref, idx_hbm, h_hbm, wu_ref, bu_ref, wd_ref, bd_ref, o_ref,
                   xbuf, idx_smem, isems, rsems, *, rows, nb):
    b = pl.program_id(0)
    slot = b % 2
    nxt = 1 - slot
    n_used = nused_ref[0]

    def idx_copy(blk, s):
        return pltpu.make_async_copy(idx_hbm.at[blk], idx_smem.at[s], isems.at[s])

    def row_copy(tok, r, s):
        return pltpu.make_async_copy(h_hbm.at[pl.ds(tok, 1), :], xbuf.at[s, pl.ds(r, 1), :], rsems.at[s])

    def drain_rows(s):
        def body(r, carry):
            row_copy(0, 0, s).wait()
            return carry
        lax.fori_loop(0, rows, body, 0, unroll=8)

    @pl.when(b == 0)
    def _():
        idx_copy(0, 0).start()
        idx_copy(0, 0).wait()

        def issue(r, carry):
            row_copy(idx_smem[0, r], r, 0).start()
            return carry
        lax.fori_loop(0, rows, issue, 0, unroll=8)
        idx_copy(1, 1).start()

    @pl.when(b < n_used)
    def _():
        idx_copy(0, nxt).wait()
        drain_rows(slot)
        issued = [0]

        def issue_next(n):
            for r in range(issued[0], issued[0] + n):
                row_copy(idx_smem[nxt, r], r, nxt).start()
            issued[0] += n

        x = xbuf[slot].astype(BF16)
        half = MXU_DIM // 2
        n_up = wu_ref.shape[2] // MXU_DIM
        n_down = 4
        per_up = (rows // (n_up + n_down) + 7) // 8 * 8
        acts = []
        for m in range(n_up):
            issue_next(per_up)
            cols = slice(m * MXU_DIM, (m + 1) * MXU_DIM)
            hu = _dot(x, wu_ref[0, :, cols]) + bu_ref[0, :, cols]
            g = jnp.minimum(hu[:, :half], SWIGLU_LIMIT)
            lin = jnp.clip(hu[:, half:], -SWIGLU_LIMIT, SWIGLU_LIMIT)
            acts.append((g * _sigmoid(SWIGLU_ALPHA * g) * (lin + 1.0)).astype(BF16))
        act = jnp.concatenate(acts, axis=1)
        dn = o_ref.shape[1] // n_down
        per_down = (rows - n_up * per_up) // n_down
        for m in range(n_down):
            issue_next(per_down if m < n_down - 1 else rows - issued[0])
            cols = slice(m * dn, (m + 1) * dn)
            o_ref[:, cols] = _dot(act, wd_ref[0, :, cols]) + bd_ref[0, :, cols]
        idx_copy(b + 2, slot).start()

    @pl.when(b >= n_used)
    def _():
        o_ref[...] = jnp.zeros_like(o_ref)

    @pl.when(b == n_used)
    def _():
        idx_copy(0, nxt).wait()
        drain_rows(slot)

    @pl.when((b == nb - 1) & (b < n_used))
    def _():
        idx_copy(0, slot).wait()
        drain_rows(nxt)


def _experts(h2, slot_tok, block_e, n_used, wu, bu, wd, bd, layer):
    d = h2.shape[1]
    rows = MOE_ROW_BLOCK
    nb = slot_tok.shape[0] // rows
    de = D_EXPERT
    idx = jnp.concatenate([slot_tok, jnp.zeros((2 * rows,), jnp.int32)]).reshape(nb + 2, rows)

    def wmap(b, be, nu):
        return (layer * N_EXPERTS + be[b], 0, 0)

    grid_spec = pltpu.PrefetchScalarGridSpec(
        num_scalar_prefetch=2,
        grid=(nb,),
        in_specs=[
            pl.BlockSpec(memory_space=pl.ANY),
            pl.BlockSpec(memory_space=pl.ANY),
            pl.BlockSpec((1, d, 2 * de), wmap),
            pl.BlockSpec((1, 1, 2 * de), wmap),
            pl.BlockSpec((1, de, d), wmap),
            pl.BlockSpec((1, 1, d), wmap),
        ],
        out_specs=pl.BlockSpec((rows, d), lambda b, be, nu: (b, 0)),
        scratch_shapes=[pltpu.VMEM((2, rows, d), F32), pltpu.SMEM((2, rows), jnp.int32),
                        pltpu.SemaphoreType.DMA((2,)), pltpu.SemaphoreType.DMA((2,))],
    )
    return pl.pallas_call(
        functools.partial(_expert_kernel, rows=rows, nb=nb),
        grid_spec=grid_spec,
        out_shape=jax.ShapeDtypeStruct((nb * rows, d), F32),
        compiler_params=_cp(("arbitrary",)),
        name="experts",
    )(block_e, n_used, idx, h2, wu, bu, wd, bd)


def _combine_kernel(idx_hbm, yb_hbm, r_ref, x_ref, nw_ref, g_ref, o_ref, ybuf, idx_smem, isems, rsems, *, tm, nt):
    i = pl.program_id(0)
    slot = i % 2
    n_rows = TOP_K * tm

    def idx_copy(tile, s):
        return pltpu.make_async_copy(idx_hbm.at[tile], idx_smem.at[s], isems.at[s])

    def row_copy(src_row, r, s):
        return pltpu.make_async_copy(yb_hbm.at[pl.ds(src_row, 1), :], ybuf.at[s, pl.ds(r, 1), :], rsems.at[s])

    def issue_rows(s):
        def body(r2, carry):
            for u in range(2):
                r = 2 * r2 + u
                row_copy(idx_smem[s, r], r, s).start(priority=u)
            return carry
        lax.fori_loop(0, n_rows // 2, body, 0, unroll=4)

    @pl.when(i == 0)
    def _():
        idx_copy(0, 0).start()
        idx_copy(0, 0).wait()
        issue_rows(0)

        @pl.when(nt > 1)
        def _():
            idx_copy(1, 1).start()

    @pl.when(i + 1 < nt)
    def _():
        idx_copy(0, 1 - slot).wait()
        issue_rows(1 - slot)

    @pl.when(i + 2 < nt)
    def _():
        idx_copy(i + 2, slot).start()

    def drain(r, carry):
        row_copy(0, 0, slot).wait()
        return carry
    lax.fori_loop(0, n_rows, drain, 0, unroll=8)

    gates = r_ref[...]
    y = None
    for k in range(TOP_K):
        term = gates[:, k:k + 1] * ybuf[slot, k * tm:(k + 1) * tm, :]
        y = term if y is None else y + term
    o_ref[...] = x_ref[...] + g_ref[0] * _rms_rows(y, nw_ref[0])


def _combine(yb, inv_slot, route, x2, nw3, mod3, seq, *, nw_idx, g_idx):
    t, d = x2.shape
    tm = min(256, seq)
    nt = t // tm
    per_b = seq // tm
    idx_tiles = inv_slot.reshape(TOP_K, nt, tm).transpose(1, 0, 2).reshape(nt, TOP_K * tm)
    return pl.pallas_call(
        functools.partial(_combine_kernel, tm=tm, nt=nt),
        grid=(nt,),
        in_specs=[
            pl.BlockSpec(memory_space=pl.ANY),
            pl.BlockSpec(memory_space=pl.ANY),
            pl.BlockSpec((tm, LANES), lambda i: (i, 0)),
            pl.BlockSpec((tm, d), lambda i: (i, 0)),
            pl.BlockSpec((1, 1, d), lambda i: (nw_idx, 0, 0)),
            pl.BlockSpec((1, 1, d), lambda i: (i // per_b, 0, g_idx)),
        ],
        out_specs=pl.BlockSpec((tm, d), lambda i: (i, 0)),
        out_shape=jax.ShapeDtypeStruct((t, d), F32),
        scratch_shapes=[pltpu.VMEM((2, TOP_K * tm, d), F32), pltpu.SMEM((2, TOP_K * tm), jnp.int32),
                        pltpu.SemaphoreType.DMA((2,)), pltpu.SemaphoreType.DMA((2,))],
        compiler_params=_cp(("arbitrary",)),
        name="combine",
    )(idx_tiles, yb, route, x2, nw3, mod3)


_W_IN_SEGMENTS = (
    (0, C_AQ, 3072),
    (3080, C_MO, 512),
    (3592, C_GQ, 1024),
    (4632, C_GR, 512),
    (5144, C_FQ, 1536),
    (6684, C_GATE, 8192),
)
_W_IN_SMALL = ((3072, SM_I, 8), (4616, SM_GA, GLA_RANK), (6680, SM_FF, BRANCH_HEADS))


def _reorder_kernel(w_ref, o_ref):
    rows = w_ref.shape[1]
    for src, dst, n in _W_IN_SEGMENTS:
        o_ref[0, :, dst:dst + n] = w_ref[0, :, src:src + n].astype(BF16)
    r = lax.broadcasted_iota(jnp.int32, (LANES, LANES), 0)
    c = lax.broadcasted_iota(jnp.int32, (LANES, LANES), 1)
    small = jnp.zeros((rows, LANES), F32)
    for src, lane0, n in _W_IN_SMALL:
        base = src // LANES * LANES
        sel = jnp.where((c >= lane0) & (c < lane0 + n) & (r == c - lane0 + (src - base)), 1.0, 0.0).astype(BF16)
        small = small + _dot(w_ref[0, :, base:base + LANES].astype(BF16), sel)
    o_ref[0, :, C_SMALL:C_SMALL + LANES] = small.astype(BF16)
    o_ref[0, :, C_SMALL + LANES:] = jnp.zeros((rows, N_PROJ - C_SMALL - LANES), BF16)


def _reorder_w_in(w_in):
    depth, d, n = w_in.shape
    tr = 256
    return pl.pallas_call(
        _reorder_kernel,
        grid=(depth, d // tr),
        in_specs=[pl.BlockSpec((1, tr, n), lambda l, i: (l, i, 0))],
        out_specs=pl.BlockSpec((1, tr, N_PROJ), lambda l, i: (l, i, 0)),
        out_shape=jax.ShapeDtypeStruct((depth, d, N_PROJ), BF16),
        compiler_params=_cp(("parallel", "parallel")),
        name="reorder_w_in",
    )(w_in)


def _routing_tables(experts_t, t):
    tk = t * TOP_K
    rb = MOE_ROW_BLOCK
    flat_e = experts_t[:TOP_K].reshape(-1)
    flat_ids = jnp.arange(tk, dtype=jnp.int32)
    e_sorted, order = lax.sort((flat_e, flat_ids), num_keys=1, is_stable=True)
    experts = jnp.arange(N_EXPERTS, dtype=jnp.int32)
    counts = jnp.sum((flat_e[:, None] == experts[None, :]).astype(jnp.int32), axis=0)
    padded = (counts + rb - 1) // rb * rb
    pad_end = jnp.cumsum(padded)
    pad_start = pad_end - padded
    start = jnp.cumsum(counts) - counts
    slot_sorted = pad_start[e_sorted] + flat_ids - start[e_sorted]
    _, inv_slot = lax.sort((order, slot_sorted), num_keys=1)
    n_blocks = -(-tk // rb) + N_EXPERTS
    blk_first = jnp.arange(n_blocks, dtype=jnp.int32) * rb
    block_e = jnp.minimum(jnp.sum((pad_end[None, :] <= blk_first[:, None]).astype(jnp.int32), axis=1),
                          N_EXPERTS - 1).astype(jnp.int32)
    pos = jnp.arange(n_blocks * rb, dtype=jnp.int32)
    e_pos = jnp.repeat(block_e, rb)
    r_pos = pos - pad_start[e_pos]
    src = jnp.clip(start[e_pos] + r_pos, 0, tk - 1)
    slot_tok = jnp.where(r_pos < counts[e_pos], order[src] % t, 0).astype(jnp.int32)
    n_used = (pad_end[-1] // rb).astype(jnp.int32).reshape(1)
    return slot_tok, inv_slot.astype(jnp.int32), block_e, n_used


def _rope_tables(seq):
    half = HEAD_DIM // 2
    inv = ROPE_THETA ** (-jnp.arange(half, dtype=F32) / half)
    ang = jnp.arange(seq).astype(F32)[:, None] * inv[None, :]
    cos, sin = jnp.cos(ang), jnp.sin(ang)
    return jnp.concatenate([cos, cos], axis=-1), jnp.concatenate([-sin, sin], axis=-1)


def _deinterleave_bias(b_up):
    depth, e, n = b_up.shape
    half = MXU_DIM // 2
    return b_up.reshape(depth, e, n // MXU_DIM, half, 2).transpose(0, 1, 2, 4, 3).reshape(depth, e, 1, n)


def kernel(x, c, ada_w, ada_b, norm_w, w_in, conv_w, conv_b, mlstm_i_b, mlstm_f_b, mlstm_norm_w, gla_w_a2, gla_b_a2, gla_norm_w, fox_f_b, w_branch, w_out, router_w, router_b, w_up, b_up, w_down, b_down):
    batch, seq, d = x.shape
    depth = ada_w.shape[0]
    t = batch * seq
    x2 = x.reshape(t, d)
    mod = _ada_mod(c, ada_w, ada_b)
    cos, sin = _rope_tables(seq)
    w_up_d = _deinterleave_cast(w_up.reshape(depth * N_EXPERTS * d, 2 * D_EXPERT))
    w_up_d = w_up_d.reshape(depth * N_EXPERTS, d, 2 * D_EXPERT)
    b_up_d = _deinterleave_bias(b_up).reshape(depth * N_EXPERTS, 1, 2 * D_EXPERT)
    w_down_b = w_down.astype(BF16).reshape(depth * N_EXPERTS, D_EXPERT, d)
    b_down_r = b_down.reshape(depth * N_EXPERTS, 1, d)
    w_branch_b = w_branch.astype(BF16).reshape(depth * N_BRANCH, BRANCH_WIDTH, d)
    w_out_b = w_out.astype(BF16)
    w_in_r = _reorder_w_in(w_in)

    for l in range(depth):
        mod3 = mod[l].reshape(batch, 1, 6 * d)
        nw3 = norm_w[l].reshape(4, 1, d)

        proj = _in_proj(x2, mod3, nw3, w_in_r, l, seq, nw_idx=0, sc_idx=1, sh_idx=0)
        small_bias = jnp.zeros((1, LANES), F32)
        small_bias = small_bias.at[0, SM_I:SM_I + BRANCH_HEADS].set(mlstm_i_b[l])
        small_bias = small_bias.at[0, SM_F:SM_F + BRANCH_HEADS].set(mlstm_f_b[l])
        small_bias = small_bias.at[0, SM_FF:SM_FF + BRANCH_HEADS].set(fox_f_b[l])
        w_a2_pad = jnp.zeros((LANES, GLA_HEADS * GLA_DK), F32).at[SM_GA:SM_GA + GLA_RANK].set(gla_w_a2[l])

        y_a = _moba(proj, cos, sin, batch, seq)
        y_b = _mlstm(proj, small_bias, conv_w[l], conv_b[l], mlstm_norm_w[l], batch, seq)
        y_c = _gla(proj, w_a2_pad, gla_b_a2[l], gla_norm_w[l], batch, seq)
        y_d = _fox(proj, small_bias, batch, seq)
        merged = _merge((y_a, y_b, y_c, y_d), proj, w_branch_b, l)
        x2 = _out_proj(merged, w_out_b, l, x2, nw3, mod3, seq, nw_idx=1, g_idx=2)

        rw_pad = jnp.zeros((d, LANES), F32).at[:, :N_EXPERTS].set(router_w[l])
        rb_pad = jnp.zeros((1, LANES), F32).at[0, :N_EXPERTS].set(router_b[l])
        h2, route, experts_t = _router(x2, nw3, mod3, rw_pad, rb_pad, seq, nw_idx=2, sc_idx=4, sh_idx=3)
        slot_tok, inv_slot, block_e, n_used = _routing_tables(experts_t, t)
        yb = _experts(h2, slot_tok, block_e, n_used, w_up_d, b_up_d, w_down_b, b_down_r, l)
        x2 = _combine(yb, inv_slot, route, x2, nw3, mod3, seq, nw_idx=3, g_idx=5)

    return x2.reshape(batch, seq, d)
```

```python
import functools

import jax
import jax.numpy as jnp
from jax import lax
from jax.experimental import pallas as pl
from jax.experimental.pallas import tpu as pltpu

F32 = jnp.float32
BF16 = jnp.bfloat16
NEG_INF = float("-inf")
MASK_PENALTY = -1e9

D_MODEL = 2048
HEAD_DIM = 128
N_BRANCH = 4
BRANCH_HEADS = 4
BRANCH_WIDTH = BRANCH_HEADS * HEAD_DIM

MOBA_BLOCK = 256
MOBA_TOPK = 3
ROPE_THETA = 10000.0

MLSTM_CHUNK = 128
MLSTM_CONV = 4

GLA_HEADS = 4
GLA_DK = 64
GLA_DV = 128
GLA_RANK = 16
GLA_TAU = 16.0
GLA_CHUNK = 64
GLA_SUB = 16

N_EXPERTS = 32
TOP_K = 4
D_EXPERT = D_MODEL // 2
SWIGLU_LIMIT = 7.0
SWIGLU_ALPHA = 1.702
MOE_ROW_BLOCK = 512

RMS_EPS = 1e-6

LANES = 128
MXU_DIM = 256
CHUNK = 128
ATT_TILE = 512

C_AQ, C_AK, C_AV = 0, 512, 1024
C_MQ, C_MK, C_MV, C_MO = 1536, 2048, 2560, 3072
C_GQ, C_GK, C_GV, C_GR = 3584, 3840, 4096, 4608
C_FQ, C_FK, C_FV = 5120, 5632, 6144
C_GATE = 6656
C_SMALL = 14848
SM_I, SM_F, SM_GA, SM_FF = 0, 4, 8, 24
N_PROJ = 15360

NN = (((1,), (0,)), ((), ()))
NT = (((1,), (1,)), ((), ()))
TN = (((0,), (0,)), ((), ()))

VMEM_LIMIT = 56 * 1024 * 1024


def _cp(sem, vmem=VMEM_LIMIT):
    return pltpu.CompilerParams(dimension_semantics=sem, vmem_limit_bytes=vmem)


def _dot(a, b, dn=NN):
    return lax.dot_general(a, b, dn, preferred_element_type=F32)


def _split3(x):
    hi = x.astype(BF16)
    r1 = x - hi.astype(F32)
    mid = r1.astype(BF16)
    lo = (r1 - mid.astype(F32)).astype(BF16)
    return hi, mid, lo


def _dot01(sel, x, dn=NN):
    hi, mid, lo = _split3(x)
    return _dot(sel, hi, dn) + _dot(sel, mid, dn) + _dot(sel, lo, dn)


def _dot_x3(a, b, dn=NN):
    ah = a.astype(BF16)
    al = (a - ah.astype(F32)).astype(BF16)
    bh = b.astype(BF16)
    bl = (b - bh.astype(F32)).astype(BF16)
    return _dot(ah, bh, dn) + _dot(al, bh, dn) + _dot(ah, bl, dn)


def _log_sigmoid(x):
    return jnp.minimum(x, 0.0) - jnp.log1p(jnp.exp(-jnp.abs(x)))


def _sigmoid(x):
    return 0.5 * jnp.tanh(0.5 * x) + 0.5


def _tril_bf16(n):
    r = lax.broadcasted_iota(jnp.int32, (n, n), 0)
    c = lax.broadcasted_iota(jnp.int32, (n, n), 1)
    return jnp.where(c <= r, 1.0, 0.0).astype(BF16)


def _row_selector(lane_idx):
    r = lax.broadcasted_iota(jnp.int32, (8, LANES), 0)
    c = lax.broadcasted_iota(jnp.int32, (8, LANES), 1)
    return jnp.where((r == 0) & (c == lane_idx), 1.0, 0.0).astype(BF16)


def _pick_lane(x, lane_idx):
    lane = lax.broadcasted_iota(jnp.int32, x.shape, 1)
    return jnp.sum(jnp.where(lane == lane_idx, x, 0.0), axis=-1, keepdims=True)


def _ada_kernel(c_ref, w_ref, b_ref, o_ref):
    c = c_ref[...]
    cond = c * _sigmoid(c)
    o_ref[0] = _dot(cond.astype(BF16), w_ref[0].astype(BF16)) + b_ref[0]


def _ada_mod(c, ada_w, ada_b):
    depth, d, n = ada_w.shape
    b = c.shape[0]
    tn = 1024
    return pl.pallas_call(
        _ada_kernel,
        grid=(depth, n // tn),
        in_specs=[
            pl.BlockSpec((b, d), lambda l, j: (0, 0)),
            pl.BlockSpec((1, d, tn), lambda l, j: (l, 0, j)),
            pl.BlockSpec((1, 1, tn), lambda l, j: (l, 0, j)),
        ],
        out_specs=pl.BlockSpec((1, b, tn), lambda l, j: (l, 0, j)),
        out_shape=jax.ShapeDtypeStruct((depth, b, n), F32),
        compiler_params=_cp(("parallel", "parallel")),
        name="ada_mod",
    )(c, ada_w, ada_b.reshape(depth, 1, n))


def _norm_mod_rows(x, nw, sc, sh):
    var = jnp.mean(x * x, axis=-1, keepdims=True)
    y = x * lax.rsqrt(var + RMS_EPS) * nw
    return y * (1.0 + sc) + sh


def _rms_rows(y, nw):
    var = jnp.mean(y * y, axis=-1, keepdims=True)
    return y * lax.rsqrt(var + RMS_EPS) * nw


def _inproj_kernel(x_ref, nw_ref, sc_ref, sh_ref, w_ref, o_ref, h_scr, *, rows):
    @pl.when(pl.program_id(1) == 0)
    def _():
        def body(r, carry):
            sl = pl.ds(pl.multiple_of(r * rows, rows), rows)
            h = _norm_mod_rows(x_ref[sl, :], nw_ref[0], sc_ref[0], sh_ref[0])
            h_scr[sl, :] = h.astype(BF16)
            return carry
        lax.fori_loop(0, x_ref.shape[0] // rows, body, 0)

    o_ref[...] = _dot(h_scr[...], w_ref[0])


def _in_proj(x2, mod3, nw3, w_re, layer, seq, *, nw_idx, sc_idx, sh_idx):
    t, d = x2.shape
    n = w_re.shape[2]
    tm = min(1024, seq)
    tn = 1024
    per_b = seq // tm
    return pl.pallas_call(
        functools.partial(_inproj_kernel, rows=min(256, tm)),
        grid=(t // tm, n // tn),
        in_specs=[
            pl.BlockSpec((tm, d), lambda i, j: (i, 0)),
            pl.BlockSpec((1, 1, d), lambda i, j: (nw_idx, 0, 0)),
            pl.BlockSpec((1, 1, d), lambda i, j: (i // per_b, 0, sc_idx)),
            pl.BlockSpec((1, 1, d), lambda i, j: (i // per_b, 0, sh_idx)),
            pl.BlockSpec((1, d, tn), lambda i, j: (layer, 0, j)),
        ],
        out_specs=pl.BlockSpec((tm, tn), lambda i, j: (i, j)),
        out_shape=jax.ShapeDtypeStruct((t, n), F32),
        scratch_shapes=[pltpu.VMEM((tm, d), BF16)],
        compiler_params=_cp(("parallel", "arbitrary")),
        name="in_proj",
    )(x2, nw3, mod3, mod3, w_re)


ATT_HEADS = 2


def _attention_sweep(i, tile, q_augs, kaug_scr, vb_scr, o_ref):
    nh = len(q_augs)
    rows_q = pl.ds(pl.multiple_of(i * tile, tile), tile)
    r_io = lax.broadcasted_iota(jnp.int32, (tile, tile), 0)
    c_io = lax.broadcasted_iota(jnp.int32, (tile, tile), 1)
    init = []
    for hh in range(nh):
        s = _dot(q_augs[hh], kaug_scr[hh, rows_q, :], NT)
        s = jnp.where(c_io <= r_io, s, NEG_INF)
        m0 = jnp.max(s, axis=-1, keepdims=True)
        p = jnp.exp(s - m0)
        init += [m0, jnp.sum(p, axis=-1, keepdims=True), _dot(p.astype(BF16), vb_scr[hh, rows_q, :])]

    def body(j, carry):
        rows = pl.ds(pl.multiple_of(j * tile, tile), tile)
        out = []
        for hh in range(nh):
            m, l, acc = carry[3 * hh:3 * hh + 3]
            s = _dot(q_augs[hh], kaug_scr[hh, rows, :], NT)
            m_new = jnp.maximum(m, jnp.max(s, axis=-1, keepdims=True))
            alpha = jnp.exp(m - m_new)
            p = jnp.exp(s - m_new)
            l = alpha * l + jnp.sum(p, axis=-1, keepdims=True)
            acc = alpha * acc + _dot(p.astype(BF16), vb_scr[hh, rows, :])
            out += [m_new, l, acc]
        return tuple(out)

    fin = lax.fori_loop(0, i, body, tuple(init))
    for hh in range(nh):
        o_ref[:, hh * HEAD_DIM:(hh + 1) * HEAD_DIM] = (fin[3 * hh + 2] / fin[3 * hh + 1]).astype(o_ref.dtype)


def _rope(x, cos, sin):
    return x * cos + pltpu.roll(x, HEAD_DIM // 2, 1) * sin


def _moba_kernel(q_ref, k_ref, v_ref, cos_ref, sin_ref, o_ref, kaug_scr, vb_scr, km_scr, *, seq, tile, n_sel):
    blk = MOBA_BLOCK
    nblk = seq // blk
    nb8 = -(-nblk // 8) * 8
    i = pl.program_id(2)

    @pl.when(i == 0)
    def _():
        km_scr[...] = jnp.zeros_like(km_scr)
        lane_b = lax.broadcasted_iota(jnp.int32, (blk, LANES), 1)
        for hh in range(ATT_HEADS):
            cols = slice(hh * HEAD_DIM, (hh + 1) * HEAD_DIM)
            for j in range(nblk):
                rows = pl.ds(j * blk, blk)
                kk = _rope(k_ref[rows, cols], cos_ref[rows, :], sin_ref[rows, :])
                kaug_scr[hh, rows, 0:HEAD_DIM] = kk.astype(BF16)
                kaug_scr[hh, rows, HEAD_DIM:2 * HEAD_DIM] = jnp.where(lane_b == j, 1.0, 0.0).astype(BF16)
                km_scr[hh, j:j + 1, :] = jnp.mean(kk, axis=0, keepdims=True)
                vb_scr[hh, rows, :] = v_ref[rows, cols].astype(BF16)

    rows_q = pl.ds(pl.multiple_of(i * tile, tile), tile)
    blk_id = lax.broadcasted_iota(jnp.int32, (nb8, tile), 0)
    blk_f = blk_id.astype(F32)
    q_blk = i * (tile // blk) + lax.broadcasted_iota(jnp.int32, (nb8, tile), 1) // blk
    q_augs = []
    for hh in range(ATT_HEADS):
        cols = slice(hh * HEAD_DIM, (hh + 1) * HEAD_DIM)
        q = _rope(q_ref[:, cols], cos_ref[rows_q, :], sin_ref[rows_q, :])
        g = _dot_x3(km_scr[hh], q, NT)[0:nb8, :]
        g = jnp.where(blk_id < q_blk, g, NEG_INF)
        keep = blk_id == q_blk
        for _ in range(n_sel):
            m = jnp.max(g, axis=0, keepdims=True)
            idx = jnp.min(jnp.where(g == m, blk_f, float(LANES)), axis=0, keepdims=True)
            hit = blk_f == idx
            keep = keep | (hit & (m > NEG_INF))
            g = jnp.where(hit, NEG_INF, g)
        pen_t = jnp.where(keep, 0.0, MASK_PENALTY)
        pen_t = jnp.concatenate([pen_t, jnp.full((LANES - nb8, tile), MASK_PENALTY, F32)], axis=0)
        penalty = pen_t.T
        q_augs.append(jnp.concatenate([(q * (HEAD_DIM ** -0.5)).astype(BF16), penalty.astype(BF16)], axis=1))
    _attention_sweep(i, tile, q_augs, kaug_scr, vb_scr, o_ref)


def _moba(proj, cos, sin, batch, seq):
    tile = min(ATT_TILE, seq)
    nq = seq // tile
    nblk = seq // MOBA_BLOCK
    n_sel = max(1, min(MOBA_TOPK, nblk - 1))
    hb = HEAD_DIM
    hw = ATT_HEADS * hb
    return pl.pallas_call(
        functools.partial(_moba_kernel, seq=seq, tile=tile, n_sel=n_sel),
        grid=(batch, BRANCH_HEADS // ATT_HEADS, nq),
        in_specs=[
            pl.BlockSpec((tile, hw), lambda b, p, i: (b * nq + i, C_AQ // hw + p)),
            pl.BlockSpec((seq, hw), lambda b, p, i: (b, C_AK // hw + p)),
            pl.BlockSpec((seq, hw), lambda b, p, i: (b, C_AV // hw + p)),
            pl.BlockSpec((seq, hb), lambda b, p, i: (0, 0)),
            pl.BlockSpec((seq, hb), lambda b, p, i: (0, 0)),
        ],
        out_specs=pl.BlockSpec((tile, hw), lambda b, p, i: (b * nq + i, p)),
        out_shape=jax.ShapeDtypeStruct((batch * seq, BRANCH_WIDTH), BF16),
        scratch_shapes=[pltpu.VMEM((ATT_HEADS, seq, 2 * hb), BF16), pltpu.VMEM((ATT_HEADS, seq, hb), BF16),
                        pltpu.VMEM((ATT_HEADS, LANES, hb), F32)],
        compiler_params=_cp(("parallel", "parallel", "arbitrary")),
        name="moba",
    )(proj, proj, proj, cos, sin)


def _fox_kernel(q_ref, k_ref, v_ref, sm_ref, bias_ref, o_ref, kaug_scr, vb_scr, fcol_scr, *, seq, tile):
    p_idx = pl.program_id(1)
    i = pl.program_id(2)

    def bias_lanes(f, first):
        hi, mid, lo = _split3(f)
        lane = lax.broadcasted_iota(jnp.int32, (f.shape[0], LANES), 1)
        ones_first = 3 - first
        out = jnp.where((lane >= ones_first) & (lane < ones_first + 3), 1.0, 0.0)
        out = jnp.where(lane == first, hi.astype(F32), out)
        out = jnp.where(lane == first + 1, mid.astype(F32), out)
        out = jnp.where(lane == first + 2, lo.astype(F32), out)
        return out.astype(BF16)

    @pl.when(i == 0)
    def _():
        tril = _tril_bf16(CHUNK)
        carry = jnp.zeros((1, LANES), F32)
        for c in range(seq // CHUNK):
            rows = pl.ds(c * CHUNK, CHUNK)
            lf = _log_sigmoid(sm_ref[rows, :] + bias_ref[...])
            cs = _dot01(tril, lf) + carry
            fcol_scr[rows, :] = cs
            carry = cs[CHUNK - 1:CHUNK, :]
        for hh in range(ATT_HEADS):
            cols = slice(hh * HEAD_DIM, (hh + 1) * HEAD_DIM)
            f_lane = SM_FF + p_idx * ATT_HEADS + hh
            for c in range(seq // tile):
                rows = pl.ds(c * tile, tile)
                kaug_scr[hh, rows, 0:HEAD_DIM] = k_ref[rows, cols].astype(BF16)
                kaug_scr[hh, rows, HEAD_DIM:2 * HEAD_DIM] = bias_lanes(-_pick_lane(fcol_scr[rows, :], f_lane), 3)
                vb_scr[hh, rows, :] = v_ref[rows, cols].astype(BF16)

    rows_q = pl.ds(pl.multiple_of(i * tile, tile), tile)
    fq = fcol_scr[rows_q, :]
    q_augs = []
    for hh in range(ATT_HEADS):
        cols = slice(hh * HEAD_DIM, (hh + 1) * HEAD_DIM)
        f_t = _pick_lane(fq, SM_FF + p_idx * ATT_HEADS + hh)
        q_augs.append(jnp.concatenate([(q_ref[:, cols] * (HEAD_DIM ** -0.5)).astype(BF16), bias_lanes(f_t, 0)], axis=1))
    _attention_sweep(i, tile, q_augs, kaug_scr, vb_scr, o_ref)


def _fox(proj, small_bias, batch, seq):
    tile = min(ATT_TILE, seq)
    nq = seq // tile
    hb = HEAD_DIM
    hw = ATT_HEADS * hb
    return pl.pallas_call(
        functools.partial(_fox_kernel, seq=seq, tile=tile),
        grid=(batch, BRANCH_HEADS // ATT_HEADS, nq),
        in_specs=[
            pl.BlockSpec((tile, hw), lambda b, p, i: (b * nq + i, C_FQ // hw + p)),
            pl.BlockSpec((seq, hw), lambda b, p, i: (b, C_FK // hw + p)),
            pl.BlockSpec((seq, hw), lambda b, p, i: (b, C_FV // hw + p)),
            pl.BlockSpec((seq, LANES), lambda b, p, i: (b, C_SMALL // LANES)),
            pl.BlockSpec((1, LANES), lambda b, p, i: (0, 0)),
        ],
        out_specs=pl.BlockSpec((tile, hw), lambda b, p, i: (b * nq + i, p)),
        out_shape=jax.ShapeDtypeStruct((batch * seq, BRANCH_WIDTH), BF16),
        scratch_shapes=[pltpu.VMEM((ATT_HEADS, seq, 2 * hb), BF16), pltpu.VMEM((ATT_HEADS, seq, hb), BF16),
                        pltpu.VMEM((seq, LANES), F32)],
        compiler_params=_cp(("parallel", "parallel", "arbitrary")),
        name="fox",
    )(proj, proj, proj, proj, small_bias)


def _conv_silu_rows(x_ref, w_ref, b_ref, r0, n):
    k = w_ref.shape[0]
    x0 = x_ref[r0:r0 + n, :]
    y = x0 * w_ref[k - 1:k, :] + b_ref[...]
    for sft in range(1, k):
        if r0 >= sft:
            xs = x_ref[r0 - sft:r0 - sft + n, :]
        else:
            row = lax.broadcasted_iota(jnp.int32, x0.shape, 0)
            xs = jnp.where(row >= sft, pltpu.roll(x0, sft, 0), 0.0)
        y = y + xs * w_ref[k - 1 - sft:k - sft, :]
    return y * _sigmoid(y)


MLSTM_UNROLL = 16


def _mlstm_kernel(q_ref, k_ref, v_ref, o_ref, sm_ref, bias_ref, cwq_ref, cwk_ref, cbq_ref, cbk_ref,
                  nw_ref, y_ref, q_scr, k_scr, x_scr, g_scr, irow_scr, grow_scr, *, seq):
    h = pl.program_id(1)
    L = MLSTM_CHUNK
    nc = seq // L
    unroll = MLSTM_UNROLL if nc % MLSTM_UNROLL == 0 else (nc if nc < MLSTM_UNROLL else 1)
    i_lane = SM_I + h
    f_lane = SM_F + h

    tril = _tril_bf16(L)
    sel_i = _row_selector(i_lane)
    sel_f = _row_selector(f_lane)
    lane_s = lax.broadcasted_iota(jnp.int32, (L, LANES), 1)
    for c in range(nc):
        r0 = c * L
        rows = pl.ds(r0, L)
        q_scr[rows, :] = _conv_silu_rows(q_ref, cwq_ref, cbq_ref, r0, L)
        k_scr[rows, :] = _conv_silu_rows(k_ref, cwk_ref, cbk_ref, r0, L) * (HEAD_DIM ** -0.5)
        smb = sm_ref[rows, :] + bias_ref[...]
        xc = jnp.where(lane_s < SM_F, smb, _log_sigmoid(smb))
        gc = _dot01(tril, xc)
        x_scr[rows, :] = xc
        g_scr[rows, :] = gc
        irow_scr[:, r0:r0 + L] = _dot01(sel_i, xc, NT)
        grow_scr[:, r0:r0 + L] = _dot01(sel_f, gc, NT)

    r_io = lax.broadcasted_iota(jnp.int32, (L, L), 0)
    c_io = lax.broadcasted_iota(jnp.int32, (L, L), 1)
    causal = c_io <= r_io

    def chunk(c, carry):
        C, n, m = carry
        rows = pl.ds(pl.multiple_of(c * L, L), L)
        qc = q_scr[rows, :]
        kc = k_scr[rows, :]
        vb = v_ref[rows, :].astype(BF16)
        qb = qc.astype(BF16)
        g_col = _pick_lane(g_scr[rows, :], f_lane)
        i_col = _pick_lane(x_scr[rows, :], i_lane)
        g_row = grow_scr[0:1, rows]
        i_row = irow_scr[0:1, rows]
        dmat = jnp.where(causal, g_col - g_row + i_row, NEG_INF)
        a = g_col + m
        m_out = jnp.maximum(a, jnp.max(dmat, axis=-1, keepdims=True))
        w = jnp.exp(dmat - m_out) * _dot(qb, kc.astype(BF16), NT)
        inter = jnp.exp(a - m_out)
        num = inter * _dot(qb, C.astype(BF16)) + _dot(w.astype(BF16), vb)
        den = inter * jnp.sum(qc * n, axis=-1, keepdims=True) + jnp.sum(w, axis=-1, keepdims=True)
        hh = num / jnp.maximum(jnp.abs(den), jnp.exp(-m_out))
        g_last = g_row[:, L - 1:L]
        lw = g_last - g_col + i_col
        m_new = jnp.maximum(g_last + m, jnp.max(lw, axis=0, keepdims=True))
        wk = jnp.exp(lw - m_new)
        decay = jnp.exp(g_last + m - m_new)
        kw = kc * wk
        C = decay * C + _dot(kw.astype(BF16), vb, TN)
        n = decay * n + jnp.sum(kw, axis=0, keepdims=True)
        yn = _rms_rows(hh, nw_ref[0])
        y_ref[rows, :] = (yn * _sigmoid(o_ref[rows, :])).astype(y_ref.dtype)
        return C, n, m_new

    def step(it, carry):
        for u in range(unroll):
            carry = chunk(it * unroll + u, carry)
        return carry

    init = (jnp.zeros((HEAD_DIM, HEAD_DIM), F32), jnp.zeros((1, HEAD_DIM), F32), jnp.zeros((1, 1), F32))
    lax.fori_loop(0, nc // unroll, step, init)


def _mlstm(proj, small_bias, conv_w, conv_b, norm_w, batch, seq):
    hb = HEAD_DIM
    nh = BRANCH_HEADS
    blk = lambda c0: pl.BlockSpec((seq, hb), lambda b, h: (b, c0 // hb + h))
    return pl.pallas_call(
        functools.partial(_mlstm_kernel, seq=seq),
        grid=(batch, nh),
        in_specs=[
            blk(C_MQ), blk(C_MK), blk(C_MV), blk(C_MO),
            pl.BlockSpec((seq, LANES), lambda b, h: (b, C_SMALL // LANES)),
            pl.BlockSpec((1, LANES), lambda b, h: (0, 0)),
            pl.BlockSpec((MLSTM_CONV, hb), lambda b, h: (0, h)),
            pl.BlockSpec((MLSTM_CONV, hb), lambda b, h: (0, nh + h)),
            pl.BlockSpec((1, hb), lambda b, h: (0, h)),
            pl.BlockSpec((1, hb), lambda b, h: (0, nh + h)),
            pl.BlockSpec((1, 1, hb), lambda b, h: (h, 0, 0)),
        ],
        out_specs=pl.BlockSpec((seq, hb), lambda b, h: (b, h)),
        out_shape=jax.ShapeDtypeStruct((batch * seq, BRANCH_WIDTH), BF16),
        scratch_shapes=[
            pltpu.VMEM((seq, hb), F32), pltpu.VMEM((seq, hb), F32),
            pltpu.VMEM((seq, LANES), F32), pltpu.VMEM((seq, LANES), F32),
            pltpu.VMEM((8, seq), F32), pltpu.VMEM((8, seq), F32),
        ],
        compiler_params=_cp(("parallel", "parallel")),
        name="mlstm",
    )(proj, proj, proj, proj, proj, small_bias, conv_w, conv_w, conv_b.reshape(1, -1), conv_b.reshape(1, -1),
      norm_w.reshape(nh, 1, hb))


GLA_UNROLL = 8


def _gla_kernel(q_ref, k_ref, v_ref, r_ref, sm_ref, wa_ref, ba_ref, nw_ref, y_ref, la_scr, *, seq):
    L = GLA_CHUNK
    sub = GLA_SUB
    nc = seq // L
    unroll = GLA_UNROLL if nc % GLA_UNROLL == 0 else (nc if nc < GLA_UNROLL else 1)
    dk, dv = GLA_DK, GLA_DV

    for c in range(seq // CHUNK):
        rows = pl.ds(c * CHUNK, CHUNK)
        z = _dot_x3(sm_ref[rows, :], wa_ref[...]) + ba_ref[...]
        la_scr[rows, :] = _log_sigmoid(z) * (1.0 / GLA_TAU)
    tril = _tril_bf16(L)
    r_io = lax.broadcasted_iota(jnp.int32, (sub, L), 0)
    c_io = lax.broadcasted_iota(jnp.int32, (sub, L), 1)
    row_l = lax.broadcasted_iota(jnp.int32, (L, dk), 0)

    def chunk(c, states):
        rows = pl.ds(pl.multiple_of(c * L, L), L)
        bc2 = _dot01(tril, la_scr[rows, :])
        q2 = q_ref[rows, :] * (dk ** -0.5)
        k2 = k_ref[rows, :]
        new_states = []
        for hh in range(2):
            st = states[hh]
            lanes = slice(hh * dk, (hh + 1) * dk)
            b = bc2[:, lanes]
            qh = q2[:, lanes]
            kh = k2[:, lanes]
            vb = v_ref[rows, hh * dv:(hh + 1) * dv].astype(BF16)
            o = _dot((qh * jnp.exp(b)).astype(BF16), st.astype(BF16), NT)
            blocks = []
            for ib in range(L // sub):
                ref_row = b[ib * sub:ib * sub + 1, :]
                qi = qh[ib * sub:(ib + 1) * sub, :] * jnp.exp(b[ib * sub:(ib + 1) * sub, :] - ref_row)
                ki = kh * jnp.exp(jnp.where(row_l < (ib + 1) * sub, ref_row - b, 0.0))
                a = _dot(qi.astype(BF16), ki.astype(BF16), NT)
                blocks.append(jnp.where(c_io <= r_io + ib * sub, a, 0.0))
            attn = jnp.concatenate(blocks, axis=0)
            o = o + _dot(attn.astype(BF16), vb)
            b_last = b[L - 1:L, :]
            ke = kh * jnp.exp(b_last - b)
            st = jnp.exp(b_last) * st + _dot(vb, ke.astype(BF16), TN)
            new_states.append(st)
            yn = _rms_rows(o, nw_ref[0, hh:hh + 1, :])
            r = r_ref[rows, hh * dv:(hh + 1) * dv]
            y_ref[rows, hh * dv:(hh + 1) * dv] = (yn * (r * _sigmoid(r))).astype(y_ref.dtype)
        return tuple(new_states)

    def step(it, states):
        for u in range(unroll):
            states = chunk(it * unroll + u, states)
        return states

    init = (jnp.zeros((dv, dk), F32), jnp.zeros((dv, dk), F32))
    lax.fori_loop(0, nc // unroll, step, init)


def _gla(proj, w_a2_pad, b_a2, norm_w, batch, seq):
    return pl.pallas_call(
        functools.partial(_gla_kernel, seq=seq),
        grid=(batch, GLA_HEADS // 2),
        in_specs=[
            pl.BlockSpec((seq, LANES), lambda b, p: (b, C_GQ // LANES + p)),
            pl.BlockSpec((seq, LANES), lambda b, p: (b, C_GK // LANES + p)),
            pl.BlockSpec((seq, 2 * GLA_DV), lambda b, p: (b, C_GV // (2 * GLA_DV) + p)),
            pl.BlockSpec((seq, 2 * GLA_DV), lambda b, p: (b, C_GR // (2 * GLA_DV) + p)),
            pl.BlockSpec((seq, LANES), lambda b, p: (b, C_SMALL // LANES)),
            pl.BlockSpec((LANES, LANES), lambda b, p: (0, p)),
            pl.BlockSpec((1, LANES), lambda b, p: (0, p)),
            pl.BlockSpec((1, 2, GLA_DV), lambda b, p: (p, 0, 0)),
        ],
        out_specs=pl.BlockSpec((seq, 2 * GLA_DV), lambda b, p: (b, p)),
        out_shape=jax.ShapeDtypeStruct((batch * seq, GLA_HEADS * GLA_DV), BF16),
        scratch_shapes=[pltpu.VMEM((seq, LANES), F32)],
        compiler_params=_cp(("parallel", "parallel")),
        name="gla",
    )(proj, proj, proj, proj, proj, w_a2_pad, b_a2.reshape(1, -1), norm_w.reshape(GLA_HEADS // 2, 2, GLA_DV))


def _merge_kernel(ya_ref, yb_ref, yc_ref, yd_ref, g0_ref, g1_ref, g2_ref, g3_ref, w_ref, o_ref):
    ys = (ya_ref, yb_ref, yc_ref, yd_ref)
    gs = (g0_ref, g1_ref, g2_ref, g3_ref)
    acc = None
    for b in range(N_BRANCH):
        term = _sigmoid(gs[b][...]) * _dot(ys[b][...], w_ref[b])
        acc = term if acc is None else acc + term
    o_ref[...] = acc.astype(o_ref.dtype)


def _merge(ys, proj, w_branch_bf16, layer):
    t = proj.shape[0]
    d = D_MODEL
    tm, tn = 1024, 512
    tm = min(tm, t)
    yspec = pl.BlockSpec((tm, BRANCH_WIDTH), lambda i, j: (i, 0))
    gspec = lambda b: pl.BlockSpec((tm, tn), lambda i, j: (i, (C_GATE + b * d) // tn + j))
    return pl.pallas_call(
        _merge_kernel,
        grid=(t // tm, d // tn),
        in_specs=[yspec] * 4 + [gspec(b) for b in range(N_BRANCH)]
        + [pl.BlockSpec((N_BRANCH, BRANCH_WIDTH, tn), lambda i, j: (layer, 0, j))],
        out_specs=pl.BlockSpec((tm, tn), lambda i, j: (i, j)),
        out_shape=jax.ShapeDtypeStruct((t, d), BF16),
        compiler_params=_cp(("parallel", "parallel")),
        name="merge",
    )(*ys, proj, proj, proj, proj, w_branch_bf16)


def _outproj_kernel(m_ref, w_ref, x_ref, nw_ref, g_ref, o_ref):
    y = _dot(m_ref[...], w_ref[0])
    o_ref[...] = x_ref[...] + g_ref[0] * _rms_rows(y, nw_ref[0])


def _out_proj(merged, w_out_bf16, layer, x2, nw3, mod3, seq, *, nw_idx, g_idx):
    t, d = x2.shape
    tm = min(512, seq)
    per_b = seq // tm
    return pl.pallas_call(
        _outproj_kernel,
        grid=(t // tm,),
        in_specs=[
            pl.BlockSpec((tm, d), lambda i: (i, 0)),
            pl.BlockSpec((1, d, d), lambda i: (layer, 0, 0)),
            pl.BlockSpec((tm, d), lambda i: (i, 0)),
            pl.BlockSpec((1, 1, d), lambda i: (nw_idx, 0, 0)),
            pl.BlockSpec((1, 1, d), lambda i: (i // per_b, 0, g_idx)),
        ],
        out_specs=pl.BlockSpec((tm, d), lambda i: (i, 0)),
        out_shape=jax.ShapeDtypeStruct((t, d), F32),
        compiler_params=_cp(("parallel",)),
        name="out_proj",
    )(merged, w_out_bf16, x2, nw3, mod3)


def _router_kernel(x_ref, nw_ref, sc_ref, sh_ref, rw_ref, rb_ref, h_ref, r_ref, et_ref):
    h = _norm_mod_rows(x_ref[...], nw_ref[0], sc_ref[0], sh_ref[0])
    h_ref[...] = h
    logits = _dot_x3(h, rw_ref[...]) + rb_ref[...]
    lane = lax.broadcasted_iota(jnp.int32, logits.shape, 1)
    lane_f = lane.astype(F32)
    l = jnp.where(lane < N_EXPERTS, logits, NEG_INF)
    vals, idxs = [], []
    for _ in range(TOP_K):
        m = jnp.max(l, axis=-1, keepdims=True)
        idx = jnp.min(jnp.where(l == m, lane_f, float(LANES)), axis=-1, keepdims=True)
        vals.append(m)
        idxs.append(idx)
        l = jnp.where(lane_f == idx, NEG_INF, l)
    es = [jnp.exp(v - vals[0]) for v in vals]
    tot = es[0]
    for e in es[1:]:
        tot = tot + e
    out = jnp.zeros(logits.shape, F32)
    for r in range(TOP_K):
        out = jnp.where(lane == r, es[r] / tot, out)
        out = jnp.where(lane == TOP_K + r, idxs[r], out)
    r_ref[...] = out
    et_ref[...] = out.T[TOP_K:TOP_K + 8, :].astype(jnp.int32)


def _router(x2, nw3, mod3, router_w_pad, router_b_pad, seq, *, nw_idx, sc_idx, sh_idx):
    t, d = x2.shape
    tm = min(256, seq)
    per_b = seq // tm
    return pl.pallas_call(
        _router_kernel,
        grid=(t // tm,),
        in_specs=[
            pl.BlockSpec((tm, d), lambda i: (i, 0)),
            pl.BlockSpec((1, 1, d), lambda i: (nw_idx, 0, 0)),
            pl.BlockSpec((1, 1, d), lambda i: (i // per_b, 0, sc_idx)),
            pl.BlockSpec((1, 1, d), lambda i: (i // per_b, 0, sh_idx)),
            pl.BlockSpec((d, LANES), lambda i: (0, 0)),
            pl.BlockSpec((1, LANES), lambda i: (0, 0)),
        ],
        out_specs=[pl.BlockSpec((tm, d), lambda i: (i, 0)), pl.BlockSpec((tm, LANES), lambda i: (i, 0)),
                   pl.BlockSpec((8, tm), lambda i: (0, i))],
        out_shape=[jax.ShapeDtypeStruct((t, d), F32), jax.ShapeDtypeStruct((t, LANES), F32),
                   jax.ShapeDtypeStruct((8, t), jnp.int32)],
        compiler_params=_cp(("parallel",)),
        name="router",
    )(x2, nw3, mod3, mod3, router_w_pad, router_b_pad)


def _deinterleave_kernel(w_ref, o_ref):
    n = MXU_DIM
    r = lax.broadcasted_iota(jnp.int32, (n, n), 0)
    c = lax.broadcasted_iota(jnp.int32, (n, n), 1)
    src = jnp.where(c < n // 2, 2 * c, 2 * (c - n // 2) + 1)
    perm = jnp.where(r == src, 1.0, 0.0).astype(BF16)
    for m in range(w_ref.shape[1] // n):
        cols = slice(m * n, (m + 1) * n)
        o_ref[:, cols] = _dot(w_ref[:, cols].astype(BF16), perm).astype(BF16)


def _deinterleave_cast(w2):
    rows, n = w2.shape
    tr = 1024
    return pl.pallas_call(
        _deinterleave_kernel,
        grid=(rows // tr,),
        in_specs=[pl.BlockSpec((tr, n), lambda i: (i, 0))],
        out_specs=pl.BlockSpec((tr, n), lambda i: (i, 0)),
        out_shape=jax.ShapeDtypeStruct((rows, n), BF16),
        compiler_params=_cp(("parallel",)),
        name="deinterleave_cast",
    )(w2)


def _expert_kernel(be_ref, nused_ref, idx_hbm, h_hbm, wu_ref, bu_ref, wd_ref, bd_ref, o_ref,
                   xbuf, idx_smem, isems, rsems, *, rows, nb):
    b = pl.program_id(0)
    slot = b % 2
    nxt = 1 - slot
    n_used = nused_ref[0]

    def idx_copy(blk, s):
        return pltpu.make_async_copy(idx_hbm.at[blk], idx_smem.at[s], isems.at[s])

    def row_copy(tok, r, s):
        return pltpu.make_async_copy(h_hbm.at[pl.ds(tok, 1), :], xbuf.at[s, pl.ds(r, 1), :], rsems.at[s])

    def drain_rows(s):
        pltpu.make_async_copy(h_hbm.at[pl.ds(0, rows), :], xbuf.at[s], rsems.at[s]).wait()

    @pl.when(b == 0)
    def _():
        idx_copy(0, 0).start()
        idx_copy(0, 0).wait()

        def issue(r, carry):
            row_copy(idx_smem[0, r], r, 0).start()
            return carry
        lax.fori_loop(0, rows, issue, 0, unroll=8)
        idx_copy(1, 1).start()

    @pl.when(b < n_used)
    def _():
        idx_copy(0, nxt).wait()
        drain_rows(slot)
        issued = [0]

        def issue_next(n):
            for r in range(issued[0], issued[0] + n):
                row_copy(idx_smem[nxt, r], r, nxt).start(priority=r % 2)
            issued[0] += n

        x = xbuf[slot].astype(BF16)
        half = MXU_DIM // 2
        n_up = wu_ref.shape[2] // MXU_DIM
        n_down = 4
        per_up = (rows // (n_up + n_down) + 7) // 8 * 8
        acts = []
        for m in range(n_up):
            issue_next(per_up)
            cols = slice(m * MXU_DIM, (m + 1) * MXU_DIM)
            hu = _dot(x, wu_ref[0, :, cols]) + bu_ref[0, :, cols]
            g = jnp.minimum(hu[:, :half], SWIGLU_LIMIT)
            lin = jnp.clip(hu[:, half:], -SWIGLU_LIMIT, SWIGLU_LIMIT)
            acts.append((g * _sigmoid(SWIGLU_ALPHA * g) * (lin + 1.0)).astype(BF16))
        act = jnp.concatenate(acts, axis=1)
        dn = o_ref.shape[1] // n_down
        per_down = (rows - n_up * per_up) // n_down
        for m in range(n_down):
            issue_next(per_down if m < n_down - 1 else rows - issued[0])
            cols = slice(m * dn, (m + 1) * dn)
            o_ref[:, cols] = _dot(act, wd_ref[0, :, cols]) + bd_ref[0, :, cols]
        idx_copy(b + 2, slot).start()

    @pl.when(b >= n_used)
    def _():
        o_ref[...] = jnp.zeros_like(o_ref)

    @pl.when(b == n_used)
    def _():
        idx_copy(0, nxt).wait()
        drain_rows(slot)

    @pl.when((b == nb - 1) & (b < n_used))
    def _():
        idx_copy(0, slot).wait()
        drain_rows(nxt)


def _experts(h2, slot_tok, block_e, n_used, wu, bu, wd, bd, layer):
    d = h2.shape[1]
    rows = MOE_ROW_BLOCK
    nb = slot_tok.shape[0] // rows
    de = D_EXPERT
    idx = jnp.concatenate([slot_tok, jnp.zeros((2 * rows,), jnp.int32)]).reshape(nb + 2, rows)

    def wmap(b, be, nu):
        return (layer * N_EXPERTS + be[b], 0, 0)

    grid_spec = pltpu.PrefetchScalarGridSpec(
        num_scalar_prefetch=2,
        grid=(nb,),
        in_specs=[
            pl.BlockSpec(memory_space=pl.ANY),
            pl.BlockSpec(memory_space=pl.ANY),
            pl.BlockSpec((1, d, 2 * de), wmap),
            pl.BlockSpec((1, 1, 2 * de), wmap),
            pl.BlockSpec((1, de, d), wmap),
            pl.BlockSpec((1, 1, d), wmap),
        ],
        out_specs=pl.BlockSpec((rows, d), lambda b, be, nu: (b, 0)),
        scratch_shapes=[pltpu.VMEM((2, rows, d), F32), pltpu.SMEM((2, rows), jnp.int32),
                        pltpu.SemaphoreType.DMA((2,)), pltpu.SemaphoreType.DMA((2,))],
    )
    return pl.pallas_call(
        functools.partial(_expert_kernel, rows=rows, nb=nb),
        grid_spec=grid_spec,
        out_shape=jax.ShapeDtypeStruct((nb * rows, d), F32),
        compiler_params=_cp(("arbitrary",)),
        name="experts",
    )(block_e, n_used, idx, h2, wu, bu, wd, bd)


def _combine_kernel(idx_hbm, yb_hbm, r_ref, x_ref, nw_ref, g_ref, o_ref, ybuf, idx_smem, isems, rsems, *, tm, nt):
    i = pl.program_id(0)
    n_rows = TOP_K * tm

    def idx_copy(tile, s):
        return pltpu.make_async_copy(idx_hbm.at[tile], idx_smem.at[s], isems.at[s])

    def issue_rows(s):
        def body(r2, carry):
            for u in range(2):
                r = 2 * r2 + u
                pltpu.make_async_copy(yb_hbm.at[pl.ds(idx_smem[s, r], 1), :], ybuf.at[s, pl.ds(r, 1), :],
                                      rsems.at[s]).start(priority=u)
            return carry
        lax.fori_loop(0, n_rows // 2, body, 0, unroll=4)

    def wait_rows(s):
        pltpu.make_async_copy(yb_hbm.at[pl.ds(0, n_rows), :], ybuf.at[s], rsems.at[s]).wait()

    def tile(slot):
        nxt = 1 - slot

        @pl.when(i + 1 < nt)
        def _():
            idx_copy(0, nxt).wait()
            issue_rows(nxt)

        @pl.when(i + 2 < nt)
        def _():
            idx_copy(i + 2, slot).start()

        wait_rows(slot)
        gates = r_ref[...]
        y = None
        for k in range(TOP_K):
            term = gates[:, k:k + 1] * ybuf[slot, k * tm:(k + 1) * tm, :]
            y = term if y is None else y + term
        o_ref[...] = x_ref[...] + g_ref[0] * _rms_rows(y, nw_ref[0])

    @pl.when(i == 0)
    def _():
        idx_copy(0, 0).start()
        idx_copy(0, 0).wait()
        issue_rows(0)

        @pl.when(nt > 1)
        def _():
            idx_copy(1, 1).start()

    @pl.when(i % 2 == 0)
    def _():
        tile(0)

    @pl.when(i % 2 == 1)
    def _():
        tile(1)


def _combine(yb, inv_slot, route, x2, nw3, mod3, seq, *, nw_idx, g_idx):
    t, d = x2.shape
    tm = min(256, seq)
    nt = t // tm
    per_b = seq // tm
    idx_tiles = inv_slot.reshape(TOP_K, nt, tm).transpose(1, 0, 2).reshape(nt, TOP_K * tm)
    return pl.pallas_call(
        functools.partial(_combine_kernel, tm=tm, nt=nt),
        grid=(nt,),
        in_specs=[
            pl.BlockSpec(memory_space=pl.ANY),
            pl.BlockSpec(memory_space=pl.ANY),
            pl.BlockSpec((tm, LANES), lambda i: (i, 0)),
            pl.BlockSpec((tm, d), lambda i: (i, 0)),
            pl.BlockSpec((1, 1, d), lambda i: (nw_idx, 0, 0)),
            pl.BlockSpec((1, 1, d), lambda i: (i // per_b, 0, g_idx)),
        ],
        out_specs=pl.BlockSpec((tm, d), lambda i: (i, 0)),
        out_shape=jax.ShapeDtypeStruct((t, d), F32),
        scratch_shapes=[pltpu.VMEM((2, TOP_K * tm, d), F32), pltpu.SMEM((2, TOP_K * tm), jnp.int32),
                        pltpu.SemaphoreType.DMA((2,)), pltpu.SemaphoreType.DMA((2,))],
        compiler_params=_cp(("arbitrary",)),
        name="combine",
    )(idx_tiles, yb, route, x2, nw3, mod3)


_W_IN_SEGMENTS = (
    (0, C_AQ, 3072),
    (3080, C_MO, 512),
    (3592, C_GQ, 1024),
    (4632, C_GR, 512),
    (5144, C_FQ, 1536),
    (6684, C_GATE, 8192),
)
_W_IN_SMALL = ((3072, SM_I, 8), (4616, SM_GA, GLA_RANK), (6680, SM_FF, BRANCH_HEADS))


def _reorder_kernel(w_ref, o_ref):
    rows = w_ref.shape[1]
    for src, dst, n in _W_IN_SEGMENTS:
        o_ref[0, :, dst:dst + n] = w_ref[0, :, src:src + n].astype(BF16)
    r = lax.broadcasted_iota(jnp.int32, (LANES, LANES), 0)
    c = lax.broadcasted_iota(jnp.int32, (LANES, LANES), 1)
    small = jnp.zeros((rows, LANES), F32)
    for src, lane0, n in _W_IN_SMALL:
        base = src // LANES * LANES
        sel = jnp.where((c >= lane0) & (c < lane0 + n) & (r == c - lane0 + (src - base)), 1.0, 0.0).astype(BF16)
        small = small + _dot(w_ref[0, :, base:base + LANES].astype(BF16), sel)
    o_ref[0, :, C_SMALL:C_SMALL + LANES] = small.astype(BF16)
    o_ref[0, :, C_SMALL + LANES:] = jnp.zeros((rows, N_PROJ - C_SMALL - LANES), BF16)


def _reorder_w_in(w_in):
    depth, d, n = w_in.shape
    tr = 256
    return pl.pallas_call(
        _reorder_kernel,
        grid=(depth, d // tr),
        in_specs=[pl.BlockSpec((1, tr, n), lambda l, i: (l, i, 0))],
        out_specs=pl.BlockSpec((1, tr, N_PROJ), lambda l, i: (l, i, 0)),
        out_shape=jax.ShapeDtypeStruct((depth, d, N_PROJ), BF16),
        compiler_params=_cp(("parallel", "parallel")),
        name="reorder_w_in",
    )(w_in)


def _routing_tables(experts_t, t):
    tk = t * TOP_K
    rb = MOE_ROW_BLOCK
    flat_e = experts_t[:TOP_K].reshape(-1)
    flat_ids = jnp.arange(tk, dtype=jnp.int32)
    e_sorted, order = lax.sort((flat_e, flat_ids), num_keys=1, is_stable=True)
    experts = jnp.arange(N_EXPERTS, dtype=jnp.int32)
    counts = jnp.sum((flat_e[:, None] == experts[None, :]).astype(jnp.int32), axis=0)
    padded = (counts + rb - 1) // rb * rb
    pad_end = jnp.cumsum(padded)
    pad_start = pad_end - padded
    start = jnp.cumsum(counts) - counts
    slot_sorted = pad_start[e_sorted] + flat_ids - start[e_sorted]
    _, inv_slot = lax.sort((order, slot_sorted), num_keys=1)
    n_blocks = -(-tk // rb) + N_EXPERTS
    blk_first = jnp.arange(n_blocks, dtype=jnp.int32) * rb
    block_e = jnp.minimum(jnp.sum((pad_end[None, :] <= blk_first[:, None]).astype(jnp.int32), axis=1),
                          N_EXPERTS - 1).astype(jnp.int32)
    pos = jnp.arange(n_blocks * rb, dtype=jnp.int32)
    e_pos = jnp.repeat(block_e, rb)
    r_pos = pos - pad_start[e_pos]
    src = jnp.clip(start[e_pos] + r_pos, 0, tk - 1)
    slot_tok = jnp.where(r_pos < counts[e_pos], order[src] % t, 0).astype(jnp.int32)
    n_used = (pad_end[-1] // rb).astype(jnp.int32).reshape(1)
    return slot_tok, inv_slot.astype(jnp.int32), block_e, n_used


def _rope_tables(seq):
    half = HEAD_DIM // 2
    inv = ROPE_THETA ** (-jnp.arange(half, dtype=F32) / half)
    ang = jnp.arange(seq).astype(F32)[:, None] * inv[None, :]
    cos, sin = jnp.cos(ang), jnp.sin(ang)
    return jnp.concatenate([cos, cos], axis=-1), jnp.concatenate([-sin, sin], axis=-1)


def _deinterleave_bias(b_up):
    depth, e, n = b_up.shape
    half = MXU_DIM // 2
    return b_up.reshape(depth, e, n // MXU_DIM, half, 2).transpose(0, 1, 2, 4, 3).reshape(depth, e, 1, n)


def kernel(x, c, ada_w, ada_b, norm_w, w_in, conv_w, conv_b, mlstm_i_b, mlstm_f_b, mlstm_norm_w, gla_w_a2, gla_b_a2, gla_norm_w, fox_f_b, w_branch, w_out, router_w, router_b, w_up, b_up, w_down, b_down):
    batch, seq, d = x.shape
    depth = ada_w.shape[0]
    t = batch * seq
    x2 = x.reshape(t, d)
    mod = _ada_mod(c, ada_w, ada_b)
    cos, sin = _rope_tables(seq)
    w_up_d = _deinterleave_cast(w_up.reshape(depth * N_EXPERTS * d, 2 * D_EXPERT))
    w_up_d = w_up_d.reshape(depth * N_EXPERTS, d, 2 * D_EXPERT)
    b_up_d = _deinterleave_bias(b_up).reshape(depth * N_EXPERTS, 1, 2 * D_EXPERT)
    w_down_b = w_down.astype(BF16).reshape(depth * N_EXPERTS, D_EXPERT, d)
    b_down_r = b_down.reshape(depth * N_EXPERTS, 1, d)
    w_branch_b = w_branch.astype(BF16).reshape(depth * N_BRANCH, BRANCH_WIDTH, d)
    w_out_b = w_out.astype(BF16)
    w_in_r = _reorder_w_in(w_in)

    for l in range(depth):
        mod3 = mod[l].reshape(batch, 1, 6 * d)
        nw3 = norm_w[l].reshape(4, 1, d)

        proj = _in_proj(x2, mod3, nw3, w_in_r, l, seq, nw_idx=0, sc_idx=1, sh_idx=0)
        small_bias = jnp.zeros((1, LANES), F32)
        small_bias = small_bias.at[0, SM_I:SM_I + BRANCH_HEADS].set(mlstm_i_b[l])
        small_bias = small_bias.at[0, SM_F:SM_F + BRANCH_HEADS].set(mlstm_f_b[l])
        small_bias = small_bias.at[0, SM_FF:SM_FF + BRANCH_HEADS].set(fox_f_b[l])
        w_a2_pad = jnp.zeros((LANES, GLA_HEADS * GLA_DK), F32).at[SM_GA:SM_GA + GLA_RANK].set(gla_w_a2[l])

        y_a = _moba(proj, cos, sin, batch, seq)
        y_b = _mlstm(proj, small_bias, conv_w[l], conv_b[l], mlstm_norm_w[l], batch, seq)
        y_c = _gla(proj, w_a2_pad, gla_b_a2[l], gla_norm_w[l], batch, seq)
        y_d = _fox(proj, small_bias, batch, seq)
        merged = _merge((y_a, y_b, y_c, y_d), proj, w_branch_b, l)
        x2 = _out_proj(merged, w_out_b, l, x2, nw3, mod3, seq, nw_idx=1, g_idx=2)

        rw_pad = jnp.zeros((d, LANES), F32).at[:, :N_EXPERTS].set(router_w[l])
        rb_pad = jnp.zeros((1, LANES), F32).at[0, :N_EXPERTS].set(router_b[l])
        h2, route, experts_t = _router(x2, nw3, mod3, rw_pad, rb_pad, seq, nw_idx=2, sc_idx=4, sh_idx=3)
        slot_tok, inv_slot, block_e, n_used = _routing_tables(experts_t, t)
        yb = _experts(h2, slot_tok, block_e, n_used, w_up_d, b_up_d, w_down_b, b_down_r, l)
        x2 = _combine(yb, inv_slot, route, x2, nw3, mod3, seq, nw_idx=3, g_idx=5)

    return x2.reshape(batch, seq, d)
```

```python
import functools

import jax
import jax.numpy as jnp
from jax import lax
from jax.experimental import pallas as pl
from jax.experimental.pallas import tpu as pltpu

F32 = jnp.float32
BF16 = jnp.bfloat16
NEG_INF = float("-inf")
MASK_PENALTY = -1e9

D_MODEL = 2048
HEAD_DIM = 128
N_BRANCH = 4
BRANCH_HEADS = 4
BRANCH_WIDTH = BRANCH_HEADS * HEAD_DIM

MOBA_BLOCK = 256
MOBA_TOPK = 3
ROPE_THETA = 10000.0

MLSTM_CHUNK = 128
MLSTM_CONV = 4

GLA_HEADS = 4
GLA_DK = 64
GLA_DV = 128
GLA_RANK = 16
GLA_TAU = 16.0
GLA_CHUNK = 64
GLA_SUB = 16

N_EXPERTS = 32
TOP_K = 4
D_EXPERT = D_MODEL // 2
SWIGLU_LIMIT = 7.0
SWIGLU_ALPHA = 1.702
MOE_ROW_BLOCK = 512

RMS_EPS = 1e-6

LANES = 128
MXU_DIM = 256
CHUNK = 128
ATT_TILE = 512

C_AQ, C_AK, C_AV = 0, 512, 1024
C_MQ, C_MK, C_MV, C_MO = 1536, 2048, 2560, 3072
C_GQ, C_GK, C_GV, C_GR = 3584, 3840, 4096, 4608
C_FQ, C_FK, C_FV = 5120, 5632, 6144
C_GATE = 6656
C_SMALL = 14848
SM_I, SM_F, SM_GA, SM_FF = 0, 4, 8, 24
N_PROJ = 15360

NN = (((1,), (0,)), ((), ()))
NT = (((1,), (1,)), ((), ()))
TN = (((0,), (0,)), ((), ()))

VMEM_LIMIT = 56 * 1024 * 1024


def _cp(sem, vmem=VMEM_LIMIT):
    return pltpu.CompilerParams(dimension_semantics=sem, vmem_limit_bytes=vmem)


def _dot(a, b, dn=NN):
    return lax.dot_general(a, b, dn, preferred_element_type=F32)


def _split3(x):
    hi = x.astype(BF16)
    r1 = x - hi.astype(F32)
    mid = r1.astype(BF16)
    lo = (r1 - mid.astype(F32)).astype(BF16)
    return hi, mid, lo


def _dot01(sel, x, dn=NN):
    hi, mid, lo = _split3(x)
    return _dot(sel, hi, dn) + _dot(sel, mid, dn) + _dot(sel, lo, dn)


def _dot_x3(a, b, dn=NN):
    ah = a.astype(BF16)
    al = (a - ah.astype(F32)).astype(BF16)
    bh = b.astype(BF16)
    bl = (b - bh.astype(F32)).astype(BF16)
    return _dot(ah, bh, dn) + _dot(al, bh, dn) + _dot(ah, bl, dn)


def _log_sigmoid(x):
    return jnp.minimum(x, 0.0) - jnp.log1p(jnp.exp(-jnp.abs(x)))


def _sigmoid(x):
    return 0.5 * jnp.tanh(0.5 * x) + 0.5


def _tril_bf16(n):
    r = lax.broadcasted_iota(jnp.int32, (n, n), 0)
    c = lax.broadcasted_iota(jnp.int32, (n, n), 1)
    return jnp.where(c <= r, 1.0, 0.0).astype(BF16)


def _row_selector(lane_idx):
    r = lax.broadcasted_iota(jnp.int32, (8, LANES), 0)
    c = lax.broadcasted_iota(jnp.int32, (8, LANES), 1)
    return jnp.where((r == 0) & (c == lane_idx), 1.0, 0.0).astype(BF16)


def _pick_lane(x, lane_idx):
    lane = lax.broadcasted_iota(jnp.int32, x.shape, 1)
    return jnp.sum(jnp.where(lane == lane_idx, x, 0.0), axis=-1, keepdims=True)


def _ada_kernel(c_ref, w_ref, b_ref, o_ref):
    c = c_ref[...]
    cond = c * _sigmoid(c)
    o_ref[0] = _dot(cond.astype(BF16), w_ref[0].astype(BF16)) + b_ref[0]


def _ada_mod(c, ada_w, ada_b):
    depth, d, n = ada_w.shape
    b = c.shape[0]
    tn = 1024
    return pl.pallas_call(
        _ada_kernel,
        grid=(depth, n // tn),
        in_specs=[
            pl.BlockSpec((b, d), lambda l, j: (0, 0)),
            pl.BlockSpec((1, d, tn), lambda l, j: (l, 0, j)),
            pl.BlockSpec((1, 1, tn), lambda l, j: (l, 0, j)),
        ],
        out_specs=pl.BlockSpec((1, b, tn), lambda l, j: (l, 0, j)),
        out_shape=jax.ShapeDtypeStruct((depth, b, n), F32),
        compiler_params=_cp(("parallel", "parallel")),
        name="ada_mod",
    )(c, ada_w, ada_b.reshape(depth, 1, n))


def _norm_mod_rows(x, nw, sc, sh):
    var = jnp.mean(x * x, axis=-1, keepdims=True)
    y = x * lax.rsqrt(var + RMS_EPS) * nw
    return y * (1.0 + sc) + sh


def _rms_rows(y, nw):
    var = jnp.mean(y * y, axis=-1, keepdims=True)
    return y * lax.rsqrt(var + RMS_EPS) * nw


def _inproj_kernel(x_ref, nw_ref, sc_ref, sh_ref, w_ref, o_ref, h_scr, *, rows):
    @pl.when(pl.program_id(1) == 0)
    def _():
        def body(r, carry):
            sl = pl.ds(pl.multiple_of(r * rows, rows), rows)
            h = _norm_mod_rows(x_ref[sl, :], nw_ref[0], sc_ref[0], sh_ref[0])
            h_scr[sl, :] = h.astype(BF16)
            return carry
        lax.fori_loop(0, x_ref.shape[0] // rows, body, 0)

    o_ref[...] = _dot(h_scr[...], w_ref[0])


def _in_proj(x2, mod3, nw3, w_re, layer, seq, *, nw_idx, sc_idx, sh_idx):
    t, d = x2.shape
    n = w_re.shape[2]
    tm = min(1024, seq)
    tn = 1024
    per_b = seq // tm
    return pl.pallas_call(
        functools.partial(_inproj_kernel, rows=min(256, tm)),
        grid=(t // tm, n // tn),
        in_specs=[
            pl.BlockSpec((tm, d), lambda i, j: (i, 0)),
            pl.BlockSpec((1, 1, d), lambda i, j: (nw_idx, 0, 0)),
            pl.BlockSpec((1, 1, d), lambda i, j: (i // per_b, 0, sc_idx)),
            pl.BlockSpec((1, 1, d), lambda i, j: (i // per_b, 0, sh_idx)),
            pl.BlockSpec((1, d, tn), lambda i, j: (layer, 0, j)),
        ],
        out_specs=pl.BlockSpec((tm, tn), lambda i, j: (i, j)),
        out_shape=jax.ShapeDtypeStruct((t, n), F32),
        scratch_shapes=[pltpu.VMEM((tm, d), BF16)],
        compiler_params=_cp(("parallel", "arbitrary")),
        name="in_proj",
    )(x2, nw3, mod3, mod3, w_re)


ATT_HEADS = 2


def _attention_sweep(i, tile, q_augs, kaug_scr, vb_scr, o_ref):
    nh = len(q_augs)
    rows_q = pl.ds(pl.multiple_of(i * tile, tile), tile)
    r_io = lax.broadcasted_iota(jnp.int32, (tile, tile), 0)
    c_io = lax.broadcasted_iota(jnp.int32, (tile, tile), 1)
    init = []
    for hh in range(nh):
        s = _dot(q_augs[hh], kaug_scr[hh, rows_q, :], NT)
        s = jnp.where(c_io <= r_io, s, NEG_INF)
        m0 = jnp.max(s, axis=-1, keepdims=True)
        p = jnp.exp(s - m0)
        init += [m0, jnp.sum(p, axis=-1, keepdims=True), _dot(p.astype(BF16), vb_scr[hh, rows_q, :])]

    def body(j, carry):
        rows = pl.ds(pl.multiple_of(j * tile, tile), tile)
        out = []
        for hh in range(nh):
            m, l, acc = carry[3 * hh:3 * hh + 3]
            s = _dot(q_augs[hh], kaug_scr[hh, rows, :], NT)
            m_new = jnp.maximum(m, jnp.max(s, axis=-1, keepdims=True))
            alpha = jnp.exp(m - m_new)
            p = jnp.exp(s - m_new)
            l = alpha * l + jnp.sum(p, axis=-1, keepdims=True)
            acc = alpha * acc + _dot(p.astype(BF16), vb_scr[hh, rows, :])
            out += [m_new, l, acc]
        return tuple(out)

    fin = lax.fori_loop(0, i, body, tuple(init))
    for hh in range(nh):
        o_ref[:, hh * HEAD_DIM:(hh + 1) * HEAD_DIM] = (fin[3 * hh + 2] / fin[3 * hh + 1]).astype(o_ref.dtype)


def _rope(x, cos, sin):
    return x * cos + pltpu.roll(x, HEAD_DIM // 2, 1) * sin


def _moba_kernel(q_ref, k_ref, v_ref, cos_ref, sin_ref, o_ref, kaug_scr, vb_scr, km_scr, *, seq, tile, n_sel):
    blk = MOBA_BLOCK
    nblk = seq // blk
    nb8 = -(-nblk // 8) * 8
    i = pl.program_id(2)

    @pl.when(i == 0)
    def _():
        km_scr[...] = jnp.zeros_like(km_scr)
        lane_b = lax.broadcasted_iota(jnp.int32, (blk, LANES), 1)
        for hh in range(ATT_HEADS):
            cols = slice(hh * HEAD_DIM, (hh + 1) * HEAD_DIM)
            for j in range(nblk):
                rows = pl.ds(j * blk, blk)
                kk = _rope(k_ref[rows, cols], cos_ref[rows, :], sin_ref[rows, :])
                kaug_scr[hh, rows, 0:HEAD_DIM] = kk.astype(BF16)
                kaug_scr[hh, rows, HEAD_DIM:2 * HEAD_DIM] = jnp.where(lane_b == j, 1.0, 0.0).astype(BF16)
                km_scr[hh, j:j + 1, :] = jnp.mean(kk, axis=0, keepdims=True)
                vb_scr[hh, rows, :] = v_ref[rows, cols].astype(BF16)

    rows_q = pl.ds(pl.multiple_of(i * tile, tile), tile)
    blk_id = lax.broadcasted_iota(jnp.int32, (nb8, tile), 0)
    blk_f = blk_id.astype(F32)
    q_blk = i * (tile // blk) + lax.broadcasted_iota(jnp.int32, (nb8, tile), 1) // blk
    q_augs = []
    for hh in range(ATT_HEADS):
        cols = slice(hh * HEAD_DIM, (hh + 1) * HEAD_DIM)
        q = _rope(q_ref[:, cols], cos_ref[rows_q, :], sin_ref[rows_q, :])
        g = _dot_x3(km_scr[hh], q, NT)[0:nb8, :]
        g = jnp.where(blk_id < q_blk, g, NEG_INF)
        keep = blk_id == q_blk
        for _ in range(n_sel):
            m = jnp.max(g, axis=0, keepdims=True)
            idx = jnp.min(jnp.where(g == m, blk_f, float(LANES)), axis=0, keepdims=True)
            hit = blk_f == idx
            keep = keep | (hit & (m > NEG_INF))
            g = jnp.where(hit, NEG_INF, g)
        pen_t = jnp.where(keep, 0.0, MASK_PENALTY)
        pen_t = jnp.concatenate([pen_t, jnp.full((LANES - nb8, tile), MASK_PENALTY, F32)], axis=0)
        penalty = pen_t.T
        q_augs.append(jnp.concatenate([(q * (HEAD_DIM ** -0.5)).astype(BF16), penalty.astype(BF16)], axis=1))
    _attention_sweep(i, tile, q_augs, kaug_scr, vb_scr, o_ref)


def _moba(proj, cos, sin, batch, seq):
    tile = min(ATT_TILE, seq)
    nq = seq // tile
    nblk = seq // MOBA_BLOCK
    n_sel = max(1, min(MOBA_TOPK, nblk - 1))
    hb = HEAD_DIM
    hw = ATT_HEADS * hb
    return pl.pallas_call(
        functools.partial(_moba_kernel, seq=seq, tile=tile, n_sel=n_sel),
        grid=(batch, BRANCH_HEADS // ATT_HEADS, nq),
        in_specs=[
            pl.BlockSpec((tile, hw), lambda b, p, i: (b * nq + i, C_AQ // hw + p)),
            pl.BlockSpec((seq, hw), lambda b, p, i: (b, C_AK // hw + p)),
            pl.BlockSpec((seq, hw), lambda b, p, i: (b, C_AV // hw + p)),
            pl.BlockSpec((seq, hb), lambda b, p, i: (0, 0)),
            pl.BlockSpec((seq, hb), lambda b, p, i: (0, 0)),
        ],
        out_specs=pl.BlockSpec((tile, hw), lambda b, p, i: (b * nq + i, p)),
        out_shape=jax.ShapeDtypeStruct((batch * seq, BRANCH_WIDTH), BF16),
        scratch_shapes=[pltpu.VMEM((ATT_HEADS, seq, 2 * hb), BF16), pltpu.VMEM((ATT_HEADS, seq, hb), BF16),
                        pltpu.VMEM((ATT_HEADS, LANES, hb), F32)],
        compiler_params=_cp(("parallel", "parallel", "arbitrary")),
        name="moba",
    )(proj, proj, proj, cos, sin)


def _fox_kernel(q_ref, k_ref, v_ref, sm_ref, bias_ref, o_ref, kaug_scr, vb_scr, fcol_scr, *, seq, tile):
    p_idx = pl.program_id(1)
    i = pl.program_id(2)

    def bias_lanes(f, first):
        hi, mid, lo = _split3(f)
        lane = lax.broadcasted_iota(jnp.int32, (f.shape[0], LANES), 1)
        ones_first = 3 - first
        out = jnp.where((lane >= ones_first) & (lane < ones_first + 3), 1.0, 0.0)
        out = jnp.where(lane == first, hi.astype(F32), out)
        out = jnp.where(lane == first + 1, mid.astype(F32), out)
        out = jnp.where(lane == first + 2, lo.astype(F32), out)
        return out.astype(BF16)

    @pl.when(i == 0)
    def _():
        tril = _tril_bf16(CHUNK)
        carry = jnp.zeros((1, LANES), F32)
        for c in range(seq // CHUNK):
            rows = pl.ds(c * CHUNK, CHUNK)
            lf = _log_sigmoid(sm_ref[rows, :] + bias_ref[...])
            cs = _dot01(tril, lf) + carry
            fcol_scr[rows, :] = cs
            carry = cs[CHUNK - 1:CHUNK, :]
        for hh in range(ATT_HEADS):
            cols = slice(hh * HEAD_DIM, (hh + 1) * HEAD_DIM)
            f_lane = SM_FF + p_idx * ATT_HEADS + hh
            for c in range(seq // tile):
                rows = pl.ds(c * tile, tile)
                kaug_scr[hh, rows, 0:HEAD_DIM] = k_ref[rows, cols].astype(BF16)
                kaug_scr[hh, rows, HEAD_DIM:2 * HEAD_DIM] = bias_lanes(-_pick_lane(fcol_scr[rows, :], f_lane), 3)
                vb_scr[hh, rows, :] = v_ref[rows, cols].astype(BF16)

    rows_q = pl.ds(pl.multiple_of(i * tile, tile), tile)
    fq = fcol_scr[rows_q, :]
    q_augs = []
    for hh in range(ATT_HEADS):
        cols = slice(hh * HEAD_DIM, (hh + 1) * HEAD_DIM)
        f_t = _pick_lane(fq, SM_FF + p_idx * ATT_HEADS + hh)
        q_augs.append(jnp.concatenate([(q_ref[:, cols] * (HEAD_DIM ** -0.5)).astype(BF16), bias_lanes(f_t, 0)], axis=1))
    _attention_sweep(i, tile, q_augs, kaug_scr, vb_scr, o_ref)


def _fox(proj, small_bias, batch, seq):
    tile = min(ATT_TILE, seq)
    nq = seq // tile
    hb = HEAD_DIM
    hw = ATT_HEADS * hb
    return pl.pallas_call(
        functools.partial(_fox_kernel, seq=seq, tile=tile),
        grid=(batch, BRANCH_HEADS // ATT_HEADS, nq),
        in_specs=[
            pl.BlockSpec((tile, hw), lambda b, p, i: (b * nq + i, C_FQ // hw + p)),
            pl.BlockSpec((seq, hw), lambda b, p, i: (b, C_FK // hw + p)),
            pl.BlockSpec((seq, hw), lambda b, p, i: (b, C_FV // hw + p)),
            pl.BlockSpec((seq, LANES), lambda b, p, i: (b, C_SMALL // LANES)),
            pl.BlockSpec((1, LANES), lambda b, p, i: (0, 0)),
        ],
        out_specs=pl.BlockSpec((tile, hw), lambda b, p, i: (b * nq + i, p)),
        out_shape=jax.ShapeDtypeStruct((batch * seq, BRANCH_WIDTH), BF16),
        scratch_shapes=[pltpu.VMEM((ATT_HEADS, seq, 2 * hb), BF16), pltpu.VMEM((ATT_HEADS, seq, hb), BF16),
                        pltpu.VMEM((seq, LANES), F32)],
        compiler_params=_cp(("parallel", "parallel", "arbitrary")),
        name="fox",
    )(proj, proj, proj, proj, small_bias)


def _conv_silu_rows(x_ref, w_ref, b_ref, r0, n):
    k = w_ref.shape[0]
    x0 = x_ref[r0:r0 + n, :]
    y = x0 * w_ref[k - 1:k, :] + b_ref[...]
    for sft in range(1, k):
        if r0 >= sft:
            xs = x_ref[r0 - sft:r0 - sft + n, :]
        else:
            row = lax.broadcasted_iota(jnp.int32, x0.shape, 0)
            xs = jnp.where(row >= sft, pltpu.roll(x0, sft, 0), 0.0)
        y = y + xs * w_ref[k - 1 - sft:k - sft, :]
    return y * _sigmoid(y)


MLSTM_UNROLL = 16


def _mlstm_kernel(q_ref, k_ref, v_ref, o_ref, sm_ref, bias_ref, cwq_ref, cwk_ref, cbq_ref, cbk_ref,
                  nw_ref, y_ref, q_scr, k_scr, x_scr, g_scr, irow_scr, grow_scr, *, seq):
    h = pl.program_id(1)
    L = MLSTM_CHUNK
    nc = seq // L
    unroll = MLSTM_UNROLL if nc % MLSTM_UNROLL == 0 else (nc if nc < MLSTM_UNROLL else 1)
    i_lane = SM_I + h
    f_lane = SM_F + h

    tril = _tril_bf16(L)
    sel_i = _row_selector(i_lane)
    sel_f = _row_selector(f_lane)
    lane_s = lax.broadcasted_iota(jnp.int32, (L, LANES), 1)
    for c in range(nc):
        r0 = c * L
        rows = pl.ds(r0, L)
        q_scr[rows, :] = _conv_silu_rows(q_ref, cwq_ref, cbq_ref, r0, L)
        k_scr[rows, :] = _conv_silu_rows(k_ref, cwk_ref, cbk_ref, r0, L) * (HEAD_DIM ** -0.5)
        smb = sm_ref[rows, :] + bias_ref[...]
        xc = jnp.where(lane_s < SM_F, smb, _log_sigmoid(smb))
        gc = _dot01(tril, xc)
        x_scr[rows, :] = xc
        g_scr[rows, :] = gc
        irow_scr[:, r0:r0 + L] = _dot01(sel_i, xc, NT)
        grow_scr[:, r0:r0 + L] = _dot01(sel_f, gc, NT)

    r_io = lax.broadcasted_iota(jnp.int32, (L, L), 0)
    c_io = lax.broadcasted_iota(jnp.int32, (L, L), 1)
    causal = c_io <= r_io

    def chunk(c, carry):
        C, n, m = carry
        rows = pl.ds(pl.multiple_of(c * L, L), L)
        qc = q_scr[rows, :]
        kc = k_scr[rows, :]
        vb = v_ref[rows, :].astype(BF16)
        qb = qc.astype(BF16)
        g_col = _pick_lane(g_scr[rows, :], f_lane)
        i_col = _pick_lane(x_scr[rows, :], i_lane)
        g_row = grow_scr[0:1, rows]
        i_row = irow_scr[0:1, rows]
        dmat = jnp.where(causal, g_col - g_row + i_row, NEG_INF)
        a = g_col + m
        m_out = jnp.maximum(a, jnp.max(dmat, axis=-1, keepdims=True))
        w = jnp.exp(dmat - m_out) * _dot(qb, kc.astype(BF16), NT)
        inter = jnp.exp(a - m_out)
        num = inter * _dot(qb, C.astype(BF16)) + _dot(w.astype(BF16), vb)
        den = inter * jnp.sum(qc * n, axis=-1, keepdims=True) + jnp.sum(w, axis=-1, keepdims=True)
        hh = num / jnp.maximum(jnp.abs(den), jnp.exp(-m_out))
        g_last = g_row[:, L - 1:L]
        lw = g_last - g_col + i_col
        m_new = jnp.maximum(g_last + m, jnp.max(lw, axis=0, keepdims=True))
        wk = jnp.exp(lw - m_new)
        decay = jnp.exp(g_last + m - m_new)
        kw = kc * wk
        C = decay * C + _dot(kw.astype(BF16), vb, TN)
        n = decay * n + jnp.sum(kw, axis=0, keepdims=True)
        yn = _rms_rows(hh, nw_ref[0])
        y_ref[rows, :] = (yn * _sigmoid(o_ref[rows, :])).astype(y_ref.dtype)
        return C, n, m_new

    def step(it, carry):
        for u in range(unroll):
            carry = chunk(it * unroll + u, carry)
        return carry

    init = (jnp.zeros((HEAD_DIM, HEAD_DIM), F32), jnp.zeros((1, HEAD_DIM), F32), jnp.zeros((1, 1), F32))
    lax.fori_loop(0, nc // unroll, step, init)


def _mlstm(proj, small_bias, conv_w, conv_b, norm_w, batch, seq):
    hb = HEAD_DIM
    nh = BRANCH_HEADS
    blk = lambda c0: pl.BlockSpec((seq, hb), lambda b, h: (b, c0 // hb + h))
    return pl.pallas_call(
        functools.partial(_mlstm_kernel, seq=seq),
        grid=(batch, nh),
        in_specs=[
            blk(C_MQ), blk(C_MK), blk(C_MV), blk(C_MO),
            pl.BlockSpec((seq, LANES), lambda b, h: (b, C_SMALL // LANES)),
            pl.BlockSpec((1, LANES), lambda b, h: (0, 0)),
            pl.BlockSpec((MLSTM_CONV, hb), lambda b, h: (0, h)),
            pl.BlockSpec((MLSTM_CONV, hb), lambda b, h: (0, nh + h)),
            pl.BlockSpec((1, hb), lambda b, h: (0, h)),
            pl.BlockSpec((1, hb), lambda b, h: (0, nh + h)),
            pl.BlockSpec((1, 1, hb), lambda b, h: (h, 0, 0)),
        ],
        out_specs=pl.BlockSpec((seq, hb), lambda b, h: (b, h)),
        out_shape=jax.ShapeDtypeStruct((batch * seq, BRANCH_WIDTH), BF16),
        scratch_shapes=[
            pltpu.VMEM((seq, hb), F32), pltpu.VMEM((seq, hb), F32),
            pltpu.VMEM((seq, LANES), F32), pltpu.VMEM((seq, LANES), F32),
            pltpu.VMEM((8, seq), F32), pltpu.VMEM((8, seq), F32),
        ],
        compiler_params=_cp(("parallel", "parallel")),
        name="mlstm",
    )(proj, proj, proj, proj, proj, small_bias, conv_w, conv_w, conv_b.reshape(1, -1), conv_b.reshape(1, -1),
      norm_w.reshape(nh, 1, hb))


GLA_UNROLL = 16


def _gla_kernel(q_ref, k_ref, v_ref, r_ref, sm_ref, wa_ref, ba_ref, nw_ref, y_ref, la_scr, *, seq):
    L = GLA_CHUNK
    sub = GLA_SUB
    nc = seq // L
    unroll = GLA_UNROLL if nc % GLA_UNROLL == 0 else (nc if nc < GLA_UNROLL else 1)
    dk, dv = GLA_DK, GLA_DV

    for c in range(seq // CHUNK):
        rows = pl.ds(c * CHUNK, CHUNK)
        z = _dot_x3(sm_ref[rows, :], wa_ref[...]) + ba_ref[...]
        la_scr[rows, :] = _log_sigmoid(z) * (1.0 / GLA_TAU)
    tril = _tril_bf16(L)
    r_io = lax.broadcasted_iota(jnp.int32, (sub, L), 0)
    c_io = lax.broadcasted_iota(jnp.int32, (sub, L), 1)
    row_l = lax.broadcasted_iota(jnp.int32, (L, dk), 0)

    def chunk(c, states):
        rows = pl.ds(pl.multiple_of(c * L, L), L)
        bc2 = _dot01(tril, la_scr[rows, :])
        q2 = q_ref[rows, :] * (dk ** -0.5)
        k2 = k_ref[rows, :]
        new_states = []
        for hh in range(2):
            st = states[hh]
            lanes = slice(hh * dk, (hh + 1) * dk)
            b = bc2[:, lanes]
            qh = q2[:, lanes]
            kh = k2[:, lanes]
            vb = v_ref[rows, hh * dv:(hh + 1) * dv].astype(BF16)
            o = _dot((qh * jnp.exp(b)).astype(BF16), st.astype(BF16), NT)
            blocks = []
            for ib in range(L // sub):
                ref_row = b[ib * sub:ib * sub + 1, :]
                qi = qh[ib * sub:(ib + 1) * sub, :] * jnp.exp(b[ib * sub:(ib + 1) * sub, :] - ref_row)
                ki = kh * jnp.exp(jnp.where(row_l < (ib + 1) * sub, ref_row - b, 0.0))
                a = _dot(qi.astype(BF16), ki.astype(BF16), NT)
                blocks.append(jnp.where(c_io <= r_io + ib * sub, a, 0.0))
            attn = jnp.concatenate(blocks, axis=0)
            o = o + _dot(attn.astype(BF16), vb)
            b_last = b[L - 1:L, :]
            ke = kh * jnp.exp(b_last - b)
            st = jnp.exp(b_last) * st + _dot(vb, ke.astype(BF16), TN)
            new_states.append(st)
            yn = _rms_rows(o, nw_ref[0, hh:hh + 1, :])
            r = r_ref[rows, hh * dv:(hh + 1) * dv]
            y_ref[rows, hh * dv:(hh + 1) * dv] = (yn * (r * _sigmoid(r))).astype(y_ref.dtype)
        return tuple(new_states)

    def step(it, states):
        for u in range(unroll):
            states = chunk(it * unroll + u, states)
        return states

    init = (jnp.zeros((dv, dk), F32), jnp.zeros((dv, dk), F32))
    lax.fori_loop(0, nc // unroll, step, init)


def _gla(proj, w_a2_pad, b_a2, norm_w, batch, seq):
    return pl.pallas_call(
        functools.partial(_gla_kernel, seq=seq),
        grid=(batch, GLA_HEADS // 2),
        in_specs=[
            pl.BlockSpec((seq, LANES), lambda b, p: (b, C_GQ // LANES + p)),
            pl.BlockSpec((seq, LANES), lambda b, p: (b, C_GK // LANES + p)),
            pl.BlockSpec((seq, 2 * GLA_DV), lambda b, p: (b, C_GV // (2 * GLA_DV) + p)),
            pl.BlockSpec((seq, 2 * GLA_DV), lambda b, p: (b, C_GR // (2 * GLA_DV) + p)),
            pl.BlockSpec((seq, LANES), lambda b, p: (b, C_SMALL // LANES)),
            pl.BlockSpec((LANES, LANES), lambda b, p: (0, p)),
            pl.BlockSpec((1, LANES), lambda b, p: (0, p)),
            pl.BlockSpec((1, 2, GLA_DV), lambda b, p: (p, 0, 0)),
        ],
        out_specs=pl.BlockSpec((seq, 2 * GLA_DV), lambda b, p: (b, p)),
        out_shape=jax.ShapeDtypeStruct((batch * seq, GLA_HEADS * GLA_DV), BF16),
        scratch_shapes=[pltpu.VMEM((seq, LANES), F32)],
        compiler_params=_cp(("parallel", "parallel")),
        name="gla",
    )(proj, proj, proj, proj, proj, w_a2_pad, b_a2.reshape(1, -1), norm_w.reshape(GLA_HEADS // 2, 2, GLA_DV))


def _merge_kernel(ya_ref, yb_ref, yc_ref, yd_ref, g0_ref, g1_ref, g2_ref, g3_ref, w_ref, o_ref):
    ys = (ya_ref, yb_ref, yc_ref, yd_ref)
    gs = (g0_ref, g1_ref, g2_ref, g3_ref)
    acc = None
    for b in range(N_BRANCH):
        term = _sigmoid(gs[b][...]) * _dot(ys[b][...], w_ref[b])
        acc = term if acc is None else acc + term
    o_ref[...] = acc.astype(o_ref.dtype)


def _merge(ys, proj, w_branch_bf16, layer):
    t = proj.shape[0]
    d = D_MODEL
    tm, tn = 1024, 512
    tm = min(tm, t)
    yspec = pl.BlockSpec((tm, BRANCH_WIDTH), lambda i, j: (i, 0))
    gspec = lambda b: pl.BlockSpec((tm, tn), lambda i, j: (i, (C_GATE + b * d) // tn + j))
    return pl.pallas_call(
        _merge_kernel,
        grid=(t // tm, d // tn),
        in_specs=[yspec] * 4 + [gspec(b) for b in range(N_BRANCH)]
        + [pl.BlockSpec((N_BRANCH, BRANCH_WIDTH, tn), lambda i, j: (layer, 0, j))],
        out_specs=pl.BlockSpec((tm, tn), lambda i, j: (i, j)),
        out_shape=jax.ShapeDtypeStruct((t, d), BF16),
        compiler_params=_cp(("parallel", "parallel")),
        name="merge",
    )(*ys, proj, proj, proj, proj, w_branch_bf16)


def _outproj_kernel(m_ref, w_ref, x_ref, nw_ref, g_ref, o_ref):
    y = _dot(m_ref[...], w_ref[0])
    o_ref[...] = x_ref[...] + g_ref[0] * _rms_rows(y, nw_ref[0])


def _out_proj(merged, w_out_bf16, layer, x2, nw3, mod3, seq, *, nw_idx, g_idx):
    t, d = x2.shape
    tm = min(512, seq)
    per_b = seq // tm
    return pl.pallas_call(
        _outproj_kernel,
        grid=(t // tm,),
        in_specs=[
            pl.BlockSpec((tm, d), lambda i: (i, 0)),
            pl.BlockSpec((1, d, d), lambda i: (layer, 0, 0)),
            pl.BlockSpec((tm, d), lambda i: (i, 0)),
            pl.BlockSpec((1, 1, d), lambda i: (nw_idx, 0, 0)),
            pl.BlockSpec((1, 1, d), lambda i: (i // per_b, 0, g_idx)),
        ],
        out_specs=pl.BlockSpec((tm, d), lambda i: (i, 0)),
        out_shape=jax.ShapeDtypeStruct((t, d), F32),
        compiler_params=_cp(("parallel",)),
        name="out_proj",
    )(merged, w_out_bf16, x2, nw3, mod3)


def _router_kernel(x_ref, nw_ref, sc_ref, sh_ref, rw_ref, rb_ref, h_ref, r_ref, et_ref):
    h = _norm_mod_rows(x_ref[...], nw_ref[0], sc_ref[0], sh_ref[0])
    h_ref[...] = h
    logits = _dot_x3(h, rw_ref[...]) + rb_ref[...]
    lane = lax.broadcasted_iota(jnp.int32, logits.shape, 1)
    lane_f = lane.astype(F32)
    l = jnp.where(lane < N_EXPERTS, logits, NEG_INF)
    vals, idxs = [], []
    for _ in range(TOP_K):
        m = jnp.max(l, axis=-1, keepdims=True)
        idx = jnp.min(jnp.where(l == m, lane_f, float(LANES)), axis=-1, keepdims=True)
        vals.append(m)
        idxs.append(idx)
        l = jnp.where(lane_f == idx, NEG_INF, l)
    es = [jnp.exp(v - vals[0]) for v in vals]
    tot = es[0]
    for e in es[1:]:
        tot = tot + e
    out = jnp.zeros(logits.shape, F32)
    for r in range(TOP_K):
        out = jnp.where(lane == r, es[r] / tot, out)
        out = jnp.where(lane == TOP_K + r, idxs[r], out)
    r_ref[...] = out
    et_ref[...] = out.T[TOP_K:TOP_K + 8, :].astype(jnp.int32)


def _router(x2, nw3, mod3, router_w_pad, router_b_pad, seq, *, nw_idx, sc_idx, sh_idx):
    t, d = x2.shape
    tm = min(256, seq)
    per_b = seq // tm
    return pl.pallas_call(
        _router_kernel,
        grid=(t // tm,),
        in_specs=[
            pl.BlockSpec((tm, d), lambda i: (i, 0)),
            pl.BlockSpec((1, 1, d), lambda i: (nw_idx, 0, 0)),
            pl.BlockSpec((1, 1, d), lambda i: (i // per_b, 0, sc_idx)),
            pl.BlockSpec((1, 1, d), lambda i: (i // per_b, 0, sh_idx)),
            pl.BlockSpec((d, LANES), lambda i: (0, 0)),
            pl.BlockSpec((1, LANES), lambda i: (0, 0)),
        ],
        out_specs=[pl.BlockSpec((tm, d), lambda i: (i, 0)), pl.BlockSpec((tm, LANES), lambda i: (i, 0)),
                   pl.BlockSpec((8, tm), lambda i: (0, i))],
        out_shape=[jax.ShapeDtypeStruct((t, d), F32), jax.ShapeDtypeStruct((t, LANES), F32),
                   jax.ShapeDtypeStruct((8, t), jnp.int32)],
        compiler_params=_cp(("parallel",)),
        name="router",
    )(x2, nw3, mod3, mod3, router_w_pad, router_b_pad)


def _deinterleave_kernel(w_ref, o_ref):
    n = MXU_DIM
    r = lax.broadcasted_iota(jnp.int32, (n, n), 0)
    c = lax.broadcasted_iota(jnp.int32, (n, n), 1)
    src = jnp.where(c < n // 2, 2 * c, 2 * (c - n // 2) + 1)
    perm = jnp.where(r == src, 1.0, 0.0).astype(BF16)
    for m in range(w_ref.shape[1] // n):
        cols = slice(m * n, (m + 1) * n)
        o_ref[:, cols] = _dot(w_ref[:, cols].astype(BF16), perm).astype(BF16)


def _deinterleave_cast(w2):
    rows, n = w2.shape
    tr = 1024
    return pl.pallas_call(
        _deinterleave_kernel,
        grid=(rows // tr,),
        in_specs=[pl.BlockSpec((tr, n), lambda i: (i, 0))],
        out_specs=pl.BlockSpec((tr, n), lambda i: (i, 0)),
        out_shape=jax.ShapeDtypeStruct((rows, n), BF16),
        compiler_params=_cp(("parallel",)),
        name="deinterleave_cast",
    )(w2)


def _expert_kernel(be_ref, nused_ref, idx_hbm, h_hbm, wu_ref, bu_ref, wd_ref, bd_ref, o_ref,
                   xbuf, idx_smem, isems, rsems, *, rows, nb):
    b = pl.program_id(0)
    slot = b % 2
    nxt = 1 - slot
    n_used = nused_ref[0]

    def idx_copy(blk, s):
        return pltpu.make_async_copy(idx_hbm.at[blk], idx_smem.at[s], isems.at[s])

    def row_copy(tok, r, s):
        return pltpu.make_async_copy(h_hbm.at[pl.ds(tok, 1), :], xbuf.at[s, pl.ds(r, 1), :], rsems.at[s])

    def drain_rows(s):
        pltpu.make_async_copy(h_hbm.at[pl.ds(0, rows), :], xbuf.at[s], rsems.at[s]).wait()

    @pl.when(b == 0)
    def _():
        idx_copy(0, 0).start()
        idx_copy(0, 0).wait()

        def issue(r, carry):
            row_copy(idx_smem[0, r], r, 0).start()
            return carry
        lax.fori_loop(0, rows, issue, 0, unroll=8)
        idx_copy(1, 1).start()

    @pl.when(b < n_used)
    def _():
        idx_copy(0, nxt).wait()
        drain_rows(slot)
        issued = [0]

        def issue_next(n):
            for r in range(issued[0], issued[0] + n):
                row_copy(idx_smem[nxt, r], r, nxt).start(priority=r % 2)
            issued[0] += n

        x = xbuf[slot].astype(BF16)
        half = MXU_DIM // 2
        n_up = wu_ref.shape[2] // MXU_DIM
        n_down = 4
        per_up = (rows // (n_up + n_down) + 7) // 8 * 8
        acts = []
        for m in range(n_up):
            issue_next(per_up)
            cols = slice(m * MXU_DIM, (m + 1) * MXU_DIM)
            hu = _dot(x, wu_ref[0, :, cols]) + bu_ref[0, :, cols]
            g = jnp.minimum(hu[:, :half], SWIGLU_LIMIT)
            lin = jnp.clip(hu[:, half:], -SWIGLU_LIMIT, SWIGLU_LIMIT)
            acts.append((g * _sigmoid(SWIGLU_ALPHA * g) * (lin + 1.0)).astype(BF16))
        act = jnp.concatenate(acts, axis=1)
        dn = o_ref.shape[1] // n_down
        per_down = (rows - n_up * per_up) // n_down
        for m in range(n_down):
            issue_next(per_down if m < n_down - 1 else rows - issued[0])
            cols = slice(m * dn, (m + 1) * dn)
            o_ref[:, cols] = _dot(act, wd_ref[0, :, cols]) + bd_ref[0, :, cols]
        idx_copy(b + 2, slot).start()

    @pl.when(b >= n_used)
    def _():
        o_ref[...] = jnp.zeros_like(o_ref)

    @pl.when(b == n_used)
    def _():
        idx_copy(0, nxt).wait()
        drain_rows(slot)

    @pl.when((b == nb - 1) & (b < n_used))
    def _():
        idx_copy(0, slot).wait()
        drain_rows(nxt)


def _experts(h2, slot_tok, block_e, n_used, wu, bu, wd, bd, layer):
    d = h2.shape[1]
    rows = MOE_ROW_BLOCK
    nb = slot_tok.shape[0] // rows
    de = D_EXPERT
    idx = jnp.concatenate([slot_tok, jnp.zeros((2 * rows,), jnp.int32)]).reshape(nb + 2, rows)

    def wmap(b, be, nu):
        return (layer * N_EXPERTS + be[b], 0, 0)

    grid_spec = pltpu.PrefetchScalarGridSpec(
        num_scalar_prefetch=2,
        grid=(nb,),
        in_specs=[
            pl.BlockSpec(memory_space=pl.ANY),
            pl.BlockSpec(memory_space=pl.ANY),
            pl.BlockSpec((1, d, 2 * de), wmap),
            pl.BlockSpec((1, 1, 2 * de), wmap),
            pl.BlockSpec((1, de, d), wmap),
            pl.BlockSpec((1, 1, d), wmap),
        ],
        out_specs=pl.BlockSpec((rows, d), lambda b, be, nu: (b, 0)),
        scratch_shapes=[pltpu.VMEM((2, rows, d), F32), pltpu.SMEM((2, rows), jnp.int32),
                        pltpu.SemaphoreType.DMA((2,)), pltpu.SemaphoreType.DMA((2,))],
    )
    return pl.pallas_call(
        functools.partial(_expert_kernel, rows=rows, nb=nb),
        grid_spec=grid_spec,
        out_shape=jax.ShapeDtypeStruct((nb * rows, d), F32),
        compiler_params=_cp(("arbitrary",)),
        name="experts",
    )(block_e, n_used, idx, h2, wu, bu, wd, bd)


def _combine_kernel(idx_hbm, yb_hbm, r_ref, x_ref, nw_ref, g_ref, o_ref, ybuf, idx_smem, isems, rsems, *, tm, nt):
    i = pl.program_id(0)
    n_rows = TOP_K * tm

    def idx_copy(tile, s):
        return pltpu.make_async_copy(idx_hbm.at[tile], idx_smem.at[s], isems.at[s])

    def row_copy(r, s):
        return pltpu.make_async_copy(yb_hbm.at[pl.ds(idx_smem[s, r], 1), :], ybuf.at[s, pl.ds(r, 1), :], rsems.at[s])

    def issue_rows(s, unrolled):
        if unrolled:
            for r in range(n_rows):
                row_copy(r, s).start(priority=r % 2)
            return

        def body(r2, carry):
            for u in range(2):
                row_copy(2 * r2 + u, s).start(priority=u)
            return carry
        lax.fori_loop(0, n_rows // 2, body, 0, unroll=4)

    def wait_rows(s):
        pltpu.make_async_copy(yb_hbm.at[pl.ds(0, n_rows), :], ybuf.at[s], rsems.at[s]).wait()

    def tile(slot):
        nxt = 1 - slot

        @pl.when(i + 1 < nt)
        def _():
            idx_copy(0, nxt).wait()
            issue_rows(nxt, True)

        @pl.when(i + 2 < nt)
        def _():
            idx_copy(i + 2, slot).start()

        wait_rows(slot)
        gates = r_ref[...]
        y = None
        for k in range(TOP_K):
            term = gates[:, k:k + 1] * ybuf[slot, k * tm:(k + 1) * tm, :]
            y = term if y is None else y + term
        o_ref[...] = x_ref[...] + g_ref[0] * _rms_rows(y, nw_ref[0])

    @pl.when(i == 0)
    def _():
        idx_copy(0, 0).start()
        idx_copy(0, 0).wait()
        issue_rows(0, False)

        @pl.when(nt > 1)
        def _():
            idx_copy(1, 1).start()

    @pl.when(i % 2 == 0)
    def _():
        tile(0)

    @pl.when(i % 2 == 1)
    def _():
        tile(1)


def _combine(yb, inv_slot, route, x2, nw3, mod3, seq, *, nw_idx, g_idx):
    t, d = x2.shape
    tm = min(256, seq)
    nt = t // tm
    per_b = seq // tm
    idx_tiles = inv_slot.reshape(TOP_K, nt, tm).transpose(1, 0, 2).reshape(nt, TOP_K * tm)
    return pl.pallas_call(
        functools.partial(_combine_kernel, tm=tm, nt=nt),
        grid=(nt,),
        in_specs=[
            pl.BlockSpec(memory_space=pl.ANY),
            pl.BlockSpec(memory_space=pl.ANY),
            pl.BlockSpec((tm, LANES), lambda i: (i, 0)),
            pl.BlockSpec((tm, d), lambda i: (i, 0)),
            pl.BlockSpec((1, 1, d), lambda i: (nw_idx, 0, 0)),
            pl.BlockSpec((1, 1, d), lambda i: (i // per_b, 0, g_idx)),
        ],
        out_specs=pl.BlockSpec((tm, d), lambda i: (i, 0)),
        out_shape=jax.ShapeDtypeStruct((t, d), F32),
        scratch_shapes=[pltpu.VMEM((2, TOP_K * tm, d), F32), pltpu.SMEM((2, TOP_K * tm), jnp.int32),
                        pltpu.SemaphoreType.DMA((2,)), pltpu.SemaphoreType.DMA((2,))],
        compiler_params=_cp(("arbitrary",)),
        name="combine",
    )(idx_tiles, yb, route, x2, nw3, mod3)


_W_IN_SEGMENTS = (
    (0, C_AQ, 3072),
    (3080, C_MO, 512),
    (3592, C_GQ, 1024),
    (4632, C_GR, 512),
    (5144, C_FQ, 1536),
    (6684, C_GATE, 8192),
)
_W_IN_SMALL = ((3072, SM_I, 8), (4616, SM_GA, GLA_RANK), (6680, SM_FF, BRANCH_HEADS))


def _reorder_kernel(w_ref, o_ref):
    rows = w_ref.shape[1]
    for src, dst, n in _W_IN_SEGMENTS:
        o_ref[0, :, dst:dst + n] = w_ref[0, :, src:src + n].astype(BF16)
    r = lax.broadcasted_iota(jnp.int32, (LANES, LANES), 0)
    c = lax.broadcasted_iota(jnp.int32, (LANES, LANES), 1)
    small = jnp.zeros((rows, LANES), F32)
    for src, lane0, n in _W_IN_SMALL:
        base = src // LANES * LANES
        sel = jnp.where((c >= lane0) & (c < lane0 + n) & (r == c - lane0 + (src - base)), 1.0, 0.0).astype(BF16)
        small = small + _dot(w_ref[0, :, base:base + LANES].astype(BF16), sel)
    o_ref[0, :, C_SMALL:C_SMALL + LANES] = small.astype(BF16)
    o_ref[0, :, C_SMALL + LANES:] = jnp.zeros((rows, N_PROJ - C_SMALL - LANES), BF16)


def _reorder_w_in(w_in):
    depth, d, n = w_in.shape
    tr = 256
    return pl.pallas_call(
        _reorder_kernel,
        grid=(depth, d // tr),
        in_specs=[pl.BlockSpec((1, tr, n), lambda l, i: (l, i, 0))],
        out_specs=pl.BlockSpec((1, tr, N_PROJ), lambda l, i: (l, i, 0)),
        out_shape=jax.ShapeDtypeStruct((depth, d, N_PROJ), BF16),
        compiler_params=_cp(("parallel", "parallel")),
        name="reorder_w_in",
    )(w_in)


def _routing_tables(experts_t, t):
    tk = t * TOP_K
    rb = MOE_ROW_BLOCK
    flat_e = experts_t[:TOP_K].reshape(-1)
    flat_ids = jnp.arange(tk, dtype=jnp.int32)
    e_sorted, order = lax.sort((flat_e, flat_ids), num_keys=1, is_stable=True)
    experts = jnp.arange(N_EXPERTS, dtype=jnp.int32)
    counts = jnp.sum((flat_e[:, None] == experts[None, :]).astype(jnp.int32), axis=0)
    padded = (counts + rb - 1) // rb * rb
    pad_end = jnp.cumsum(padded)
    pad_start = pad_end - padded
    start = jnp.cumsum(counts) - counts
    slot_sorted = pad_start[e_sorted] + flat_ids - start[e_sorted]
    _, inv_slot = lax.sort((order, slot_sorted), num_keys=1)
    n_blocks = -(-tk // rb) + N_EXPERTS
    blk_first = jnp.arange(n_blocks, dtype=jnp.int32) * rb
    block_e = jnp.minimum(jnp.sum((pad_end[None, :] <= blk_first[:, None]).astype(jnp.int32), axis=1),
                          N_EXPERTS - 1).astype(jnp.int32)
    pos = jnp.arange(n_blocks * rb, dtype=jnp.int32)
    e_pos = jnp.repeat(block_e, rb)
    r_pos = pos - pad_start[e_pos]
    src = jnp.clip(start[e_pos] + r_pos, 0, tk - 1)
    slot_tok = jnp.where(r_pos < counts[e_pos], order[src] % t, 0).astype(jnp.int32)
    n_used = (pad_end[-1] // rb).astype(jnp.int32).reshape(1)
    return slot_tok, inv_slot.astype(jnp.int32), block_e, n_used


def _rope_tables(seq):
    half = HEAD_DIM // 2
    inv = ROPE_THETA ** (-jnp.arange(half, dtype=F32) / half)
    ang = jnp.arange(seq).astype(F32)[:, None] * inv[None, :]
    cos, sin = jnp.cos(ang), jnp.sin(ang)
    return jnp.concatenate([cos, cos], axis=-1), jnp.concatenate([-sin, sin], axis=-1)


def _deinterleave_bias(b_up):
    depth, e, n = b_up.shape
    half = MXU_DIM // 2
    return b_up.reshape(depth, e, n // MXU_DIM, half, 2).transpose(0, 1, 2, 4, 3).reshape(depth, e, 1, n)


def kernel(x, c, ada_w, ada_b, norm_w, w_in, conv_w, conv_b, mlstm_i_b, mlstm_f_b, mlstm_norm_w, gla_w_a2, gla_b_a2, gla_norm_w, fox_f_b, w_branch, w_out, router_w, router_b, w_up, b_up, w_down, b_down):
    batch, seq, d = x.shape
    depth = ada_w.shape[0]
    t = batch * seq
    x2 = x.reshape(t, d)
    mod = _ada_mod(c, ada_w, ada_b)
    cos, sin = _rope_tables(seq)
    w_up_d = _deinterleave_cast(w_up.reshape(depth * N_EXPERTS * d, 2 * D_EXPERT))
    w_up_d = w_up_d.reshape(depth * N_EXPERTS, d, 2 * D_EXPERT)
    b_up_d = _deinterleave_bias(b_up).reshape(depth * N_EXPERTS, 1, 2 * D_EXPERT)
    w_down_b = w_down.astype(BF16).reshape(depth * N_EXPERTS, D_EXPERT, d)
    b_down_r = b_down.reshape(depth * N_EXPERTS, 1, d)
    w_branch_b = w_branch.astype(BF16).reshape(depth * N_BRANCH, BRANCH_WIDTH, d)
    w_out_b = w_out.astype(BF16)
    w_in_r = _reorder_w_in(w_in)

    for l in range(depth):
        mod3 = mod[l].reshape(batch, 1, 6 * d)
        nw3 = norm_w[l].reshape(4, 1, d)

        proj = _in_proj(x2, mod3, nw3, w_in_r, l, seq, nw_idx=0, sc_idx=1, sh_idx=0)
        small_bias = jnp.zeros((1, LANES), F32)
        small_bias = small_bias.at[0, SM_I:SM_I + BRANCH_HEADS].set(mlstm_i_b[l])
        small_bias = small_bias.at[0, SM_F:SM_F + BRANCH_HEADS].set(mlstm_f_b[l])
        small_bias = small_bias.at[0, SM_FF:SM_FF + BRANCH_HEADS].set(fox_f_b[l])
        w_a2_pad = jnp.zeros((LANES, GLA_HEADS * GLA_DK), F32).at[SM_GA:SM_GA + GLA_RANK].set(gla_w_a2[l])

        y_a = _moba(proj, cos, sin, batch, seq)
        y_b = _mlstm(proj, small_bias, conv_w[l], conv_b[l], mlstm_norm_w[l], batch, seq)
        y_c = _gla(proj, w_a2_pad, gla_b_a2[l], gla_norm_w[l], batch, seq)
        y_d = _fox(proj, small_bias, batch, seq)
        merged = _merge((y_a, y_b, y_c, y_d), proj, w_branch_b, l)
        x2 = _out_proj(merged, w_out_b, l, x2, nw3, mod3, seq, nw_idx=1, g_idx=2)

        rw_pad = jnp.zeros((d, LANES), F32).at[:, :N_EXPERTS].set(router_w[l])
        rb_pad = jnp.zeros((1, LANES), F32).at[0, :N_EXPERTS].set(router_b[l])
        h2, route, experts_t = _router(x2, nw3, mod3, rw_pad, rb_pad, seq, nw_idx=2, sc_idx=4, sh_idx=3)
        slot_tok, inv_slot, block_e, n_used = _routing_tables(experts_t, t)
        yb = _experts(h2, slot_tok, block_e, n_used, w_up_d, b_up_d, w_down_b, b_down_r, l)
        x2 = _combine(yb, inv_slot, route, x2, nw3, mod3, seq, nw_idx=3, g_idx=5)

    return x2.reshape(batch, seq, d)
```

```python
import functools

import jax
import jax.numpy as jnp
from jax import lax
from jax.experimental import pallas as pl
from jax.experimental.pallas import tpu as pltpu

F32 = jnp.float32
BF16 = jnp.bfloat16
NEG_INF = float("-inf")
MASK_PENALTY = -1e9

D_MODEL = 2048
HEAD_DIM = 128
N_BRANCH = 4
BRANCH_HEADS = 4
BRANCH_WIDTH = BRANCH_HEADS * HEAD_DIM

MOBA_BLOCK = 256
MOBA_TOPK = 3
ROPE_THETA = 10000.0

MLSTM_CHUNK = 128
MLSTM_CONV = 4

GLA_HEADS = 4
GLA_DK = 64
GLA_DV = 128
GLA_RANK = 16
GLA_TAU = 16.0
GLA_CHUNK = 64
GLA_SUB = 16

N_EXPERTS = 32
TOP_K = 4
D_EXPERT = D_MODEL // 2
SWIGLU_LIMIT = 7.0
SWIGLU_ALPHA = 1.702
MOE_ROW_BLOCK = 512

RMS_EPS = 1e-6

LANES = 128
MXU_DIM = 256
CHUNK = 128
ATT_TILE = 512

C_AQ, C_AK, C_AV = 0, 512, 1024
C_MQ, C_MK, C_MV, C_MO = 1536, 2048, 2560, 3072
C_GQ, C_GK, C_GV, C_GR = 3584, 3840, 4096, 4608
C_FQ, C_FK, C_FV = 5120, 5632, 6144
C_GATE = 6656
C_SMALL = 14848
SM_I, SM_F, SM_GA, SM_FF = 0, 4, 8, 24
N_PROJ = 15360

NN = (((1,), (0,)), ((), ()))
NT = (((1,), (1,)), ((), ()))
TN = (((0,), (0,)), ((), ()))

VMEM_LIMIT = 56 * 1024 * 1024


def _cp(sem, vmem=VMEM_LIMIT):
    return pltpu.CompilerParams(dimension_semantics=sem, vmem_limit_bytes=vmem)


def _dot(a, b, dn=NN):
    return lax.dot_general(a, b, dn, preferred_element_type=F32)


def _split3(x):
    hi = x.astype(BF16)
    r1 = x - hi.astype(F32)
    mid = r1.astype(BF16)
    lo = (r1 - mid.astype(F32)).astype(BF16)
    return hi, mid, lo


def _dot01(sel, x, dn=NN):
    hi, mid, lo = _split3(x)
    return _dot(sel, hi, dn) + _dot(sel, mid, dn) + _dot(sel, lo, dn)


def _dot_x3(a, b, dn=NN):
    ah = a.astype(BF16)
    al = (a - ah.astype(F32)).astype(BF16)
    bh = b.astype(BF16)
    bl = (b - bh.astype(F32)).astype(BF16)
    return _dot(ah, bh, dn) + _dot(al, bh, dn) + _dot(ah, bl, dn)


def _log_sigmoid(x):
    return jnp.minimum(x, 0.0) - jnp.log1p(jnp.exp(-jnp.abs(x)))


def _sigmoid(x):
    return 0.5 * jnp.tanh(0.5 * x) + 0.5


def _tril_bf16(n):
    r = lax.broadcasted_iota(jnp.int32, (n, n), 0)
    c = lax.broadcasted_iota(jnp.int32, (n, n), 1)
    return jnp.where(c <= r, 1.0, 0.0).astype(BF16)


def _row_selector(lane_idx):
    r = lax.broadcasted_iota(jnp.int32, (8, LANES), 0)
    c = lax.broadcasted_iota(jnp.int32, (8, LANES), 1)
    return jnp.where((r == 0) & (c == lane_idx), 1.0, 0.0).astype(BF16)


def _pick_lane(x, lane_idx):
    lane = lax.broadcasted_iota(jnp.int32, x.shape, 1)
    return jnp.sum(jnp.where(lane == lane_idx, x, 0.0), axis=-1, keepdims=True)


def _ada_kernel(c_ref, w_ref, b_ref, o_ref):
    c = c_ref[...]
    cond = c * _sigmoid(c)
    o_ref[0] = _dot(cond.astype(BF16), w_ref[0].astype(BF16)) + b_ref[0]


def _ada_mod(c, ada_w, ada_b):
    depth, d, n = ada_w.shape
    b = c.shape[0]
    tn = 1024
    return pl.pallas_call(
        _ada_kernel,
        grid=(depth, n // tn),
        in_specs=[
            pl.BlockSpec((b, d), lambda l, j: (0, 0)),
            pl.BlockSpec((1, d, tn), lambda l, j: (l, 0, j)),
            pl.BlockSpec((1, 1, tn), lambda l, j: (l, 0, j)),
        ],
        out_specs=pl.BlockSpec((1, b, tn), lambda l, j: (l, 0, j)),
        out_shape=jax.ShapeDtypeStruct((depth, b, n), F32),
        compiler_params=_cp(("parallel", "parallel")),
        name="ada_mod",
    )(c, ada_w, ada_b.reshape(depth, 1, n))


def _norm_mod_rows(x, nw, sc, sh):
    var = jnp.mean(x * x, axis=-1, keepdims=True)
    y = x * lax.rsqrt(var + RMS_EPS) * nw
    return y * (1.0 + sc) + sh


def _rms_rows(y, nw):
    var = jnp.mean(y * y, axis=-1, keepdims=True)
    return y * lax.rsqrt(var + RMS_EPS) * nw


def _inproj_kernel(x_ref, nw_ref, sc_ref, sh_ref, w_ref, o_ref, h_scr, *, rows):
    @pl.when(pl.program_id(1) == 0)
    def _():
        def body(r, carry):
            sl = pl.ds(pl.multiple_of(r * rows, rows), rows)
            h = _norm_mod_rows(x_ref[sl, :], nw_ref[0], sc_ref[0], sh_ref[0])
            h_scr[sl, :] = h.astype(BF16)
            return carry
        lax.fori_loop(0, x_ref.shape[0] // rows, body, 0)

    o_ref[...] = _dot(h_scr[...], w_ref[0])


def _in_proj(x2, mod3, nw3, w_re, layer, seq, *, nw_idx, sc_idx, sh_idx):
    t, d = x2.shape
    n = w_re.shape[2]
    tm = min(1024, seq)
    tn = 1024
    per_b = seq // tm
    return pl.pallas_call(
        functools.partial(_inproj_kernel, rows=min(256, tm)),
        grid=(t // tm, n // tn),
        in_specs=[
            pl.BlockSpec((tm, d), lambda i, j: (i, 0)),
            pl.BlockSpec((1, 1, d), lambda i, j: (nw_idx, 0, 0)),
            pl.BlockSpec((1, 1, d), lambda i, j: (i // per_b, 0, sc_idx)),
            pl.BlockSpec((1, 1, d), lambda i, j: (i // per_b, 0, sh_idx)),
            pl.BlockSpec((1, d, tn), lambda i, j: (layer, 0, j)),
        ],
        out_specs=pl.BlockSpec((tm, tn), lambda i, j: (i, j)),
        out_shape=jax.ShapeDtypeStruct((t, n), F32),
        scratch_shapes=[pltpu.VMEM((tm, d), BF16)],
        compiler_params=_cp(("parallel", "arbitrary")),
        name="in_proj",
    )(x2, nw3, mod3, mod3, w_re)


ATT_HEADS = 2


def _attention_sweep(i, tile, q_augs, kaug_scr, vb_scr, o_ref):
    nh = len(q_augs)
    rows_q = pl.ds(pl.multiple_of(i * tile, tile), tile)
    r_io = lax.broadcasted_iota(jnp.int32, (tile, tile), 0)
    c_io = lax.broadcasted_iota(jnp.int32, (tile, tile), 1)
    init = []
    for hh in range(nh):
        s = _dot(q_augs[hh], kaug_scr[hh, rows_q, :], NT)
        s = jnp.where(c_io <= r_io, s, NEG_INF)
        m0 = jnp.max(s, axis=-1, keepdims=True)
        p = jnp.exp(s - m0)
        init += [m0, jnp.sum(p, axis=-1, keepdims=True), _dot(p.astype(BF16), vb_scr[hh, rows_q, :])]

    def body(j, carry):
        rows = pl.ds(pl.multiple_of(j * tile, tile), tile)
        out = []
        for hh in range(nh):
            m, l, acc = carry[3 * hh:3 * hh + 3]
            s = _dot(q_augs[hh], kaug_scr[hh, rows, :], NT)
            m_new = jnp.maximum(m, jnp.max(s, axis=-1, keepdims=True))
            alpha = jnp.exp(m - m_new)
            p = jnp.exp(s - m_new)
            l = alpha * l + jnp.sum(p, axis=-1, keepdims=True)
            acc = alpha * acc + _dot(p.astype(BF16), vb_scr[hh, rows, :])
            out += [m_new, l, acc]
        return tuple(out)

    fin = lax.fori_loop(0, i, body, tuple(init))
    for hh in range(nh):
        o_ref[:, hh * HEAD_DIM:(hh + 1) * HEAD_DIM] = (fin[3 * hh + 2] / fin[3 * hh + 1]).astype(o_ref.dtype)


def _rope(x, cos, sin):
    return x * cos + pltpu.roll(x, HEAD_DIM // 2, 1) * sin


def _moba_kernel(q_ref, k_ref, v_ref, cos_ref, sin_ref, o_ref, kaug_scr, vb_scr, km_scr, *, seq, tile, n_sel):
    blk = MOBA_BLOCK
    nblk = seq // blk
    nb8 = -(-nblk // 8) * 8
    i = pl.program_id(2)

    @pl.when(i == 0)
    def _():
        km_scr[...] = jnp.zeros_like(km_scr)
        lane_b = lax.broadcasted_iota(jnp.int32, (blk, LANES), 1)
        for hh in range(ATT_HEADS):
            cols = slice(hh * HEAD_DIM, (hh + 1) * HEAD_DIM)
            for j in range(nblk):
                rows = pl.ds(j * blk, blk)
                kk = _rope(k_ref[rows, cols], cos_ref[rows, :], sin_ref[rows, :])
                kaug_scr[hh, rows, 0:HEAD_DIM] = kk.astype(BF16)
                kaug_scr[hh, rows, HEAD_DIM:2 * HEAD_DIM] = jnp.where(lane_b == j, 1.0, 0.0).astype(BF16)
                km_scr[hh, j:j + 1, :] = jnp.mean(kk, axis=0, keepdims=True)
                vb_scr[hh, rows, :] = v_ref[rows, cols].astype(BF16)

    rows_q = pl.ds(pl.multiple_of(i * tile, tile), tile)
    blk_id = lax.broadcasted_iota(jnp.int32, (nb8, tile), 0)
    blk_f = blk_id.astype(F32)
    q_blk = i * (tile // blk) + lax.broadcasted_iota(jnp.int32, (nb8, tile), 1) // blk
    q_augs = []
    for hh in range(ATT_HEADS):
        cols = slice(hh * HEAD_DIM, (hh + 1) * HEAD_DIM)
        q = _rope(q_ref[:, cols], cos_ref[rows_q, :], sin_ref[rows_q, :])
        g = _dot_x3(km_scr[hh], q, NT)[0:nb8, :]
        g = jnp.where(blk_id < q_blk, g, NEG_INF)
        keep = blk_id == q_blk
        for _ in range(n_sel):
            m = jnp.max(g, axis=0, keepdims=True)
            idx = jnp.min(jnp.where(g == m, blk_f, float(LANES)), axis=0, keepdims=True)
            hit = blk_f == idx
            keep = keep | (hit & (m > NEG_INF))
            g = jnp.where(hit, NEG_INF, g)
        pen_t = jnp.where(keep, 0.0, MASK_PENALTY)
        pen_t = jnp.concatenate([pen_t, jnp.full((LANES - nb8, tile), MASK_PENALTY, F32)], axis=0)
        penalty = pen_t.T
        q_augs.append(jnp.concatenate([(q * (HEAD_DIM ** -0.5)).astype(BF16), penalty.astype(BF16)], axis=1))
    _attention_sweep(i, tile, q_augs, kaug_scr, vb_scr, o_ref)


def _moba(proj, cos, sin, batch, seq):
    tile = min(ATT_TILE, seq)
    nq = seq // tile
    nblk = seq // MOBA_BLOCK
    n_sel = max(1, min(MOBA_TOPK, nblk - 1))
    hb = HEAD_DIM
    hw = ATT_HEADS * hb
    return pl.pallas_call(
        functools.partial(_moba_kernel, seq=seq, tile=tile, n_sel=n_sel),
        grid=(batch, BRANCH_HEADS // ATT_HEADS, nq),
        in_specs=[
            pl.BlockSpec((tile, hw), lambda b, p, i: (b * nq + i, C_AQ // hw + p)),
            pl.BlockSpec((seq, hw), lambda b, p, i: (b, C_AK // hw + p)),
            pl.BlockSpec((seq, hw), lambda b, p, i: (b, C_AV // hw + p)),
            pl.BlockSpec((seq, hb), lambda b, p, i: (0, 0)),
            pl.BlockSpec((seq, hb), lambda b, p, i: (0, 0)),
        ],
        out_specs=pl.BlockSpec((tile, hw), lambda b, p, i: (b * nq + i, p)),
        out_shape=jax.ShapeDtypeStruct((batch * seq, BRANCH_WIDTH), BF16),
        scratch_shapes=[pltpu.VMEM((ATT_HEADS, seq, 2 * hb), BF16), pltpu.VMEM((ATT_HEADS, seq, hb), BF16),
                        pltpu.VMEM((ATT_HEADS, LANES, hb), F32)],
        compiler_params=_cp(("parallel", "parallel", "arbitrary")),
        name="moba",
    )(proj, proj, proj, cos, sin)


def _fox_kernel(q_ref, k_ref, v_ref, sm_ref, bias_ref, o_ref, kaug_scr, vb_scr, fcol_scr, *, seq, tile):
    p_idx = pl.program_id(1)
    i = pl.program_id(2)

    def bias_lanes(f, first):
        hi, mid, lo = _split3(f)
        lane = lax.broadcasted_iota(jnp.int32, (f.shape[0], LANES), 1)
        ones_first = 3 - first
        out = jnp.where((lane >= ones_first) & (lane < ones_first + 3), 1.0, 0.0)
        out = jnp.where(lane == first, hi.astype(F32), out)
        out = jnp.where(lane == first + 1, mid.astype(F32), out)
        out = jnp.where(lane == first + 2, lo.astype(F32), out)
        return out.astype(BF16)

    @pl.when(i == 0)
    def _():
        tril = _tril_bf16(CHUNK)
        carry = jnp.zeros((1, LANES), F32)
        for c in range(seq // CHUNK):
            rows = pl.ds(c * CHUNK, CHUNK)
            lf = _log_sigmoid(sm_ref[rows, :] + bias_ref[...])
            cs = _dot01(tril, lf) + carry
            fcol_scr[rows, :] = cs
            carry = cs[CHUNK - 1:CHUNK, :]
        for hh in range(ATT_HEADS):
            cols = slice(hh * HEAD_DIM, (hh + 1) * HEAD_DIM)
            f_lane = SM_FF + p_idx * ATT_HEADS + hh
            for c in range(seq // tile):
                rows = pl.ds(c * tile, tile)
                kaug_scr[hh, rows, 0:HEAD_DIM] = k_ref[rows, cols].astype(BF16)
                kaug_scr[hh, rows, HEAD_DIM:2 * HEAD_DIM] = bias_lanes(-_pick_lane(fcol_scr[rows, :], f_lane), 3)
                vb_scr[hh, rows, :] = v_ref[rows, cols].astype(BF16)

    rows_q = pl.ds(pl.multiple_of(i * tile, tile), tile)
    fq = fcol_scr[rows_q, :]
    q_augs = []
    for hh in range(ATT_HEADS):
        cols = slice(hh * HEAD_DIM, (hh + 1) * HEAD_DIM)
        f_t = _pick_lane(fq, SM_FF + p_idx * ATT_HEADS + hh)
        q_augs.append(jnp.concatenate([(q_ref[:, cols] * (HEAD_DIM ** -0.5)).astype(BF16), bias_lanes(f_t, 0)], axis=1))
    _attention_sweep(i, tile, q_augs, kaug_scr, vb_scr, o_ref)


def _fox(proj, small_bias, batch, seq):
    tile = min(ATT_TILE, seq)
    nq = seq // tile
    hb = HEAD_DIM
    hw = ATT_HEADS * hb
    return pl.pallas_call(
        functools.partial(_fox_kernel, seq=seq, tile=tile),
        grid=(batch, BRANCH_HEADS // ATT_HEADS, nq),
        in_specs=[
            pl.BlockSpec((tile, hw), lambda b, p, i: (b * nq + i, C_FQ // hw + p)),
            pl.BlockSpec((seq, hw), lambda b, p, i: (b, C_FK // hw + p)),
            pl.BlockSpec((seq, hw), lambda b, p, i: (b, C_FV // hw + p)),
            pl.BlockSpec((seq, LANES), lambda b, p, i: (b, C_SMALL // LANES)),
            pl.BlockSpec((1, LANES), lambda b, p, i: (0, 0)),
        ],
        out_specs=pl.BlockSpec((tile, hw), lambda b, p, i: (b * nq + i, p)),
        out_shape=jax.ShapeDtypeStruct((batch * seq, BRANCH_WIDTH), BF16),
        scratch_shapes=[pltpu.VMEM((ATT_HEADS, seq, 2 * hb), BF16), pltpu.VMEM((ATT_HEADS, seq, hb), BF16),
                        pltpu.VMEM((seq, LANES), F32)],
        compiler_params=_cp(("parallel", "parallel", "arbitrary")),
        name="fox",
    )(proj, proj, proj, proj, small_bias)


def _conv_silu_rows(x_ref, w_ref, b_ref, r0, n):
    k = w_ref.shape[0]
    x0 = x_ref[r0:r0 + n, :]
    y = x0 * w_ref[k - 1:k, :] + b_ref[...]
    for sft in range(1, k):
        if r0 >= sft:
            xs = x_ref[r0 - sft:r0 - sft + n, :]
        else:
            row = lax.broadcasted_iota(jnp.int32, x0.shape, 0)
            xs = jnp.where(row >= sft, pltpu.roll(x0, sft, 0), 0.0)
        y = y + xs * w_ref[k - 1 - sft:k - sft, :]
    return y * _sigmoid(y)


MLSTM_UNROLL = 16


def _mlstm_kernel(q_ref, k_ref, v_ref, o_ref, sm_ref, bias_ref, cwq_ref, cwk_ref, cbq_ref, cbk_ref,
                  nw_ref, y_ref, q_scr, k_scr, x_scr, g_scr, irow_scr, grow_scr, *, seq):
    h = pl.program_id(1)
    L = MLSTM_CHUNK
    nc = seq // L
    unroll = MLSTM_UNROLL if nc % MLSTM_UNROLL == 0 else (nc if nc < MLSTM_UNROLL else 1)
    i_lane = SM_I + h
    f_lane = SM_F + h

    tril = _tril_bf16(L)
    sel_i = _row_selector(i_lane)
    sel_f = _row_selector(f_lane)
    lane_s = lax.broadcasted_iota(jnp.int32, (L, LANES), 1)
    for c in range(nc):
        r0 = c * L
        rows = pl.ds(r0, L)
        q_scr[rows, :] = _conv_silu_rows(q_ref, cwq_ref, cbq_ref, r0, L)
        k_scr[rows, :] = _conv_silu_rows(k_ref, cwk_ref, cbk_ref, r0, L) * (HEAD_DIM ** -0.5)
        smb = sm_ref[rows, :] + bias_ref[...]
        xc = jnp.where(lane_s < SM_F, smb, _log_sigmoid(smb))
        gc = _dot01(tril, xc)
        x_scr[rows, :] = xc
        g_scr[rows, :] = gc
        irow_scr[:, r0:r0 + L] = _dot01(sel_i, xc, NT)
        grow_scr[:, r0:r0 + L] = _dot01(sel_f, gc, NT)

    r_io = lax.broadcasted_iota(jnp.int32, (L, L), 0)
    c_io = lax.broadcasted_iota(jnp.int32, (L, L), 1)
    causal = c_io <= r_io

    def chunk(c, carry):
        C, n, m = carry
        rows = pl.ds(pl.multiple_of(c * L, L), L)
        qc = q_scr[rows, :]
        kc = k_scr[rows, :]
        vb = v_ref[rows, :].astype(BF16)
        qb = qc.astype(BF16)
        g_col = _pick_lane(g_scr[rows, :], f_lane)
        i_col = _pick_lane(x_scr[rows, :], i_lane)
        g_row = grow_scr[0:1, rows]
        i_row = irow_scr[0:1, rows]
        dmat = jnp.where(causal, g_col - g_row + i_row, NEG_INF)
        a = g_col + m
        m_out = jnp.maximum(a, jnp.max(dmat, axis=-1, keepdims=True))
        w = jnp.exp(dmat - m_out) * _dot(qb, kc.astype(BF16), NT)
        inter = jnp.exp(a - m_out)
        num = inter * _dot(qb, C.astype(BF16)) + _dot(w.astype(BF16), vb)
        den = inter * jnp.sum(qc * n, axis=-1, keepdims=True) + jnp.sum(w, axis=-1, keepdims=True)
        hh = num / jnp.maximum(jnp.abs(den), jnp.exp(-m_out))
        g_last = g_row[:, L - 1:L]
        lw = g_last - g_col + i_col
        m_new = jnp.maximum(g_last + m, jnp.max(lw, axis=0, keepdims=True))
        wk = jnp.exp(lw - m_new)
        decay = jnp.exp(g_last + m - m_new)
        kw = kc * wk
        C = decay * C + _dot(kw.astype(BF16), vb, TN)
        n = decay * n + jnp.sum(kw, axis=0, keepdims=True)
        yn = _rms_rows(hh, nw_ref[0])
        y_ref[rows, :] = (yn * _sigmoid(o_ref[rows, :])).astype(y_ref.dtype)
        return C, n, m_new

    def step(it, carry):
        for u in range(unroll):
            carry = chunk(it * unroll + u, carry)
        return carry

    init = (jnp.zeros((HEAD_DIM, HEAD_DIM), F32), jnp.zeros((1, HEAD_DIM), F32), jnp.zeros((1, 1), F32))
    lax.fori_loop(0, nc // unroll, step, init)


def _mlstm(proj, small_bias, conv_w, conv_b, norm_w, batch, seq):
    hb = HEAD_DIM
    nh = BRANCH_HEADS
    blk = lambda c0: pl.BlockSpec((seq, hb), lambda b, h: (b, c0 // hb + h))
    return pl.pallas_call(
        functools.partial(_mlstm_kernel, seq=seq),
        grid=(batch, nh),
        in_specs=[
            blk(C_MQ), blk(C_MK), blk(C_MV), blk(C_MO),
            pl.BlockSpec((seq, LANES), lambda b, h: (b, C_SMALL // LANES)),
            pl.BlockSpec((1, LANES), lambda b, h: (0, 0)),
            pl.BlockSpec((MLSTM_CONV, hb), lambda b, h: (0, h)),
            pl.BlockSpec((MLSTM_CONV, hb), lambda b, h: (0, nh + h)),
            pl.BlockSpec((1, hb), lambda b, h: (0, h)),
            pl.BlockSpec((1, hb), lambda b, h: (0, nh + h)),
            pl.BlockSpec((1, 1, hb), lambda b, h: (h, 0, 0)),
        ],
        out_specs=pl.BlockSpec((seq, hb), lambda b, h: (b, h)),
        out_shape=jax.ShapeDtypeStruct((batch * seq, BRANCH_WIDTH), BF16),
        scratch_shapes=[
            pltpu.VMEM((seq, hb), F32), pltpu.VMEM((seq, hb), F32),
            pltpu.VMEM((seq, LANES), F32), pltpu.VMEM((seq, LANES), F32),
            pltpu.VMEM((8, seq), F32), pltpu.VMEM((8, seq), F32),
        ],
        compiler_params=_cp(("parallel", "parallel")),
        name="mlstm",
    )(proj, proj, proj, proj, proj, small_bias, conv_w, conv_w, conv_b.reshape(1, -1), conv_b.reshape(1, -1),
      norm_w.reshape(nh, 1, hb))


GLA_UNROLL = 16


def _gla_kernel(q_ref, k_ref, v_ref, r_ref, sm_ref, wa_ref, ba_ref, nw_ref, y_ref, la_scr, *, seq):
    L = GLA_CHUNK
    sub = GLA_SUB
    nc = seq // L
    unroll = GLA_UNROLL if nc % GLA_UNROLL == 0 else (nc if nc < GLA_UNROLL else 1)
    dk, dv = GLA_DK, GLA_DV

    for c in range(seq // CHUNK):
        rows = pl.ds(c * CHUNK, CHUNK)
        z = _dot_x3(sm_ref[rows, :], wa_ref[...]) + ba_ref[...]
        la_scr[rows, :] = _log_sigmoid(z) * (1.0 / GLA_TAU)
    tril = _tril_bf16(L)
    r_io = lax.broadcasted_iota(jnp.int32, (sub, L), 0)
    c_io = lax.broadcasted_iota(jnp.int32, (sub, L), 1)
    row_l = lax.broadcasted_iota(jnp.int32, (L, dk), 0)

    def chunk(c, states):
        rows = pl.ds(pl.multiple_of(c * L, L), L)
        bc2 = _dot01(tril, la_scr[rows, :])
        q2 = q_ref[rows, :] * (dk ** -0.5)
        k2 = k_ref[rows, :]
        new_states = []
        for hh in range(2):
            st = states[hh]
            lanes = slice(hh * dk, (hh + 1) * dk)
            b = bc2[:, lanes]
            qh = q2[:, lanes]
            kh = k2[:, lanes]
            vb = v_ref[rows, hh * dv:(hh + 1) * dv].astype(BF16)
            o = _dot((qh * jnp.exp(b)).astype(BF16), st.astype(BF16), NT)
            blocks = []
            for ib in range(L // sub):
                ref_row = b[ib * sub:ib * sub + 1, :]
                qi = qh[ib * sub:(ib + 1) * sub, :] * jnp.exp(b[ib * sub:(ib + 1) * sub, :] - ref_row)
                ki = kh * jnp.exp(jnp.where(row_l < (ib + 1) * sub, ref_row - b, 0.0))
                a = _dot(qi.astype(BF16), ki.astype(BF16), NT)
                blocks.append(jnp.where(c_io <= r_io + ib * sub, a, 0.0))
            attn = jnp.concatenate(blocks, axis=0)
            o = o + _dot(attn.astype(BF16), vb)
            b_last = b[L - 1:L, :]
            ke = kh * jnp.exp(b_last - b)
            st = jnp.exp(b_last) * st + _dot(vb, ke.astype(BF16), TN)
            new_states.append(st)
            yn = _rms_rows(o, nw_ref[0, hh:hh + 1, :])
            r = r_ref[rows, hh * dv:(hh + 1) * dv]
            y_ref[rows, hh * dv:(hh + 1) * dv] = (yn * (r * _sigmoid(r))).astype(y_ref.dtype)
        return tuple(new_states)

    def step(it, states):
        for u in range(unroll):
            states = chunk(it * unroll + u, states)
        return states

    init = (jnp.zeros((dv, dk), F32), jnp.zeros((dv, dk), F32))
    lax.fori_loop(0, nc // unroll, step, init)


def _gla(proj, w_a2_pad, b_a2, norm_w, batch, seq):
    return pl.pallas_call(
        functools.partial(_gla_kernel, seq=seq),
        grid=(batch, GLA_HEADS // 2),
        in_specs=[
            pl.BlockSpec((seq, LANES), lambda b, p: (b, C_GQ // LANES + p)),
            pl.BlockSpec((seq, LANES), lambda b, p: (b, C_GK // LANES + p)),
            pl.BlockSpec((seq, 2 * GLA_DV), lambda b, p: (b, C_GV // (2 * GLA_DV) + p)),
            pl.BlockSpec((seq, 2 * GLA_DV), lambda b, p: (b, C_GR // (2 * GLA_DV) + p)),
            pl.BlockSpec((seq, LANES), lambda b, p: (b, C_SMALL // LANES)),
            pl.BlockSpec((LANES, LANES), lambda b, p: (0, p)),
            pl.BlockSpec((1, LANES), lambda b, p: (0, p)),
            pl.BlockSpec((1, 2, GLA_DV), lambda b, p: (p, 0, 0)),
        ],
        out_specs=pl.BlockSpec((seq, 2 * GLA_DV), lambda b, p: (b, p)),
        out_shape=jax.ShapeDtypeStruct((batch * seq, GLA_HEADS * GLA_DV), BF16),
        scratch_shapes=[pltpu.VMEM((seq, LANES), F32)],
        compiler_params=_cp(("parallel", "parallel")),
        name="gla",
    )(proj, proj, proj, proj, proj, w_a2_pad, b_a2.reshape(1, -1), norm_w.reshape(GLA_HEADS // 2, 2, GLA_DV))


def _merge_kernel(ya_ref, yb_ref, yc_ref, yd_ref, g0_ref, g1_ref, g2_ref, g3_ref, w_ref, o_ref):
    ys = (ya_ref, yb_ref, yc_ref, yd_ref)
    gs = (g0_ref, g1_ref, g2_ref, g3_ref)
    acc = None
    for b in range(N_BRANCH):
        term = _sigmoid(gs[b][...]) * _dot(ys[b][...], w_ref[b])
        acc = term if acc is None else acc + term
    o_ref[...] = acc.astype(o_ref.dtype)


def _merge(ys, proj, w_branch_bf16, layer):
    t = proj.shape[0]
    d = D_MODEL
    tm, tn = 1024, 512
    tm = min(tm, t)
    yspec = pl.BlockSpec((tm, BRANCH_WIDTH), lambda i, j: (i, 0))
    gspec = lambda b: pl.BlockSpec((tm, tn), lambda i, j: (i, (C_GATE + b * d) // tn + j))
    return pl.pallas_call(
        _merge_kernel,
        grid=(t // tm, d // tn),
        in_specs=[yspec] * 4 + [gspec(b) for b in range(N_BRANCH)]
        + [pl.BlockSpec((N_BRANCH, BRANCH_WIDTH, tn), lambda i, j: (layer, 0, j))],
        out_specs=pl.BlockSpec((tm, tn), lambda i, j: (i, j)),
        out_shape=jax.ShapeDtypeStruct((t, d), BF16),
        compiler_params=_cp(("parallel", "parallel")),
        name="merge",
    )(*ys, proj, proj, proj, proj, w_branch_bf16)


def _outproj_kernel(m_ref, w_ref, x_ref, nw_ref, g_ref, o_ref):
    y = _dot(m_ref[...], w_ref[0])
    o_ref[...] = x_ref[...] + g_ref[0] * _rms_rows(y, nw_ref[0])


def _out_proj(merged, w_out_bf16, layer, x2, nw3, mod3, seq, *, nw_idx, g_idx):
    t, d = x2.shape
    tm = min(512, seq)
    per_b = seq // tm
    return pl.pallas_call(
        _outproj_kernel,
        grid=(t // tm,),
        in_specs=[
            pl.BlockSpec((tm, d), lambda i: (i, 0)),
            pl.BlockSpec((1, d, d), lambda i: (layer, 0, 0)),
            pl.BlockSpec((tm, d), lambda i: (i, 0)),
            pl.BlockSpec((1, 1, d), lambda i: (nw_idx, 0, 0)),
            pl.BlockSpec((1, 1, d), lambda i: (i // per_b, 0, g_idx)),
        ],
        out_specs=pl.BlockSpec((tm, d), lambda i: (i, 0)),
        out_shape=jax.ShapeDtypeStruct((t, d), F32),
        compiler_params=_cp(("parallel",)),
        name="out_proj",
    )(merged, w_out_bf16, x2, nw3, mod3)


def _router_kernel(x_ref, nw_ref, sc_ref, sh_ref, rw_ref, rb_ref, h_ref, r_ref, et_ref):
    h = _norm_mod_rows(x_ref[...], nw_ref[0], sc_ref[0], sh_ref[0])
    h_ref[...] = h
    logits = _dot_x3(h, rw_ref[...]) + rb_ref[...]
    lane = lax.broadcasted_iota(jnp.int32, logits.shape, 1)
    lane_f = lane.astype(F32)
    l = jnp.where(lane < N_EXPERTS, logits, NEG_INF)
    vals, idxs = [], []
    for _ in range(TOP_K):
        m = jnp.max(l, axis=-1, keepdims=True)
        idx = jnp.min(jnp.where(l == m, lane_f, float(LANES)), axis=-1, keepdims=True)
        vals.append(m)
        idxs.append(idx)
        l = jnp.where(lane_f == idx, NEG_INF, l)
    es = [jnp.exp(v - vals[0]) for v in vals]
    tot = es[0]
    for e in es[1:]:
        tot = tot + e
    out = jnp.zeros(logits.shape, F32)
    for r in range(TOP_K):
        out = jnp.where(lane == r, es[r] / tot, out)
        out = jnp.where(lane == TOP_K + r, idxs[r], out)
    r_ref[...] = out
    et_ref[...] = out.T[TOP_K:TOP_K + 8, :].astype(jnp.int32)


def _router(x2, nw3, mod3, router_w_pad, router_b_pad, seq, *, nw_idx, sc_idx, sh_idx):
    t, d = x2.shape
    tm = min(256, seq)
    per_b = seq // tm
    return pl.pallas_call(
        _router_kernel,
        grid=(t // tm,),
        in_specs=[
            pl.BlockSpec((tm, d), lambda i: (i, 0)),
            pl.BlockSpec((1, 1, d), lambda i: (nw_idx, 0, 0)),
            pl.BlockSpec((1, 1, d), lambda i: (i // per_b, 0, sc_idx)),
            pl.BlockSpec((1, 1, d), lambda i: (i // per_b, 0, sh_idx)),
            pl.BlockSpec((d, LANES), lambda i: (0, 0)),
            pl.BlockSpec((1, LANES), lambda i: (0, 0)),
        ],
        out_specs=[pl.BlockSpec((tm, d), lambda i: (i, 0)), pl.BlockSpec((tm, LANES), lambda i: (i, 0)),
                   pl.BlockSpec((8, tm), lambda i: (0, i))],
        out_shape=[jax.ShapeDtypeStruct((t, d), F32), jax.ShapeDtypeStruct((t, LANES), F32),
                   jax.ShapeDtypeStruct((8, t), jnp.int32)],
        compiler_params=_cp(("parallel",)),
        name="router",
    )(x2, nw3, mod3, mod3, router_w_pad, router_b_pad)


def _deinterleave_kernel(w_ref, o_ref):
    n = MXU_DIM
    r = lax.broadcasted_iota(jnp.int32, (n, n), 0)
    c = lax.broadcasted_iota(jnp.int32, (n, n), 1)
    src = jnp.where(c < n // 2, 2 * c, 2 * (c - n // 2) + 1)
    perm = jnp.where(r == src, 1.0, 0.0).astype(BF16)
    for m in range(w_ref.shape[1] // n):
        cols = slice(m * n, (m + 1) * n)
        o_ref[:, cols] = _dot(w_ref[:, cols].astype(BF16), perm).astype(BF16)


def _deinterleave_cast(w2):
    rows, n = w2.shape
    tr = 1024
    return pl.pallas_call(
        _deinterleave_kernel,
        grid=(rows // tr,),
        in_specs=[pl.BlockSpec((tr, n), lambda i: (i, 0))],
        out_specs=pl.BlockSpec((tr, n), lambda i: (i, 0)),
        out_shape=jax.ShapeDtypeStruct((rows, n), BF16),
        compiler_params=_cp(("parallel",)),
        name="deinterleave_cast",
    )(w2)


def _expert_kernel(be_ref, nused_ref, idx_hbm, h_hbm, wu_ref, bu_ref, wd_ref, bd_ref, o_ref,
                   xbuf0, xbuf1, idx0, idx1, isems, rsems, *, rows, nb):
    b = pl.program_id(0)
    n_used = nused_ref[0]
    xbufs = (xbuf0, xbuf1)
    idxs = (idx0, idx1)

    def idx_copy(blk, s):
        return pltpu.make_async_copy(idx_hbm.at[blk], idxs[s], isems.at[s])

    def row_copy(tok, r, s):
        return pltpu.make_async_copy(h_hbm.at[pl.ds(tok, 1), :], xbufs[s].at[pl.ds(r, 1), :], rsems.at[s])

    def drain_rows(s):
        pltpu.make_async_copy(h_hbm.at[pl.ds(0, rows), :], xbufs[s], rsems.at[s]).wait()

    @pl.when(b == 0)
    def _():
        idx_copy(0, 0).start()
        idx_copy(0, 0).wait()

        def issue(r, carry):
            row_copy(idx0[r], r, 0).start()
            return carry
        lax.fori_loop(0, rows, issue, 0, unroll=8)
        idx_copy(1, 1).start()

    def block(slot):
        nxt = 1 - slot
        idx_copy(0, nxt).wait()
        drain_rows(slot)
        issued = [0]

        def issue_next(n):
            for r in range(issued[0], issued[0] + n):
                row_copy(idxs[nxt][r], r, nxt).start(priority=r % 2)
            issued[0] += n

        x = xbufs[slot][...].astype(BF16)
        half = MXU_DIM // 2
        n_up = wu_ref.shape[2] // MXU_DIM
        n_down = 4
        per_up = (rows // (n_up + n_down) + 7) // 8 * 8
        acts = []
        for m in range(n_up):
            issue_next(per_up)
            cols = slice(m * MXU_DIM, (m + 1) * MXU_DIM)
            hu = _dot(x, wu_ref[0, :, cols]) + bu_ref[0, :, cols]
            g = jnp.minimum(hu[:, :half], SWIGLU_LIMIT)
            lin = jnp.clip(hu[:, half:], -SWIGLU_LIMIT, SWIGLU_LIMIT)
            acts.append((g * _sigmoid(SWIGLU_ALPHA * g) * (lin + 1.0)).astype(BF16))
        act = jnp.concatenate(acts, axis=1)
        dn = o_ref.shape[1] // n_down
        per_down = (rows - n_up * per_up) // n_down
        for m in range(n_down):
            issue_next(per_down if m < n_down - 1 else rows - issued[0])
            cols = slice(m * dn, (m + 1) * dn)
            o_ref[:, cols] = _dot(act, wd_ref[0, :, cols]) + bd_ref[0, :, cols]
        idx_copy(b + 2, slot).start()

    for par in range(2):
        @pl.when((b < n_used) & (b % 2 == par))
        def _(par=par):
            block(par)

    @pl.when(b >= n_used)
    def _():
        o_ref[...] = jnp.zeros_like(o_ref)

    for par in range(2):
        @pl.when((b == n_used) & (b % 2 == par))
        def _(par=par):
            idx_copy(0, 1 - par).wait()
            drain_rows(par)

        @pl.when((b == nb - 1) & (b < n_used) & (b % 2 == par))
        def _(par=par):
            idx_copy(0, par).wait()
            drain_rows(1 - par)


def _experts(h2, slot_tok, block_e, n_used, wu, bu, wd, bd, layer):
    d = h2.shape[1]
    rows = MOE_ROW_BLOCK
    nb = slot_tok.shape[0] // rows
    de = D_EXPERT
    idx = jnp.concatenate([slot_tok, jnp.zeros((2 * rows,), jnp.int32)]).reshape(nb + 2, rows)

    def wmap(b, be, nu):
        return (layer * N_EXPERTS + be[b], 0, 0)

    grid_spec = pltpu.PrefetchScalarGridSpec(
        num_scalar_prefetch=2,
        grid=(nb,),
        in_specs=[
            pl.BlockSpec(memory_space=pl.ANY),
            pl.BlockSpec(memory_space=pl.ANY),
            pl.BlockSpec((1, d, 2 * de), wmap),
            pl.BlockSpec((1, 1, 2 * de), wmap),
            pl.BlockSpec((1, de, d), wmap),
            pl.BlockSpec((1, 1, d), wmap),
        ],
        out_specs=pl.BlockSpec((rows, d), lambda b, be, nu: (b, 0)),
        scratch_shapes=[pltpu.VMEM((rows, d), F32), pltpu.VMEM((rows, d), F32),
                        pltpu.SMEM((rows,), jnp.int32), pltpu.SMEM((rows,), jnp.int32),
                        pltpu.SemaphoreType.DMA((2,)), pltpu.SemaphoreType.DMA((2,))],
    )
    return pl.pallas_call(
        functools.partial(_expert_kernel, rows=rows, nb=nb),
        grid_spec=grid_spec,
        out_shape=jax.ShapeDtypeStruct((nb * rows, d), F32),
        compiler_params=_cp(("arbitrary",)),
        name="experts",
    )(block_e, n_used, idx, h2, wu, bu, wd, bd)


def _combine_kernel(idx_hbm, yb_hbm, r_ref, x_ref, nw_ref, g_ref, o_ref, ybuf, idx_smem, isems, rsems, *, tm, nt):
    i = pl.program_id(0)
    n_rows = TOP_K * tm

    def idx_copy(tile, s):
        return pltpu.make_async_copy(idx_hbm.at[tile], idx_smem.at[s], isems.at[s])

    def row_copy(r, s):
        return pltpu.make_async_copy(yb_hbm.at[pl.ds(idx_smem[s, r], 1), :], ybuf.at[s, pl.ds(r, 1), :], rsems.at[s])

    def issue_rows(s, unrolled):
        if unrolled:
            for r in range(n_rows):
                row_copy(r, s).start(priority=r % 2)
            return

        def body(r2, carry):
            for u in range(2):
                row_copy(2 * r2 + u, s).start(priority=u)
            return carry
        lax.fori_loop(0, n_rows // 2, body, 0, unroll=4)

    def wait_rows(s):
        pltpu.make_async_copy(yb_hbm.at[pl.ds(0, n_rows), :], ybuf.at[s], rsems.at[s]).wait()

    def tile(slot):
        nxt = 1 - slot

        @pl.when(i + 1 < nt)
        def _():
            idx_copy(0, nxt).wait()
            issue_rows(nxt, True)

        @pl.when(i + 2 < nt)
        def _():
            idx_copy(i + 2, slot).start()

        wait_rows(slot)
        gates = r_ref[...]
        y = None
        for k in range(TOP_K):
            term = gates[:, k:k + 1] * ybuf[slot, k * tm:(k + 1) * tm, :]
            y = term if y is None else y + term
        o_ref[...] = x_ref[...] + g_ref[0] * _rms_rows(y, nw_ref[0])

    @pl.when(i == 0)
    def _():
        idx_copy(0, 0).start()
        idx_copy(0, 0).wait()
        issue_rows(0, False)

        @pl.when(nt > 1)
        def _():
            idx_copy(1, 1).start()

    @pl.when(i % 2 == 0)
    def _():
        tile(0)

    @pl.when(i % 2 == 1)
    def _():
        tile(1)


def _combine(yb, inv_slot, route, x2, nw3, mod3, seq, *, nw_idx, g_idx):
    t, d = x2.shape
    tm = min(256, seq)
    nt = t // tm
    per_b = seq // tm
    idx_tiles = inv_slot.reshape(TOP_K, nt, tm).transpose(1, 0, 2).reshape(nt, TOP_K * tm)
    return pl.pallas_call(
        functools.partial(_combine_kernel, tm=tm, nt=nt),
        grid=(nt,),
        in_specs=[
            pl.BlockSpec(memory_space=pl.ANY),
            pl.BlockSpec(memory_space=pl.ANY),
            pl.BlockSpec((tm, LANES), lambda i: (i, 0)),
            pl.BlockSpec((tm, d), lambda i: (i, 0)),
            pl.BlockSpec((1, 1, d), lambda i: (nw_idx, 0, 0)),
            pl.BlockSpec((1, 1, d), lambda i: (i // per_b, 0, g_idx)),
        ],
        out_specs=pl.BlockSpec((tm, d), lambda i: (i, 0)),
        out_shape=jax.ShapeDtypeStruct((t, d), F32),
        scratch_shapes=[pltpu.VMEM((2, TOP_K * tm, d), F32), pltpu.SMEM((2, TOP_K * tm), jnp.int32),
                        pltpu.SemaphoreType.DMA((2,)), pltpu.SemaphoreType.DMA((2,))],
        compiler_params=_cp(("arbitrary",)),
        name="combine",
    )(idx_tiles, yb, route, x2, nw3, mod3)


_W_IN_SEGMENTS = (
    (0, C_AQ, 3072),
    (3080, C_MO, 512),
    (3592, C_GQ, 1024),
    (4632, C_GR, 512),
    (5144, C_FQ, 1536),
    (6684, C_GATE, 8192),
)
_W_IN_SMALL = ((3072, SM_I, 8), (4616, SM_GA, GLA_RANK), (6680, SM_FF, BRANCH_HEADS))


def _reorder_kernel(w_ref, o_ref):
    rows = w_ref.shape[1]
    for src, dst, n in _W_IN_SEGMENTS:
        o_ref[0, :, dst:dst + n] = w_ref[0, :, src:src + n].astype(BF16)
    r = lax.broadcasted_iota(jnp.int32, (LANES, LANES), 0)
    c = lax.broadcasted_iota(jnp.int32, (LANES, LANES), 1)
    small = jnp.zeros((rows, LANES), F32)
    for src, lane0, n in _W_IN_SMALL:
        base = src // LANES * LANES
        sel = jnp.where((c >= lane0) & (c < lane0 + n) & (r == c - lane0 + (src - base)), 1.0, 0.0).astype(BF16)
        small = small + _dot(w_ref[0, :, base:base + LANES].astype(BF16), sel)
    o_ref[0, :, C_SMALL:C_SMALL + LANES] = small.astype(BF16)
    o_ref[0, :, C_SMALL + LANES:] = jnp.zeros((rows, N_PROJ - C_SMALL - LANES), BF16)


def _reorder_w_in(w_in):
    depth, d, n = w_in.shape
    tr = 256
    return pl.pallas_call(
        _reorder_kernel,
        grid=(depth, d // tr),
        in_specs=[pl.BlockSpec((1, tr, n), lambda l, i: (l, i, 0))],
        out_specs=pl.BlockSpec((1, tr, N_PROJ), lambda l, i: (l, i, 0)),
        out_shape=jax.ShapeDtypeStruct((depth, d, N_PROJ), BF16),
        compiler_params=_cp(("parallel", "parallel")),
        name="reorder_w_in",
    )(w_in)


def _routing_tables(experts_t, t):
    tk = t * TOP_K
    rb = MOE_ROW_BLOCK
    flat_e = experts_t[:TOP_K].reshape(-1)
    flat_ids = jnp.arange(tk, dtype=jnp.int32)
    e_sorted, order = lax.sort((flat_e, flat_ids), num_keys=1, is_stable=True)
    experts = jnp.arange(N_EXPERTS, dtype=jnp.int32)
    counts = jnp.sum((flat_e[:, None] == experts[None, :]).astype(jnp.int32), axis=0)
    padded = (counts + rb - 1) // rb * rb
    pad_end = jnp.cumsum(padded)
    pad_start = pad_end - padded
    start = jnp.cumsum(counts) - counts
    slot_sorted = pad_start[e_sorted] + flat_ids - start[e_sorted]
    _, inv_slot = lax.sort((order, slot_sorted), num_keys=1)
    n_blocks = -(-tk // rb) + N_EXPERTS
    blk_first = jnp.arange(n_blocks, dtype=jnp.int32) * rb
    block_e = jnp.minimum(jnp.sum((pad_end[None, :] <= blk_first[:, None]).astype(jnp.int32), axis=1),
                          N_EXPERTS - 1).astype(jnp.int32)
    pos = jnp.arange(n_blocks * rb, dtype=jnp.int32)
    e_pos = jnp.repeat(block_e, rb)
    r_pos = pos - pad_start[e_pos]
    src = jnp.clip(start[e_pos] + r_pos, 0, tk - 1)
    slot_tok = jnp.where(r_pos < counts[e_pos], order[src] % t, 0).astype(jnp.int32)
    n_used = (pad_end[-1] // rb).astype(jnp.int32).reshape(1)
    return slot_tok, inv_slot.astype(jnp.int32), block_e, n_used


def _rope_tables(seq):
    half = HEAD_DIM // 2
    inv = ROPE_THETA ** (-jnp.arange(half, dtype=F32) / half)
    ang = jnp.arange(seq).astype(F32)[:, None] * inv[None, :]
    cos, sin = jnp.cos(ang), jnp.sin(ang)
    return jnp.concatenate([cos, cos], axis=-1), jnp.concatenate([-sin, sin], axis=-1)


def _deinterleave_bias(b_up):
    depth, e, n = b_up.shape
    half = MXU_DIM // 2
    return b_up.reshape(depth, e, n // MXU_DIM, half, 2).transpose(0, 1, 2, 4, 3).reshape(depth, e, 1, n)


def kernel(x, c, ada_w, ada_b, norm_w, w_in, conv_w, conv_b, mlstm_i_b, mlstm_f_b, mlstm_norm_w, gla_w_a2, gla_b_a2, gla_norm_w, fox_f_b, w_branch, w_out, router_w, router_b, w_up, b_up, w_down, b_down):
    batch, seq, d = x.shape
    depth = ada_w.shape[0]
    t = batch * seq
    x2 = x.reshape(t, d)
    mod = _ada_mod(c, ada_w, ada_b)
    cos, sin = _rope_tables(seq)
    w_up_d = _deinterleave_cast(w_up.reshape(depth * N_EXPERTS * d, 2 * D_EXPERT))
    w_up_d = w_up_d.reshape(depth * N_EXPERTS, d, 2 * D_EXPERT)
    b_up_d = _deinterleave_bias(b_up).reshape(depth * N_EXPERTS, 1, 2 * D_EXPERT)
    w_down_b = w_down.astype(BF16).reshape(depth * N_EXPERTS, D_EXPERT, d)
    b_down_r = b_down.reshape(depth * N_EXPERTS, 1, d)
    w_branch_b = w_branch.astype(BF16).reshape(depth * N_BRANCH, BRANCH_WIDTH, d)
    w_out_b = w_out.astype(BF16)
    w_in_r = _reorder_w_in(w_in)

    for l in range(depth):
        mod3 = mod[l].reshape(batch, 1, 6 * d)
        nw3 = norm_w[l].reshape(4, 1, d)

        proj = _in_proj(x2, mod3, nw3, w_in_r, l, seq, nw_idx=0, sc_idx=1, sh_idx=0)
        small_bias = jnp.zeros((1, LANES), F32)
        small_bias = small_bias.at[0, SM_I:SM_I + BRANCH_HEADS].set(mlstm_i_b[l])
        small_bias = small_bias.at[0, SM_F:SM_F + BRANCH_HEADS].set(mlstm_f_b[l])
        small_bias = small_bias.at[0, SM_FF:SM_FF + BRANCH_HEADS].set(fox_f_b[l])
        w_a2_pad = jnp.zeros((LANES, GLA_HEADS * GLA_DK), F32).at[SM_GA:SM_GA + GLA_RANK].set(gla_w_a2[l])

        y_a = _moba(proj, cos, sin, batch, seq)
        y_b = _mlstm(proj, small_bias, conv_w[l], conv_b[l], mlstm_norm_w[l], batch, seq)
        y_c = _gla(proj, w_a2_pad, gla_b_a2[l], gla_norm_w[l], batch, seq)
        y_d = _fox(proj, small_bias, batch, seq)
        merged = _merge((y_a, y_b, y_c, y_d), proj, w_branch_b, l)
        x2 = _out_proj(merged, w_out_b, l, x2, nw3, mod3, seq, nw_idx=1, g_idx=2)

        rw_pad = jnp.zeros((d, LANES), F32).at[:, :N_EXPERTS].set(router_w[l])
        rb_pad = jnp.zeros((1, LANES), F32).at[0, :N_EXPERTS].set(router_b[l])
        h2, route, experts_t = _router(x2, nw3, mod3, rw_pad, rb_pad, seq, nw_idx=2, sc_idx=4, sh_idx=3)
        slot_tok, inv_slot, block_e, n_used = _routing_tables(experts_t, t)
        yb = _experts(h2, slot_tok, block_e, n_used, w_up_d, b_up_d, w_down_b, b_down_r, l)
        x2 = _combine(yb, inv_slot, route, x2, nw3, mod3, seq, nw_idx=3, g_idx=5)

    return x2.reshape(batch, seq, d)
```
